```python
import math
import jax, jax.numpy as jnp
from jax import lax
import numpy as np


D_MODEL = 2048
BATCH = 2
SEQ = 16384
DEPTH = 4

N_BRANCH = 4
BRANCH_WIDTH = 512
HEAD_DIM = 128
DSA_HEADS = 4
IDX_HEADS = 8
IDX_DIM = 64
TOPK_MAX = 256
S5_WIDTH = 512
S5_GROUP = 16
S5_GROUPS = S5_WIDTH // S5_GROUP
S5_STATE = 64
S5_DT_MIN = 1e-3
S5_DT_MAX = 1e-1
DIFF_HEADS = 4
DIFF_QK_DIM = 64
DIFF_V_DIM = 128
MLSTM_HEADS = 4
MLSTM_DIM = 128
MLSTM_CHUNK = 64
CONV_WIDTH = 4
D_FF = 5504
REL_BUCKETS = 32
REL_MAX_DIST = 128
N_BIAS_HEADS = DSA_HEADS + DIFF_HEADS
Q_BLOCK = 128
EPS = 1e-6

IN_SPLITS = (
    DSA_HEADS * HEAD_DIM,
    DSA_HEADS * HEAD_DIM,
    DSA_HEADS * HEAD_DIM,
    IDX_HEADS * IDX_DIM,
    IDX_DIM,
    IDX_HEADS,
    S5_WIDTH,
    DIFF_HEADS * 2 * DIFF_QK_DIM,
    DIFF_HEADS * 2 * DIFF_QK_DIM,
    DIFF_HEADS * DIFF_V_DIM,
    2 * MLSTM_HEADS * MLSTM_DIM,
    MLSTM_HEADS * MLSTM_DIM,
    MLSTM_HEADS * MLSTM_DIM,
    MLSTM_HEADS,
    MLSTM_HEADS,
)
IN_COLS = sum(IN_SPLITS)

kernel_name = 'hybrid_gated_dsa_s5_diff_mlstm_macaron'


def rmsnorm(x, g):
    xf = x.astype(jnp.float32)
    y = xf * lax.rsqrt(jnp.mean(xf * xf, axis=-1, keepdims=True) + EPS)
    return (y * g.astype(jnp.float32)).astype(x.dtype)


def swiglu(x, w_in, w_out):
    g, u = jnp.split(x @ w_in, 2, axis=-1)
    return (jax.nn.silu(g) * u) @ w_out


def rel_bucket(dist):
    max_exact = REL_BUCKETS // 2
    n = jnp.maximum(dist, 0)
    large = max_exact + (jnp.log(jnp.maximum(n, 1).astype(jnp.float32) / max_exact)
                         / math.log(REL_MAX_DIST / max_exact)
                         * (REL_BUCKETS - max_exact)).astype(jnp.int32)
    large = jnp.minimum(large, REL_BUCKETS - 1)
    return jnp.where(n < max_exact, n, large)


def to_blocks(t):
    b, s = t.shape[:2]
    return jnp.moveaxis(t.reshape((b, s // Q_BLOCK, Q_BLOCK) + t.shape[2:]), 1, 0)


def from_blocks(t):
    t = jnp.moveaxis(t, 0, 1)
    return t.reshape((t.shape[0], t.shape[1] * t.shape[2]) + t.shape[3:])


def causal_conv(x, w):
    s = x.shape[1]
    xp = jnp.pad(x, ((0, 0), (CONV_WIDTH - 1, 0), (0, 0)))
    out = xp[:, 0:s] * w[0]
    for j in range(1, CONV_WIDTH):
        out = out + xp[:, j:j + s] * w[j]
    return out


def dsa_attention(q, k, v, iq, ik, iw, rel_a):
    f32 = jnp.float32
    bsz, s, h, dh = q.shape
    topk = min(TOPK_MAX, s // 4)
    spos = jnp.arange(s)
    scale = dh ** -0.5
    idx_scale = (IDX_DIM * IDX_HEADS) ** -0.5

    def block(args):
        qb, iqb, iwb, t0 = args
        tpos = t0 + jnp.arange(Q_BLOCK)
        rel = jax.nn.relu(jnp.einsum('bqhd,bsd->bqhs', iqb, ik).astype(f32))
        score = jnp.einsum('bqh,bqhs->bqs', iwb.astype(f32), rel) * idx_scale
        score = jnp.where(spos[None, None, :] <= tpos[None, :, None], score, -jnp.inf)
        _, sel = lax.top_k(score, topk)
        valid = sel <= tpos[None, :, None]
        kg = jax.vmap(lambda kk, ii: kk[ii])(k, sel)
        vg = jax.vmap(lambda vv, ii: vv[ii])(v, sel)
        logits = jnp.einsum('bqhd,bqkhd->bqhk', qb, kg).astype(f32) * scale
        bias = rel_a[rel_bucket(tpos[None, :, None] - sel)]
        logits = logits + jnp.swapaxes(bias, -1, -2).astype(f32)
        logits = jnp.where(valid[:, :, None, :], logits, -jnp.inf)
        p = jax.nn.softmax(logits, axis=-1)
        return jnp.einsum('bqhk,bqkhd->bqhd', p.astype(v.dtype), vg)

    t0s = jnp.arange(s // Q_BLOCK) * Q_BLOCK
    out = lax.map(block, (to_blocks(q), to_blocks(iq), to_blocks(iw), t0s))
    return from_blocks(out).reshape(bsz, s, h * dh)


def s5_combine(e1, e2):
    a1r, a1i, b1r, b1i = e1
    a2r, a2i, b2r, b2i = e2
    return (a2r * a1r - a2i * a1i, a2r * a1i + a2i * a1r,
            a2r * b1r - a2i * b1i + b2r, a2r * b1i + a2i * b1r + b2i)


def s5_mixer(u, a_re, a_im, log_dt, b_re, b_im, c_re, c_im, d_skip, w_glu):
    f32 = jnp.float32
    bsz, s, _ = u.shape
    uf = u.astype(f32).reshape(bsz, s, S5_GROUPS, S5_GROUP)
    ar, ai = a_re.astype(f32), a_im.astype(f32)
    dt = jnp.exp(log_dt.astype(f32))[:, None]
    mag = jnp.exp(ar * dt)
    ab_re, ab_im = mag * jnp.cos(ai * dt), mag * jnp.sin(ai * dt)
    den = ar * ar + ai * ai
    nr, ni = ab_re - 1.0, ab_im
    coef_re = (nr * ar + ni * ai) / den
    coef_im = (ni * ar - nr * ai) / den
    br, bi = b_re.astype(f32), b_im.astype(f32)
    bb_re = coef_re[..., None] * br - coef_im[..., None] * bi
    bb_im = coef_re[..., None] * bi + coef_im[..., None] * br
    bu_re = jnp.einsum('gpc,bsgc->bsgp', bb_re, uf)
    bu_im = jnp.einsum('gpc,bsgc->bsgp', bb_im, uf)
    a_full_re = jnp.broadcast_to(ab_re, bu_re.shape)
    a_full_im = jnp.broadcast_to(ab_im, bu_re.shape)
    _, _, x_re, x_im = lax.associative_scan(s5_combine, (a_full_re, a_full_im, bu_re, bu_im), axis=1)
    y = (jnp.einsum('gcp,bsgp->bsgc', c_re.astype(f32), x_re)
         - jnp.einsum('gcp,bsgp->bsgc', c_im.astype(f32), x_im))
    y = y.reshape(bsz, s, S5_WIDTH) + d_skip.astype(f32) * u.astype(f32)
    z = jax.nn.gelu(y).astype(u.dtype)
    return z * jax.nn.sigmoid(z @ w_glu)


def diff_attention(q, k, v, rel_c, lam, lambda_init, subln_g):
    f32 = jnp.float32
    bsz, s, h, _, dq = q.shape
    scale = dq ** -0.5
    spos = jnp.arange(s)

    def block(args):
        qb, t0 = args
        tpos = t0 + jnp.arange(Q_BLOCK)
        logits = jnp.einsum('bqhmd,bshmd->bhmqs', qb, k).astype(f32) * scale
        bias = rel_c[rel_bucket(tpos[:, None] - spos[None, :])]
        logits = logits + jnp.transpose(bias, (2, 0, 1))[None, :, None].astype(f32)
        logits = jnp.where(spos[None, :] <= tpos[:, None], logits, -jnp.inf)
        p = jax.nn.softmax(logits, axis=-1)
        a = p[:, :, 0] - lam * p[:, :, 1]
        return jnp.einsum('bhqs,bshd->bqhd', a.astype(v.dtype), v)

    t0s = jnp.arange(s // Q_BLOCK) * Q_BLOCK
    out = from_blocks(lax.map(block, (to_blocks(q), t0s)))
    out = rmsnorm(out, subln_g) * (1.0 - lambda_init)
    return out.reshape(bsz, s, h * v.shape[-1])


def mlstm_chunkwise(q, k, v, i_pre, f_pre):
    bsz, s, h, d = q.shape
    nc = s // MLSTM_CHUNK

    def to_chunks(t):
        return t.reshape(bsz, nc, MLSTM_CHUNK, h, -1).transpose(0, 3, 1, 2, 4)

    def gate_chunks(t):
        return t.reshape(bsz, nc, MLSTM_CHUNK, h).transpose(0, 3, 1, 2)

    q, k, v = to_chunks(q), to_chunks(k) * (d ** -0.5), to_chunks(v)
    ig = gate_chunks(i_pre)
    bcum = jnp.cumsum(gate_chunks(jax.nn.log_sigmoid(f_pre)), axis=-1)
    b_last = bcum[..., -1]
    g = b_last[..., None] - bcum + ig
    m_loc = jnp.max(g, axis=-1)
    w = jnp.exp(g - m_loc[..., None])
    c_loc = jnp.einsum('bhcl,bhcld,bhcle->bhcde', w, k, v)
    n_loc = jnp.einsum('bhcl,bhcld->bhcd', w, k)

    def step(carry, inp):
        c_st, n_st, m_st = carry
        bl, ml, cl, nl = inp
        m_new = jnp.maximum(bl + m_st, ml)
        a = jnp.exp(bl + m_st - m_new)
        bcoef = jnp.exp(ml - m_new)
        c_new = a[..., None, None] * c_st + bcoef[..., None, None] * cl
        n_new = a[..., None] * n_st + bcoef[..., None] * nl
        return (c_new, n_new, m_new), (c_st, n_st, m_st)

    init = (jnp.zeros((bsz, h, d, d), jnp.float32), jnp.zeros((bsz, h, d), jnp.float32),
            jnp.zeros((bsz, h), jnp.float32))
    xs = (jnp.moveaxis(b_last, 2, 0), jnp.moveaxis(m_loc, 2, 0),
          jnp.moveaxis(c_loc, 2, 0), jnp.moveaxis(n_loc, 2, 0))
    _, (c_prev, n_prev, m_prev) = lax.scan(step, init, xs)
    c_prev = jnp.moveaxis(c_prev, 0, 2)
    n_prev = jnp.moveaxis(n_prev, 0, 2)
    m_prev = jnp.moveaxis(m_prev, 0, 2)

    causal = jnp.tril(jnp.ones((MLSTM_CHUNK, MLSTM_CHUNK), dtype=bool))
    log_d = bcum[..., :, None] - bcum[..., None, :] + ig[..., None, :]
    log_d = jnp.where(causal, log_d, -jnp.inf)
    m_inter = bcum + m_prev[..., None]
    m_j = jnp.maximum(jnp.max(log_d, axis=-1), m_inter)
    dqk = jnp.exp(log_d - m_j[..., None]) * jnp.einsum('bhcjd,bhcsd->bhcjs', q, k)
    inter = jnp.exp(m_inter - m_j)
    num = (jnp.einsum('bhcjs,bhcse->bhcje', dqk, v)
           + inter[..., None] * jnp.einsum('bhcjd,bhcde->bhcje', q, c_prev))
    den = jnp.sum(dqk, axis=-1) + inter * jnp.einsum('bhcjd,bhcd->bhcj', q, n_prev)
    hid = num / jnp.maximum(jnp.abs(den), jnp.exp(-m_j))[..., None]
    return hid.transpose(0, 2, 3, 1, 4).reshape(bsz, s, h * d)


def hybrid_mixer(xn, w_in, conv_w, i_bias, f_bias, s5_a_re, s5_a_im, s5_log_dt, s5_b_re, s5_b_im,
                 s5_c_re, s5_c_im, s5_d, s5_w_glu, diff_lambda, diff_subln_g, lambda_init,
                 rel_table, w_gate, b_gate, w_branch, w_out):
    f32 = jnp.float32
    bsz, s, _ = xn.shape
    h = xn @ w_in
    split_points = np.cumsum(IN_SPLITS)[:-1].tolist()
    (a_q, a_k, a_v, a_iq, a_ik, a_iw, b_u, c_q, c_k, c_v,
     d_qk, d_v, d_o, d_i, d_f) = jnp.split(h, split_points, axis=-1)

    y_a = dsa_attention(a_q.reshape(bsz, s, DSA_HEADS, HEAD_DIM), a_k.reshape(bsz, s, DSA_HEADS, HEAD_DIM),
                        a_v.reshape(bsz, s, DSA_HEADS, HEAD_DIM), a_iq.reshape(bsz, s, IDX_HEADS, IDX_DIM),
                        a_ik, a_iw, rel_table[:, :DSA_HEADS])
    y_b = s5_mixer(b_u, s5_a_re, s5_a_im, s5_log_dt, s5_b_re, s5_b_im, s5_c_re, s5_c_im, s5_d, s5_w_glu)
    dl = diff_lambda.astype(f32)
    lam = jnp.exp(jnp.sum(dl[0] * dl[1])) - jnp.exp(jnp.sum(dl[2] * dl[3])) + lambda_init
    y_c = diff_attention(c_q.reshape(bsz, s, DIFF_HEADS, 2, DIFF_QK_DIM),
                         c_k.reshape(bsz, s, DIFF_HEADS, 2, DIFF_QK_DIM),
                         c_v.reshape(bsz, s, DIFF_HEADS, DIFF_V_DIM),
                         rel_table[:, DSA_HEADS:], lam, lambda_init, diff_subln_g)
    qk = jax.nn.silu(causal_conv(d_qk, conv_w))
    m_q, m_k = jnp.split(qk.astype(f32), 2, axis=-1)
    hm = mlstm_chunkwise(m_q.reshape(bsz, s, MLSTM_HEADS, MLSTM_DIM), m_k.reshape(bsz, s, MLSTM_HEADS, MLSTM_DIM),
                         d_v.astype(f32).reshape(bsz, s, MLSTM_HEADS, MLSTM_DIM),
                         d_i.astype(f32) + i_bias.astype(f32), d_f.astype(f32) + f_bias.astype(f32))
    y_d = (jax.nn.sigmoid(d_o.astype(f32)) * hm).astype(xn.dtype)

    ys = (y_a, y_b, y_c, y_d)
    merged = jax.nn.sigmoid(xn @ w_gate[0] + b_gate[0]) * (ys[0] @ w_branch[0])
    for n in range(1, N_BRANCH):
        merged = merged + jax.nn.sigmoid(xn @ w_gate[n] + b_gate[n]) * (ys[n] @ w_branch[n])
    return merged @ w_out


def setup_inputs(seed: int = 0) -> dict:
    key = jax.random.key(seed)
    ks = jax.random.split(key, 26)
    f32 = jnp.float32

    def nrm(k, shape, scale):
        return jax.random.normal(k, shape, f32) * scale

    return {
        'x': nrm(ks[0], (BATCH, SEQ, D_MODEL), 1.0),
        'norm_g': 1.0 + nrm(ks[1], (DEPTH, 3, D_MODEL), 0.02),
        'w_ffn_in': nrm(ks[2], (DEPTH, 2, D_MODEL, 2 * D_FF), D_MODEL ** -0.5),
        'w_ffn_out': nrm(ks[3], (DEPTH, 2, D_FF, D_MODEL), D_FF ** -0.5),
        'w_in': nrm(ks[4], (DEPTH, D_MODEL, IN_COLS), D_MODEL ** -0.5),
        'conv_w': nrm(ks[5], (DEPTH, CONV_WIDTH, 2 * MLSTM_HEADS * MLSTM_DIM), CONV_WIDTH ** -0.5),
        'mlstm_i_bias': nrm(ks[6], (DEPTH, MLSTM_HEADS), 0.1),
        'mlstm_f_bias': jnp.linspace(3.0, 6.0, MLSTM_HEADS, dtype=f32)[None] + nrm(ks[7], (DEPTH, MLSTM_HEADS), 0.1),
        's5_a_re': -0.5 + nrm(ks[8], (DEPTH, S5_GROUPS, S5_STATE), 0.01),
        's5_a_im': math.pi * jnp.arange(S5_STATE, dtype=f32) + nrm(ks[9], (DEPTH, S5_GROUPS, S5_STATE), 0.01),
        's5_log_dt': jax.random.uniform(ks[10], (DEPTH, S5_GROUPS), f32, math.log(S5_DT_MIN), math.log(S5_DT_MAX)),
        's5_b_re': nrm(ks[11], (DEPTH, S5_GROUPS, S5_STATE, S5_GROUP), (2 * S5_GROUP) ** -0.5),
        's5_b_im': nrm(ks[12], (DEPTH, S5_GROUPS, S5_STATE, S5_GROUP), (2 * S5_GROUP) ** -0.5),
        's5_c_re': nrm(ks[13], (DEPTH, S5_GROUPS, S5_GROUP, S5_STATE), S5_STATE ** -0.5),
        's5_c_im': nrm(ks[14], (DEPTH, S5_GROUPS, S5_GROUP, S5_STATE), S5_STATE ** -0.5),
        's5_d': nrm(ks[15], (DEPTH, S5_WIDTH), 1.0),
        's5_w_glu': nrm(ks[16], (DEPTH, S5_WIDTH, S5_WIDTH), S5_WIDTH ** -0.5),
        'diff_lambda': nrm(ks[17], (DEPTH, 4, DIFF_QK_DIM), 0.1),
        'diff_subln_g': 1.0 + nrm(ks[18], (DEPTH, DIFF_V_DIM), 0.02),
        'rel_table': nrm(ks[19], (REL_BUCKETS, N_BIAS_HEADS), 0.5),
        'w_gate': nrm(ks[20], (DEPTH, N_BRANCH, D_MODEL, D_MODEL), D_MODEL ** -0.5),
        'b_gate': nrm(ks[21], (DEPTH, N_BRANCH, D_MODEL), 0.02),
        'w_branch': nrm(ks[22], (DEPTH, N_BRANCH, BRANCH_WIDTH, D_MODEL), BRANCH_WIDTH ** -0.5),
        'w_out': nrm(ks[23], (DEPTH, D_MODEL, D_MODEL), D_MODEL ** -0.5),
        'final_g': 1.0 + nrm(ks[24], (D_MODEL,), 0.02),
    }


def reference(x, norm_g, w_ffn_in, w_ffn_out, w_in, conv_w, mlstm_i_bias, mlstm_f_bias,
              s5_a_re, s5_a_im, s5_log_dt, s5_b_re, s5_b_im, s5_c_re, s5_c_im, s5_d, s5_w_glu,
              diff_lambda, diff_subln_g, rel_table, w_gate, b_gate, w_branch, w_out, final_g):
    for l in range(DEPTH):
        lambda_init = 0.8 - 0.6 * math.exp(-0.3 * l)
        x = x + 0.5 * swiglu(rmsnorm(x, norm_g[l, 0]), w_ffn_in[l, 0], w_ffn_out[l, 0])
        x = x + hybrid_mixer(rmsnorm(x, norm_g[l, 1]), w_in[l], conv_w[l], mlstm_i_bias[l], mlstm_f_bias[l],
                             s5_a_re[l], s5_a_im[l], s5_log_dt[l], s5_b_re[l], s5_b_im[l],
                             s5_c_re[l], s5_c_im[l], s5_d[l], s5_w_glu[l], diff_lambda[l], diff_subln_g[l],
                             lambda_init, rel_table, w_gate[l], b_gate[l], w_branch[l], w_out[l])
        x = x + 0.5 * swiglu(rmsnorm(x, norm_g[l, 2]), w_ffn_in[l, 1], w_ffn_out[l, 1])
    return rmsnorm(x, final_g)
```

```python
import functools
import math

import jax
import jax.numpy as jnp
from jax import lax
from jax.experimental import pallas as pl
from jax.experimental.pallas import tpu as pltpu

F32 = jnp.float32
BF16 = jnp.bfloat16

D_MODEL = 2048
DEPTH = 4
N_BRANCH = 4
BRANCH_WIDTH = 512
HEAD_DIM = 128
DSA_HEADS = 4
IDX_HEADS = 8
IDX_DIM = 64
TOPK_MAX = 256
S5_WIDTH = 512
S5_GROUP = 16
S5_GROUPS = S5_WIDTH // S5_GROUP
S5_STATE = 64
DIFF_HEADS = 4
DIFF_QK_DIM = 64
DIFF_V_DIM = 128
MLSTM_HEADS = 4
MLSTM_DIM = 128
CONV_WIDTH = 4
D_FF = 5504
REL_BUCKETS = 32
REL_MAX_DIST = 128
EPS = 1e-6

LANES = 128
NEG = -1e30
INT_MIN = -(2 ** 31)
D_FF_PAD = 5632
MLSTM_L = 128

NT_DIMS = (((1,), (1,)), ((), ()))


def _params(sem, vmem_mb):
    return pltpu.CompilerParams(dimension_semantics=sem, vmem_limit_bytes=vmem_mb * 1024 * 1024)


def _rms(x, g):
    return x * lax.rsqrt(jnp.mean(x * x, axis=-1, keepdims=True) + EPS) * g


def _rmsnorm_kernel(x_ref, g_ref, o_ref):
    o_ref[...] = _rms(x_ref[...], g_ref[...]).astype(o_ref.dtype)


def rmsnorm(x, g, out_dtype, tm=512):
    m, d = x.shape
    return pl.pallas_call(
        _rmsnorm_kernel,
        grid=(m // tm,),
        in_specs=[pl.BlockSpec((tm, d), lambda i: (i, 0)), pl.BlockSpec((1, d), lambda i: (0, 0))],
        out_specs=pl.BlockSpec((tm, d), lambda i: (i, 0)),
        out_shape=jax.ShapeDtypeStruct((m, d), out_dtype),
        compiler_params=_params(("parallel",), 40),
    )(x, g.reshape(1, d))


def _mm_kernel(x_ref, w_ref, o_ref):
    o_ref[...] = jnp.dot(x_ref[...], w_ref[...], preferred_element_type=F32).astype(o_ref.dtype)


def matmul(x, w, out_dtype, tm=1024, tn=512):
    m, k = x.shape
    n = w.shape[1]
    return pl.pallas_call(
        _mm_kernel,
        grid=(m // tm, n // tn),
        in_specs=[pl.BlockSpec((tm, k), lambda i, j: (i, 0)), pl.BlockSpec((k, tn), lambda i, j: (0, j))],
        out_specs=pl.BlockSpec((tm, tn), lambda i, j: (i, j)),
        out_shape=jax.ShapeDtypeStruct((m, n), out_dtype),
        compiler_params=_params(("parallel", "arbitrary"), 40),
    )(x, w)


def _mm_res_kernel(x_ref, w_ref, r_ref, o_ref):
    o_ref[...] = r_ref[...] + jnp.dot(x_ref[...], w_ref[...], preferred_element_type=F32)


def matmul_residual(x, w, r, tm=1024, tn=512):
    m, k = x.shape
    n = w.shape[1]
    return pl.pallas_call(
        _mm_res_kernel,
        grid=(m // tm, n // tn),
        in_specs=[pl.BlockSpec((tm, k), lambda i, j: (i, 0)), pl.BlockSpec((k, tn), lambda i, j: (0, j)),
                  pl.BlockSpec((tm, tn), lambda i, j: (i, j))],
        out_specs=pl.BlockSpec((tm, tn), lambda i, j: (i, j)),
        out_shape=jax.ShapeDtypeStruct((m, n), F32),
        compiler_params=_params(("parallel", "arbitrary"), 40),
    )(x, w, r)


def _misc_proj_kernel(x_ref, w_ref, ikz_ref, misc_ref):
    acc = jnp.dot(x_ref[...], w_ref[...], preferred_element_type=F32)
    ikz_ref[...] = acc[:, :2 * LANES].astype(BF16)
    misc_ref[...] = acc[:, 2 * LANES:]


def misc_proj(xn, w, tm=1024):
    m, k = xn.shape
    n = w.shape[1]
    return pl.pallas_call(
        _misc_proj_kernel,
        grid=(m // tm,),
        in_specs=[pl.BlockSpec((tm, k), lambda i: (i, 0)), pl.BlockSpec((k, n), lambda i: (0, 0))],
        out_specs=[pl.BlockSpec((tm, 2 * LANES), lambda i: (i, 0)), pl.BlockSpec((tm, LANES), lambda i: (i, 0))],
        out_shape=[jax.ShapeDtypeStruct((m, 2 * LANES), BF16), jax.ShapeDtypeStruct((m, LANES), F32)],
        compiler_params=_params(("parallel",), 40),
    )(xn, w)


def _ffn_kernel(x_ref, g_ref, wg_ref, wu_ref, wo_ref, o_ref, xn_ref, acc_ref):
    j = pl.program_id(1)

    @pl.when(j == 0)
    def _():
        xn_ref[...] = _rms(x_ref[...], g_ref[...]).astype(BF16)
        acc_ref[...] = jnp.zeros_like(acc_ref)

    xn = xn_ref[...]
    g = jnp.dot(xn, wg_ref[...], preferred_element_type=F32)
    u = jnp.dot(xn, wu_ref[...], preferred_element_type=F32)
    a = (g * jax.nn.sigmoid(g) * u).astype(BF16)
    acc_ref[...] += jnp.dot(a, wo_ref[...], preferred_element_type=F32)

    @pl.when(j == pl.num_programs(1) - 1)
    def _():
        o_ref[...] = x_ref[...] + 0.5 * acc_ref[...]


def ffn_block(x, g, w_in_pad, w_out_pad, tm=512, tf=512):
    m, d = x.shape
    nf = w_out_pad.shape[0] // tf
    return pl.pallas_call(
        _ffn_kernel,
        grid=(m // tm, nf),
        in_specs=[pl.BlockSpec((tm, d), lambda i, j: (i, 0)),
                  pl.BlockSpec((1, d), lambda i, j: (0, 0)),
                  pl.BlockSpec((d, tf), lambda i, j: (0, j)),
                  pl.BlockSpec((d, tf), lambda i, j: (0, j + nf)),
                  pl.BlockSpec((tf, d), lambda i, j: (j, 0))],
        out_specs=pl.BlockSpec((tm, d), lambda i, j: (i, 0)),
        out_shape=jax.ShapeDtypeStruct((m, d), F32),
        scratch_shapes=[pltpu.VMEM((tm, d), BF16), pltpu.VMEM((tm, d), F32)],
        compiler_params=_params(("parallel", "arbitrary"), 48),
    )(x, g.reshape(1, d), w_in_pad, w_in_pad, w_out_pad)


def _merge_kernel(xn_ref, ya_ref, yb_ref, yc_ref, yd_ref, wg_ref, bg_ref, wb_ref, o_ref):
    xn = xn_ref[...]
    acc = None
    for n, y_ref in enumerate((ya_ref, yb_ref, yc_ref, yd_ref)):
        gate = jnp.dot(xn, wg_ref[n], preferred_element_type=F32) + bg_ref[n]
        proj = jnp.dot(y_ref[...], wb_ref[n], preferred_element_type=F32)
        term = jax.nn.sigmoid(gate) * proj
        acc = term if acc is None else acc + term
    o_ref[...] = acc.astype(o_ref.dtype)


def gated_merge(xn, ys, w_gate, b_gate, w_branch, tm=1024, tn=256):
    m, d = xn.shape
    bw = ys[0].shape[1]
    y_spec = pl.BlockSpec((tm, bw), lambda i, j: (i, 0))
    return pl.pallas_call(
        _merge_kernel,
        grid=(m // tm, d // tn),
        in_specs=[pl.BlockSpec((tm, d), lambda i, j: (i, 0)), y_spec, y_spec, y_spec, y_spec,
                  pl.BlockSpec((N_BRANCH, d, tn), lambda i, j: (0, 0, j)),
                  pl.BlockSpec((N_BRANCH, 1, tn), lambda i, j: (0, 0, j)),
                  pl.BlockSpec((N_BRANCH, bw, tn), lambda i, j: (0, 0, j))],
        out_specs=pl.BlockSpec((tm, tn), lambda i, j: (i, j)),
        out_shape=jax.ShapeDtypeStruct((m, d), BF16),
        compiler_params=_params(("parallel", "arbitrary"), 48),
    )(xn, *ys, w_gate, b_gate.reshape(N_BRANCH, 1, d), w_branch)


def _s5_kernel(u_ref, bbd_ref, ar_ref, ai_ref, cbd_ref, d_ref, wglu_ref, o_ref, x_ref, st_ref, *, tt):
    n = S5_GROUPS * S5_STATE

    @pl.when(pl.program_id(1) == 0)
    def _():
        st_ref[...] = jnp.zeros_like(st_ref)

    u = u_ref[...]
    x_ref[...] = jnp.dot(u.astype(BF16), bbd_ref[...], preferred_element_type=F32)
    ar = ar_ref[...]
    ai = ai_ref[...]

    def step8(i, carry):
        xr, xi = carry
        base = pl.multiple_of(i * 8, 8)
        br = x_ref[pl.ds(base, 8), 0:n]
        bi = x_ref[pl.ds(base, 8), n:2 * n]
        rows_r, rows_i = [], []
        for r in range(8):
            nr = ar * xr - ai * xi + br[r:r + 1]
            ni = ar * xi + ai * xr + bi[r:r + 1]
            xr, xi = nr, ni
            rows_r.append(xr)
            rows_i.append(xi)
        x_ref[pl.ds(base, 8), 0:n] = jnp.concatenate(rows_r, axis=0)
        x_ref[pl.ds(base, 8), n:2 * n] = jnp.concatenate(rows_i, axis=0)
        return xr, xi

    xr, xi = lax.fori_loop(0, tt // 8, step8, (st_ref[0:1, :], st_ref[1:2, :]))
    st_ref[0:1, :] = xr
    st_ref[1:2, :] = xi

    y = jnp.dot(x_ref[...].astype(BF16), cbd_ref[...], preferred_element_type=F32) + d_ref[...] * u
    z = jax.nn.gelu(y)
    gate = jnp.dot(z.astype(BF16), wglu_ref[...], preferred_element_type=F32)
    o_ref[...] = (z * jax.nn.sigmoid(gate)).astype(o_ref.dtype)


def s5_branch(h_f32, col_block, bsz, seq, bbd, abar_re, abar_im, cbd, d_skip, w_glu, tt=256):
    n = S5_GROUPS * S5_STATE
    nt = seq // tt
    const = lambda shape: pl.BlockSpec(shape, lambda b, t: (0, 0))
    return pl.pallas_call(
        functools.partial(_s5_kernel, tt=tt),
        grid=(bsz, nt),
        in_specs=[pl.BlockSpec((tt, S5_WIDTH), lambda b, t: (b * nt + t, col_block)),
                  const((S5_WIDTH, 2 * n)), const((1, n)), const((1, n)), const((2 * n, S5_WIDTH)),
                  const((1, S5_WIDTH)), const((S5_WIDTH, S5_WIDTH))],
        out_specs=pl.BlockSpec((tt, S5_WIDTH), lambda b, t: (b * nt + t, 0)),
        out_shape=jax.ShapeDtypeStruct((bsz * seq, S5_WIDTH), BF16),
        scratch_shapes=[pltpu.VMEM((tt, 2 * n), F32), pltpu.VMEM((8, n), F32)],
        compiler_params=_params(("arbitrary", "arbitrary"), 48),
    )(h_f32, bbd, abar_re, abar_im, cbd, d_skip, w_glu)


def s5_tables(a_re, a_im, log_dt, b_re, b_im, c_re, c_im):
    g, p = a_re.shape
    dt = jnp.exp(log_dt)[:, None]
    mag = jnp.exp(a_re * dt)
    ab_re, ab_im = mag * jnp.cos(a_im * dt), mag * jnp.sin(a_im * dt)
    den = a_re * a_re + a_im * a_im
    nr, ni = ab_re - 1.0, ab_im
    coef_re = (nr * a_re + ni * a_im) / den
    coef_im = (ni * a_re - nr * a_im) / den
    bb_re = coef_re[..., None] * b_re - coef_im[..., None] * b_im
    bb_im = coef_re[..., None] * b_im + coef_im[..., None] * b_re
    eye = jnp.eye(g, dtype=F32)
    to_bd_in = lambda t: jnp.einsum('gpc,gh->gchp', t, eye).reshape(g * S5_GROUP, g * p)
    to_bd_out = lambda t: jnp.einsum('gcp,gh->gphc', t, eye).reshape(g * p, g * S5_GROUP)
    bbd = jnp.concatenate([to_bd_in(bb_re), to_bd_in(bb_im)], axis=1).astype(BF16)
    cbd = jnp.concatenate([to_bd_out(c_re), to_bd_out(-c_im)], axis=0).astype(BF16)
    return bbd, ab_re.reshape(1, g * p), ab_im.reshape(1, g * p), cbd


def _log_sigmoid(x):
    return jnp.minimum(x, 0.0) - jnp.log1p(jnp.exp(-jnp.abs(x)))


def _mlstm_kernel(qk_ref, v_ref, og_ref, gate_ref, gbias_ref, convw_ref, o_ref,
                  xbuf_ref, c_ref, n_ref, m_ref, *, tt):
    L = MLSTM_L
    hd = MLSTM_DIM
    nh = MLSTM_HEADS
    i_col, f_col = 72, 76

    @pl.when(pl.program_id(1) == 0)
    def _():
        xbuf_ref[0:8, :] = jnp.zeros((8, 2 * nh * hd), F32)
        c_ref[...] = jnp.zeros_like(c_ref)
        n_ref[...] = jnp.zeros_like(n_ref)
        m_ref[...] = jnp.zeros_like(m_ref)

    xbuf_ref[8:8 + tt, :] = qk_ref[...]
    cw = convw_ref[...]
    conv = xbuf_ref[8:8 + tt, :] * cw[CONV_WIDTH - 1:CONV_WIDTH]
    for sh in range(1, CONV_WIDTH):
        conv = conv + xbuf_ref[8 - sh:8 - sh + tt, :] * cw[CONV_WIDTH - 1 - sh:CONV_WIDTH - sh]
    xbuf_ref[0:8, :] = xbuf_ref[tt:tt + 8, :]
    qk = conv * jax.nn.sigmoid(conv)

    gates = gate_ref[...] + gbias_ref[...]
    logf = _log_sigmoid(gates)
    row = lax.broadcasted_iota(jnp.int32, (L, L), 0)
    col = lax.broadcasted_iota(jnp.int32, (L, L), 1)
    causal = col <= row
    tri = jnp.where(causal, 1.0, 0.0).astype(F32)

    for c in range(tt // L):
        r0 = c * L
        gc = gates[r0:r0 + L]
        bcum = jnp.dot(tri, logf[r0:r0 + L], precision=lax.Precision.HIGHEST, preferred_element_type=F32)
        gct = gc.T
        bcumt = bcum.T
        for h in range(nh):
            ig_col = gc[:, i_col + h:i_col + h + 1]
            b_col = bcum[:, f_col + h:f_col + h + 1]
            ig_row = gct[i_col + h:i_col + h + 1, :]
            b_row = bcumt[f_col + h:f_col + h + 1, :]
            b_last = b_col[L - 1:L, :]
            q = qk[r0:r0 + L, h * hd:(h + 1) * hd]
            k = qk[r0:r0 + L, (nh + h) * hd:(nh + h + 1) * hd] * (hd ** -0.5)
            vb = v_ref[r0:r0 + L, h * hd:(h + 1) * hd].astype(BF16)
            qb = q.astype(BF16)
            kb = k.astype(BF16)
            c_prev = c_ref[h]
            n_prev = n_ref[h]
            m_prev = m_ref[h][:, 0:1]

            g = b_last - b_col + ig_col
            m_loc = jnp.max(g, axis=0, keepdims=True)
            wk = jnp.exp(g - m_loc) * k
            c_loc = jnp.dot(wk.T.astype(BF16), vb, preferred_element_type=F32)
            n_loc = jnp.sum(wk, axis=0, keepdims=True)

            m_inter = b_col + m_prev
            log_d = jnp.where(causal, b_col - b_row + ig_row, -jnp.inf)
            m_j = jnp.maximum(jnp.max(log_d, axis=1, keepdims=True), m_inter)
            dqk = jnp.exp(log_d - m_j) * lax.dot_general(qb, kb, NT_DIMS, preferred_element_type=F32)
            inter = jnp.exp(m_inter - m_j)
            num = (jnp.dot(dqk.astype(BF16), vb, preferred_element_type=F32)
                   + inter * jnp.dot(qb, c_prev.astype(BF16), preferred_element_type=F32))
            den = jnp.sum(dqk, axis=1, keepdims=True) + inter * jnp.sum(q * n_prev, axis=1, keepdims=True)
            hid = num / jnp.maximum(jnp.abs(den), jnp.exp(-m_j))
            og = jax.nn.sigmoid(og_ref[r0:r0 + L, h * hd:(h + 1) * hd])
            o_ref[r0:r0 + L, h * hd:(h + 1) * hd] = (og * hid).astype(o_ref.dtype)

            m_new = jnp.maximum(b_last + m_prev, m_loc)
            a = jnp.exp(b_last + m_prev - m_new)
            bcoef = jnp.exp(m_loc - m_new)
            c_ref[h] = a * c_prev + bcoef * c_loc
            n_ref[h] = a * n_prev + bcoef * n_loc
            m_ref[h] = jnp.broadcast_to(m_new, (1, LANES))


def mlstm_branch(h_f32, misc, gate_bias, conv_w, bsz, seq, qk_blk, v_blk, o_blk, tt=512):
    nh, hd = MLSTM_HEADS, MLSTM_DIM
    w = nh * hd
    nt = seq // tt
    return pl.pallas_call(
        functools.partial(_mlstm_kernel, tt=tt),
        grid=(bsz, nt),
        in_specs=[pl.BlockSpec((tt, 2 * w), lambda b, t: (b * nt + t, qk_blk)),
                  pl.BlockSpec((tt, w), lambda b, t: (b * nt + t, v_blk)),
                  pl.BlockSpec((tt, w), lambda b, t: (b * nt + t, o_blk)),
                  pl.BlockSpec((tt, LANES), lambda b, t: (b * nt + t, 0)),
                  pl.BlockSpec((1, LANES), lambda b, t: (0, 0)),
                  pl.BlockSpec((CONV_WIDTH, 2 * w), lambda b, t: (0, 0))],
        out_specs=pl.BlockSpec((tt, w), lambda b, t: (b * nt + t, 0)),
        out_shape=jax.ShapeDtypeStruct((bsz * seq, w), BF16),
        scratch_shapes=[pltpu.VMEM((tt + 8, 2 * w), F32), pltpu.VMEM((nh, hd, hd), F32),
                        pltpu.VMEM((nh, 1, hd), F32), pltpu.VMEM((nh, 1, LANES), F32)],
        compiler_params=_params(("arbitrary", "arbitrary"), 48),
    )(h_f32, h_f32, h_f32, misc, gate_bias, conv_w)


def rel_bucket(dist):
    max_exact = REL_BUCKETS // 2
    n = jnp.maximum(dist, 0)
    large = max_exact + (jnp.log(jnp.maximum(n, 1).astype(F32) / max_exact)
                         / math.log(REL_MAX_DIST / max_exact)
                         * (REL_BUCKETS - max_exact)).astype(jnp.int32)
    large = jnp.minimum(large, REL_BUCKETS - 1)
    return jnp.where(n < max_exact, n, large)


def bias_tiles(rel, t):
    assert t >= REL_MAX_DIST
    r = jnp.arange(t)[:, None]
    c = jnp.arange(t)[None, :]
    rel_t = rel.T
    t0 = jnp.where((r >= c)[None], rel_t[:, rel_bucket(r - c)], NEG)
    t1 = rel_t[:, rel_bucket(t + r - c)]
    t2 = jnp.broadcast_to(rel_t[:, rel_bucket(jnp.full((t, t), 2 * t))], t1.shape)
    return jnp.stack([t0, t1, t2]).astype(F32)


def tri_pairs(nq):
    qi = [q for q in range(nq) for _ in range(q + 1)]
    kj = [k for q in range(nq) for k in range(q + 1)]
    return jnp.asarray(qi, jnp.int32), jnp.asarray(kj, jnp.int32)


def _online_softmax_step(idx, logits, vb, m_ref, l_ref, acc_ref):
    m_prev = m_ref[idx]
    m_new = jnp.maximum(m_prev, jnp.max(logits, axis=1, keepdims=True))
    alpha = jnp.exp(m_prev - m_new)
    p = jnp.exp(logits - m_new)
    l_ref[idx] = alpha * l_ref[idx] + jnp.sum(p, axis=1, keepdims=True)
    acc_ref[idx] = alpha * acc_ref[idx] + jnp.dot(p.astype(BF16), vb, preferred_element_type=F32)
    m_ref[idx] = m_new


def _init_softmax_state(m_ref, l_ref, acc_ref):
    m_ref[...] = jnp.full(m_ref.shape, NEG, F32)
    l_ref[...] = jnp.zeros_like(l_ref)
    acc_ref[...] = jnp.zeros_like(acc_ref)


def _diff_attn_kernel(qi_ref, kj_ref, q_ref, k_ref, v_ref, bias_ref, lam_ref, g_ref, o_ref,
                      m_ref, l_ref, acc_ref, *, out_scale):
    p = pl.program_id(1)
    qi = qi_ref[p]
    kj = kj_ref[p]
    dq = DIFF_QK_DIM
    dv = DIFF_V_DIM
    scale = dq ** -0.5

    @pl.when(kj == 0)
    def _():
        _init_softmax_state(m_ref, l_ref, acc_ref)

    lane = lax.broadcasted_iota(jnp.int32, (q_ref.shape[0], 2 * dq), 1)
    for h in range(DIFF_HEADS):
        qh = q_ref[:, h * 2 * dq:(h + 1) * 2 * dq]
        kh = k_ref[:, h * 2 * dq:(h + 1) * 2 * dq]
        vb = v_ref[:, h * dv:(h + 1) * dv]
        bias = bias_ref[0, h]
        for mi in range(2):
            qz = jnp.where((lane < dq) if mi == 0 else (lane >= dq), qh, jnp.zeros_like(qh))
            s = lax.dot_general(qz, kh, NT_DIMS, preferred_element_type=F32)
            _online_softmax_step(2 * h + mi, s * scale + bias, vb, m_ref, l_ref, acc_ref)

    @pl.when(kj == qi)
    def _():
        lam = lam_ref[...]
        for h in range(DIFF_HEADS):
            a = acc_ref[2 * h] / l_ref[2 * h] - lam * (acc_ref[2 * h + 1] / l_ref[2 * h + 1])
            o_ref[:, h * dv:(h + 1) * dv] = (_rms(a, g_ref[...]) * out_scale).astype(o_ref.dtype)


def diff_attention(h_bf16, bsz, seq, q_blk, k_blk, v_blk, bias, lam_row, subln_g, out_scale, t=512):
    nq = seq // t
    qi, kj = tri_pairs(nq)
    w = DIFF_HEADS * DIFF_V_DIM
    nmap = 2 * DIFF_HEADS
    grid_spec = pltpu.PrefetchScalarGridSpec(
        num_scalar_prefetch=2,
        grid=(bsz, qi.shape[0]),
        in_specs=[pl.BlockSpec((t, w), lambda b, p, qi, kj: (b * nq + qi[p], q_blk)),
                  pl.BlockSpec((t, w), lambda b, p, qi, kj: (b * nq + kj[p], k_blk)),
                  pl.BlockSpec((t, w), lambda b, p, qi, kj: (b * nq + kj[p], v_blk)),
                  pl.BlockSpec((1, DIFF_HEADS, t, t),
                               lambda b, p, qi, kj: (jnp.minimum(qi[p] - kj[p], 2), 0, 0, 0)),
                  pl.BlockSpec((1, DIFF_V_DIM), lambda b, p, qi, kj: (0, 0)),
                  pl.BlockSpec((1, DIFF_V_DIM), lambda b, p, qi, kj: (0, 0))],
        out_specs=pl.BlockSpec((t, w), lambda b, p, qi, kj: (b * nq + qi[p], 0)),
        scratch_shapes=[pltpu.VMEM((nmap, t, 1), F32), pltpu.VMEM((nmap, t, 1), F32),
                        pltpu.VMEM((nmap, t, DIFF_V_DIM), F32)],
    )
    return pl.pallas_call(
        functools.partial(_diff_attn_kernel, out_scale=out_scale),
        grid_spec=grid_spec,
        out_shape=jax.ShapeDtypeStruct((bsz * seq, w), BF16),
        compiler_params=_params(("arbitrary", "arbitrary"), 48),
    )(qi, kj, h_bf16, h_bf16, h_bf16, bias, lam_row, subln_g)


def _sort_key(x):
    b = lax.bitcast_convert_type(x, jnp.int32)
    return b ^ ((b >> 31) & 0x7FFFFFFF)


def _dsa_select_kernel(iq_ref, w_ref, ikz_ref, o_ref, key_ref, wb_ref, *, tq, tk, topk, seq):
    i = pl.program_id(1)
    last = (i * tq + tq - 1) // tk
    idx_scale = (IDX_DIM * IDX_HEADS) ** -0.5
    iw_col = 64

    w = w_ref[...]
    for h in range(IDX_HEADS):
        wb_ref[h] = jnp.broadcast_to(w[:, iw_col + h:iw_col + h + 1], (tq, LANES))

    def score_tile(j):
        kt = ikz_ref[pl.ds(pl.multiple_of(j * tk, tk), tk), :]
        sc = jnp.zeros((tq, tk), F32)
        for pair in range(IDX_HEADS // 2):
            qp = iq_ref[:, pair * LANES:(pair + 1) * LANES]
            for half in range(2):
                a = lax.dot_general(qp, kt[:, half * LANES:(half + 1) * LANES], NT_DIMS,
                                    preferred_element_type=F32)
                wfull = jnp.concatenate([wb_ref[2 * pair + half]] * (tk // LANES), axis=1)
                sc = sc + wfull * jnp.maximum(a, 0.0)
        return sc * idx_scale

    t_idx = i * tq + lax.broadcasted_iota(jnp.int32, (tq, tk), 0)
    s_idx = last * tk + lax.broadcasted_iota(jnp.int32, (tq, tk), 1)
    in_prefix = s_idx <= t_idx

    def score_body(j, carry):
        key_ref[:, pl.ds(pl.multiple_of(j * tk, tk), tk)] = _sort_key(score_tile(j))
        return carry

    lax.fori_loop(0, last, score_body, 0)
    key_ref[:, pl.ds(pl.multiple_of(last * tk, tk), tk)] = _sort_key(
        jnp.where(in_prefix, score_tile(last), -jnp.inf))

    def bit_body(it, thr):
        cand = thr + lax.shift_left(jnp.int32(1), 31 - it)
        candb = jnp.broadcast_to(cand, (tq, LANES))

        def count_body(j, acc):
            blk = key_ref[:, pl.ds(pl.multiple_of(j * tk, tk), tk)]
            for c in range(tk // LANES):
                acc = acc + jnp.where(blk[:, c * LANES:(c + 1) * LANES] >= candb, 1.0, 0.0)
            return acc

        acc = lax.fori_loop(0, last + 1, count_body, jnp.zeros((tq, LANES), F32))
        cnt = jnp.sum(acc, axis=1, keepdims=True)
        return jnp.where(cnt >= topk, cand, thr)

    thr = lax.fori_loop(0, 32, bit_body, jnp.full((tq, 1), INT_MIN, jnp.int32))

    def out_body(j, carry):
        off = pl.multiple_of(j * tk, tk)
        o_ref[0, :, pl.ds(off, tk)] = jnp.where(key_ref[:, pl.ds(off, tk)] >= thr, 0.0, NEG).astype(o_ref.dtype)
        return carry

    lax.fori_loop(0, last, out_body, 0)
    off = pl.multiple_of(last * tk, tk)
    keep = (key_ref[:, pl.ds(off, tk)] >= thr) & in_prefix
    o_ref[0, :, pl.ds(off, tk)] = jnp.where(keep, 0.0, NEG).astype(o_ref.dtype)

    def fill_body(j, carry):
        o_ref[0, :, pl.ds(pl.multiple_of(j * tk, tk), tk)] = jnp.full((tq, tk), NEG, o_ref.dtype)
        return carry

    lax.fori_loop(last + 1, seq // tk, fill_body, 0)


def dsa_select(h_bf16, misc, ikz, bsz, seq, iq_blk, tq=128, tk=512):
    nq = seq // tq
    topk = min(TOPK_MAX, seq // 4)
    w = IDX_HEADS * IDX_DIM
    return pl.pallas_call(
        functools.partial(_dsa_select_kernel, tq=tq, tk=tk, topk=topk, seq=seq),
        grid=(bsz, nq),
        in_specs=[pl.BlockSpec((tq, w), lambda b, i: (b * nq + i, iq_blk)),
                  pl.BlockSpec((tq, LANES), lambda b, i: (b * nq + i, 0)),
                  pl.BlockSpec((seq, 2 * LANES), lambda b, i: (b, 0), pipeline_mode=pl.Buffered(1))],
        out_specs=pl.BlockSpec((1, tq, seq), lambda b, i: (b, i, 0)),
        out_shape=jax.ShapeDtypeStruct((bsz, seq, seq), BF16),
        scratch_shapes=[pltpu.VMEM((tq, seq), jnp.int32), pltpu.VMEM((IDX_HEADS, tq, LANES), F32)],
        compiler_params=_params(("arbitrary", "arbitrary"), 48),
    )(h_bf16, misc, ikz)


def _dsa_attn_kernel(qi_ref, kj_ref, q_ref, k_ref, v_ref, bias_ref, mb_ref, o_ref, m_ref, l_ref, acc_ref):
    p = pl.program_id(1)
    qi = qi_ref[p]
    kj = kj_ref[p]
    dh = HEAD_DIM
    scale = dh ** -0.5

    @pl.when(kj == 0)
    def _():
        _init_softmax_state(m_ref, l_ref, acc_ref)

    mb = mb_ref[0].astype(F32)
    for h in range(DSA_HEADS):
        s = lax.dot_general(q_ref[:, h * dh:(h + 1) * dh], k_ref[:, h * dh:(h + 1) * dh], NT_DIMS,
                            preferred_element_type=F32)
        _online_softmax_step(h, s * scale + bias_ref[0, h] + mb, v_ref[:, h * dh:(h + 1) * dh],
                             m_ref, l_ref, acc_ref)

    @pl.when(kj == qi)
    def _():
        for h in range(DSA_HEADS):
            o_ref[:, h * dh:(h + 1) * dh] = (acc_ref[h] / l_ref[h]).astype(o_ref.dtype)


def dsa_attention(h_bf16, maskbias, bsz, seq, q_blk, k_blk, v_blk, bias, t=512):
    nq = seq // t
    qi, kj = tri_pairs(nq)
    w = DSA_HEADS * HEAD_DIM
    grid_spec = pltpu.PrefetchScalarGridSpec(
        num_scalar_prefetch=2,
        grid=(bsz, qi.shape[0]),
        in_specs=[pl.BlockSpec((t, w), lambda b, p, qi, kj: (b * nq + qi[p], q_blk)),
                  pl.BlockSpec((t, w), lambda b, p, qi, kj: (b * nq + kj[p], k_blk)),
                  pl.BlockSpec((t, w), lambda b, p, qi, kj: (b * nq + kj[p], v_blk)),
                  pl.BlockSpec((1, DSA_HEADS, t, t),
                               lambda b, p, qi, kj: (jnp.minimum(qi[p] - kj[p], 2), 0, 0, 0)),
                  pl.BlockSpec((1, t, t), lambda b, p, qi, kj: (b, qi[p], kj[p]))],
        out_specs=pl.BlockSpec((t, w), lambda b, p, qi, kj: (b * nq + qi[p], 0)),
        scratch_shapes=[pltpu.VMEM((DSA_HEADS, t, 1), F32), pltpu.VMEM((DSA_HEADS, t, 1), F32),
                        pltpu.VMEM((DSA_HEADS, t, HEAD_DIM), F32)],
    )
    return pl.pallas_call(
        _dsa_attn_kernel,
        grid_spec=grid_spec,
        out_shape=jax.ShapeDtypeStruct((bsz * seq, w), BF16),
        compiler_params=_params(("arbitrary", "arbitrary"), 48),
    )(qi, kj, h_bf16, h_bf16, h_bf16, bias, maskbias)


_SPLITS = (512, 512, 512, 512, 64, 8, 512, 512, 512, 512, 1024, 512, 512, 4, 4)
_NAMES = ('a_q', 'a_k', 'a_v', 'a_iq', 'a_ik', 'a_iw', 'b_u', 'c_q', 'c_k', 'c_v', 'd_qk', 'd_v', 'd_o', 'd_i', 'd_f')


def _split_w_in(w_in):
    out, off = {}, 0
    for name, width in zip(_NAMES, _SPLITS):
        out[name] = w_in[:, off:off + width]
        off += width
    return out


def _group_w_in(w_in):
    c = _split_w_in(w_in)
    d = w_in.shape[0]
    z = lambda n: jnp.zeros((d, n), w_in.dtype)
    w_attn = jnp.concatenate([c['a_q'], c['a_k'], c['a_v'], c['a_iq'], c['c_q'], c['c_k'], c['c_v']], axis=1)
    w_scan = jnp.concatenate([c['d_qk'], c['b_u'], c['d_v'], c['d_o']], axis=1)
    w_misc = jnp.concatenate([c['a_ik'], z(64), z(64), c['a_ik'],
                              z(64), c['a_iw'], c['d_i'], c['d_f'], z(48)], axis=1)
    return w_attn.astype(BF16), w_scan.astype(BF16), w_misc.astype(BF16)


def _pad_ffn(w_in, w_out):
    d = w_in.shape[0]
    pad = D_FF_PAD - D_FF
    zi = jnp.zeros((d, pad), w_in.dtype)
    w_in_pad = jnp.concatenate([w_in[:, :D_FF], zi, w_in[:, D_FF:], zi], axis=1).astype(BF16)
    w_out_pad = jnp.concatenate([w_out, jnp.zeros((pad, d), w_out.dtype)], axis=0).astype(BF16)
    return w_in_pad, w_out_pad


def _branches(xn, l, bsz, seq, w_in, conv_w, i_bias, f_bias, s5, s5_d, s5_w_glu, diff_lambda,
              diff_subln_g, bias_a, bias_c):
    lambda_init = 0.8 - 0.6 * math.exp(-0.3 * l)
    w_attn, w_scan, w_misc = _group_w_in(w_in)
    h_attn = matmul(xn, w_attn, BF16)
    h_scan = matmul(xn, w_scan, F32)
    ikz, misc = misc_proj(xn, w_misc)

    maskbias = dsa_select(h_attn, misc, ikz, bsz, seq, iq_blk=3)
    y_a = dsa_attention(h_attn, maskbias, bsz, seq, 0, 1, 2, bias_a)

    bbd, abar_re, abar_im, cbd = s5
    y_b = s5_branch(h_scan, 2, bsz, seq, bbd, abar_re, abar_im, cbd, s5_d.reshape(1, -1), s5_w_glu.astype(BF16))

    dl = diff_lambda.astype(F32)
    lam = jnp.exp(jnp.sum(dl[0] * dl[1])) - jnp.exp(jnp.sum(dl[2] * dl[3])) + lambda_init
    lam_row = jnp.full((1, DIFF_V_DIM), lam, F32)
    y_c = diff_attention(h_attn, bsz, seq, 4, 5, 6, bias_c, lam_row, diff_subln_g.reshape(1, -1),
                         1.0 - lambda_init)

    gate_bias = jnp.zeros((1, LANES), F32).at[0, 72:76].set(i_bias).at[0, 76:80].set(f_bias)
    y_d = mlstm_branch(h_scan, misc, gate_bias, conv_w, bsz, seq, qk_blk=0, v_blk=3, o_blk=4)
    return y_a, y_b, y_c, y_d


def _mixer(x, l, bsz, seq, norm_g, w_in, conv_w, i_bias, f_bias, s5, s5_d, s5_w_glu, diff_lambda,
           diff_subln_g, bias_a, bias_c, w_gate, b_gate, w_branch, w_out):
    xn = rmsnorm(x, norm_g, BF16)
    ys = _branches(xn, l, bsz, seq, w_in, conv_w, i_bias, f_bias, s5, s5_d, s5_w_glu, diff_lambda,
                   diff_subln_g, bias_a, bias_c)
    merged = gated_merge(xn, ys, w_gate.astype(BF16), b_gate, w_branch.astype(BF16))
    return matmul_residual(merged, w_out.astype(BF16), x)


def kernel(x, norm_g, w_ffn_in, w_ffn_out, w_in, conv_w, mlstm_i_bias, mlstm_f_bias, s5_a_re, s5_a_im,
           s5_log_dt, s5_b_re, s5_b_im, s5_c_re, s5_c_im, s5_d, s5_w_glu, diff_lambda, diff_subln_g,
           rel_table, w_gate, b_gate, w_branch, w_out, final_g):
    bsz, seq, d = x.shape
    t_attn = 512
    bias_a = bias_tiles(rel_table[:, :DSA_HEADS], t_attn)
    bias_c = bias_tiles(rel_table[:, DSA_HEADS:], t_attn)
    xf = x.reshape(bsz * seq, d)
    for l in range(DEPTH):
        xf = ffn_block(xf, norm_g[l, 0], *_pad_ffn(w_ffn_in[l, 0], w_ffn_out[l, 0]))
        s5 = s5_tables(s5_a_re[l], s5_a_im[l], s5_log_dt[l], s5_b_re[l], s5_b_im[l], s5_c_re[l], s5_c_im[l])
        xf = _mixer(xf, l, bsz, seq, norm_g[l, 1], w_in[l], conv_w[l], mlstm_i_bias[l], mlstm_f_bias[l],
                    s5, s5_d[l], s5_w_glu[l], diff_lambda[l], diff_subln_g[l], bias_a, bias_c,
                    w_gate[l], b_gate[l], w_branch[l], w_out[l])
        xf = ffn_block(xf, norm_g[l, 2], *_pad_ffn(w_ffn_in[l, 1], w_ffn_out[l, 1]))
    return rmsnorm(xf, final_g, F32).reshape(bsz, seq, d)
```

```python
import functools
import math

import jax
import jax.numpy as jnp
from jax import lax
from jax.experimental import pallas as pl
from jax.experimental.pallas import tpu as pltpu

F32 = jnp.float32
BF16 = jnp.bfloat16

D_MODEL = 2048
DEPTH = 4
N_BRANCH = 4
BRANCH_WIDTH = 512
HEAD_DIM = 128
DSA_HEADS = 4
IDX_HEADS = 8
IDX_DIM = 64
TOPK_MAX = 256
S5_WIDTH = 512
S5_GROUP = 16
S5_GROUPS = S5_WIDTH // S5_GROUP
S5_STATE = 64
DIFF_HEADS = 4
DIFF_QK_DIM = 64
DIFF_V_DIM = 128
MLSTM_HEADS = 4
MLSTM_DIM = 128
CONV_WIDTH = 4
D_FF = 5504
REL_BUCKETS = 32
REL_MAX_DIST = 128
EPS = 1e-6

LANES = 128
NEG = -1e30
INT_MIN = -(2 ** 31)
D_FF_PAD = 5632
MLSTM_L = 128

V_ONES = 16
LOG2E = math.log2(math.e)

NT_DIMS = (((1,), (1,)), ((), ()))


def _params(sem, vmem_mb):
    return pltpu.CompilerParams(dimension_semantics=sem, vmem_limit_bytes=vmem_mb * 1024 * 1024)


def _rms(x, g):
    return x * lax.rsqrt(jnp.mean(x * x, axis=-1, keepdims=True) + EPS) * g


def _rmsnorm_kernel(x_ref, g_ref, o_ref):
    o_ref[...] = _rms(x_ref[...], g_ref[...]).astype(o_ref.dtype)


def rmsnorm(x, g, out_dtype, tm=512):
    m, d = x.shape
    return pl.pallas_call(
        _rmsnorm_kernel,
        grid=(m // tm,),
        in_specs=[pl.BlockSpec((tm, d), lambda i: (i, 0)), pl.BlockSpec((1, d), lambda i: (0, 0))],
        out_specs=pl.BlockSpec((tm, d), lambda i: (i, 0)),
        out_shape=jax.ShapeDtypeStruct((m, d), out_dtype),
        compiler_params=_params(("parallel",), 40),
    )(x, g.reshape(1, d))


def _mm_kernel(x_ref, w_ref, cs_ref, o_ref):
    acc = jnp.dot(x_ref[...], w_ref[...], preferred_element_type=F32)
    o_ref[...] = (acc * cs_ref[...]).astype(o_ref.dtype)


def matmul(x, w, col_scale, out_dtype, tm=1024, tn=512):
    m, k = x.shape
    n = w.shape[1]
    return pl.pallas_call(
        _mm_kernel,
        grid=(m // tm, n // tn),
        in_specs=[pl.BlockSpec((tm, k), lambda i, j: (i, 0)), pl.BlockSpec((k, tn), lambda i, j: (0, j)),
                  pl.BlockSpec((1, tn), lambda i, j: (0, j))],
        out_specs=pl.BlockSpec((tm, tn), lambda i, j: (i, j)),
        out_shape=jax.ShapeDtypeStruct((m, n), out_dtype),
        compiler_params=_params(("parallel", "arbitrary"), 40),
    )(x, w, col_scale)


def _mm_res_kernel(x_ref, w_ref, r_ref, o_ref):
    o_ref[...] = r_ref[...] + jnp.dot(x_ref[...], w_ref[...], preferred_element_type=F32)


def matmul_residual(x, w, r, tm=1024, tn=512):
    m, k = x.shape
    n = w.shape[1]
    return pl.pallas_call(
        _mm_res_kernel,
        grid=(m // tm, n // tn),
        in_specs=[pl.BlockSpec((tm, k), lambda i, j: (i, 0)), pl.BlockSpec((k, tn), lambda i, j: (0, j)),
                  pl.BlockSpec((tm, tn), lambda i, j: (i, j))],
        out_specs=pl.BlockSpec((tm, tn), lambda i, j: (i, j)),
        out_shape=jax.ShapeDtypeStruct((m, n), F32),
        compiler_params=_params(("parallel", "arbitrary"), 40),
    )(x, w, r)


def _misc_proj_kernel(x_ref, w_ref, ikz_ref, misc_ref):
    acc = jnp.dot(x_ref[...], w_ref[...], preferred_element_type=F32)
    ikz_ref[...] = acc[:, :2 * LANES].astype(BF16)
    misc_ref[...] = acc[:, 2 * LANES:]


def misc_proj(xn, w, tm=1024):
    m, k = xn.shape
    n = w.shape[1]
    return pl.pallas_call(
        _misc_proj_kernel,
        grid=(m // tm,),
        in_specs=[pl.BlockSpec((tm, k), lambda i: (i, 0)), pl.BlockSpec((k, n), lambda i: (0, 0))],
        out_specs=[pl.BlockSpec((tm, 2 * LANES), lambda i: (i, 0)), pl.BlockSpec((tm, LANES), lambda i: (i, 0))],
        out_shape=[jax.ShapeDtypeStruct((m, 2 * LANES), BF16), jax.ShapeDtypeStruct((m, LANES), F32)],
        compiler_params=_params(("parallel",), 40),
    )(xn, w)


def _ffn_kernel(x_ref, g_ref, wg_ref, wu_ref, wo_ref, o_ref, xn_ref, acc_ref):
    j = pl.program_id(1)

    @pl.when(j == 0)
    def _():
        xn_ref[...] = _rms(x_ref[...], g_ref[...]).astype(BF16)
        acc_ref[...] = jnp.zeros_like(acc_ref)

    xn = xn_ref[...]
    g = jnp.dot(xn, wg_ref[...], preferred_element_type=F32)
    u = jnp.dot(xn, wu_ref[...], preferred_element_type=F32)
    a = (g * jax.nn.sigmoid(g) * u).astype(BF16)
    acc_ref[...] += jnp.dot(a, wo_ref[...], preferred_element_type=F32)

    @pl.when(j == pl.num_programs(1) - 1)
    def _():
        o_ref[...] = x_ref[...] + 0.5 * acc_ref[...]


def ffn_block(x, g, w_in_pad, w_out_pad, tm=512, tf=512):
    m, d = x.shape
    nf = w_out_pad.shape[0] // tf
    return pl.pallas_call(
        _ffn_kernel,
        grid=(m // tm, nf),
        in_specs=[pl.BlockSpec((tm, d), lambda i, j: (i, 0)),
                  pl.BlockSpec((1, d), lambda i, j: (0, 0)),
                  pl.BlockSpec((d, tf), lambda i, j: (0, j)),
                  pl.BlockSpec((d, tf), lambda i, j: (0, j + nf)),
                  pl.BlockSpec((tf, d), lambda i, j: (j, 0))],
        out_specs=pl.BlockSpec((tm, d), lambda i, j: (i, 0)),
        out_shape=jax.ShapeDtypeStruct((m, d), F32),
        scratch_shapes=[pltpu.VMEM((tm, d), BF16), pltpu.VMEM((tm, d), F32)],
        compiler_params=_params(("parallel", "arbitrary"), 48),
    )(x, g.reshape(1, d), w_in_pad, w_in_pad, w_out_pad)


def _merge_kernel(xn_ref, ya_ref, yb_ref, yc_ref, yd_ref, wg_ref, bg_ref, wb_ref, o_ref):
    xn = xn_ref[...]
    acc = None
    for n, y_ref in enumerate((ya_ref, yb_ref, yc_ref, yd_ref)):
        gate = jnp.dot(xn, wg_ref[n], preferred_element_type=F32) + bg_ref[n]
        proj = jnp.dot(y_ref[...], wb_ref[n], preferred_element_type=F32)
        term = jax.nn.sigmoid(gate) * proj
        acc = term if acc is None else acc + term
    o_ref[...] = acc.astype(o_ref.dtype)


def gated_merge(xn, ys, w_gate, b_gate, w_branch, tm=1024, tn=256):
    m, d = xn.shape
    bw = ys[0].shape[1]
    y_spec = pl.BlockSpec((tm, bw), lambda i, j: (i, 0))
    return pl.pallas_call(
        _merge_kernel,
        grid=(m // tm, d // tn),
        in_specs=[pl.BlockSpec((tm, d), lambda i, j: (i, 0)), y_spec, y_spec, y_spec, y_spec,
                  pl.BlockSpec((N_BRANCH, d, tn), lambda i, j: (0, 0, j)),
                  pl.BlockSpec((N_BRANCH, 1, tn), lambda i, j: (0, 0, j)),
                  pl.BlockSpec((N_BRANCH, bw, tn), lambda i, j: (0, 0, j))],
        out_specs=pl.BlockSpec((tm, tn), lambda i, j: (i, j)),
        out_shape=jax.ShapeDtypeStruct((m, d), BF16),
        compiler_params=_params(("parallel", "arbitrary"), 48),
    )(xn, *ys, w_gate, b_gate.reshape(N_BRANCH, 1, d), w_branch)


def _s5_kernel(u_ref, bbd_ref, ar_ref, ai_ref, cbd_ref, d_ref, wglu_ref, o_ref, x_ref, st_ref, *, tt):
    n = S5_GROUPS * S5_STATE

    @pl.when(pl.program_id(1) == 0)
    def _():
        st_ref[...] = jnp.zeros_like(st_ref)

    u = u_ref[...]
    x_ref[...] = jnp.dot(u.astype(BF16), bbd_ref[...], preferred_element_type=F32)
    ar = ar_ref[...]
    ai = ai_ref[...]

    def step8(i, carry):
        xr, xi = carry
        base = pl.multiple_of(i * 8, 8)
        br = x_ref[pl.ds(base, 8), 0:n]
        bi = x_ref[pl.ds(base, 8), n:2 * n]
        rows_r, rows_i = [], []
        for r in range(8):
            nr = ar * xr - ai * xi + br[r:r + 1]
            ni = ar * xi + ai * xr + bi[r:r + 1]
            xr, xi = nr, ni
            rows_r.append(xr)
            rows_i.append(xi)
        x_ref[pl.ds(base, 8), 0:n] = jnp.concatenate(rows_r, axis=0)
        x_ref[pl.ds(base, 8), n:2 * n] = jnp.concatenate(rows_i, axis=0)
        return xr, xi

    xr, xi = lax.fori_loop(0, tt // 8, step8, (st_ref[0:1, :], st_ref[1:2, :]))
    st_ref[0:1, :] = xr
    st_ref[1:2, :] = xi

    y = jnp.dot(x_ref[...].astype(BF16), cbd_ref[...], preferred_element_type=F32) + d_ref[...] * u
    z = jax.nn.gelu(y)
    gate = jnp.dot(z.astype(BF16), wglu_ref[...], preferred_element_type=F32)
    o_ref[...] = (z * jax.nn.sigmoid(gate)).astype(o_ref.dtype)


def s5_branch(h_f32, col_block, bsz, seq, bbd, abar_re, abar_im, cbd, d_skip, w_glu, tt=256):
    n = S5_GROUPS * S5_STATE
    nt = seq // tt
    const = lambda shape: pl.BlockSpec(shape, lambda b, t: (0, 0))
    return pl.pallas_call(
        functools.partial(_s5_kernel, tt=tt),
        grid=(bsz, nt),
        in_specs=[pl.BlockSpec((tt, S5_WIDTH), lambda b, t: (b * nt + t, col_block)),
                  const((S5_WIDTH, 2 * n)), const((1, n)), const((1, n)), const((2 * n, S5_WIDTH)),
                  const((1, S5_WIDTH)), const((S5_WIDTH, S5_WIDTH))],
        out_specs=pl.BlockSpec((tt, S5_WIDTH), lambda b, t: (b * nt + t, 0)),
        out_shape=jax.ShapeDtypeStruct((bsz * seq, S5_WIDTH), BF16),
        scratch_shapes=[pltpu.VMEM((tt, 2 * n), F32), pltpu.VMEM((8, n), F32)],
        compiler_params=_params(("arbitrary", "arbitrary"), 48),
    )(h_f32, bbd, abar_re, abar_im, cbd, d_skip, w_glu)


def s5_tables(a_re, a_im, log_dt, b_re, b_im, c_re, c_im):
    g, p = a_re.shape
    dt = jnp.exp(log_dt)[:, None]
    mag = jnp.exp(a_re * dt)
    ab_re, ab_im = mag * jnp.cos(a_im * dt), mag * jnp.sin(a_im * dt)
    den = a_re * a_re + a_im * a_im
    nr, ni = ab_re - 1.0, ab_im
    coef_re = (nr * a_re + ni * a_im) / den
    coef_im = (ni * a_re - nr * a_im) / den
    bb_re = coef_re[..., None] * b_re - coef_im[..., None] * b_im
    bb_im = coef_re[..., None] * b_im + coef_im[..., None] * b_re
    eye = jnp.eye(g, dtype=F32)
    to_bd_in = lambda t: jnp.einsum('gpc,gh->gchp', t, eye).reshape(g * S5_GROUP, g * p)
    to_bd_out = lambda t: jnp.einsum('gcp,gh->gphc', t, eye).reshape(g * p, g * S5_GROUP)
    bbd = jnp.concatenate([to_bd_in(bb_re), to_bd_in(bb_im)], axis=1).astype(BF16)
    cbd = jnp.concatenate([to_bd_out(c_re), to_bd_out(-c_im)], axis=0).astype(BF16)
    return bbd, ab_re.reshape(1, g * p), ab_im.reshape(1, g * p), cbd


def _log_sigmoid(x):
    return jnp.minimum(x, 0.0) - jnp.log1p(jnp.exp(-jnp.abs(x)))


def _mlstm_kernel(qk_ref, v_ref, og_ref, gate_ref, gbias_ref, convw_ref, o_ref,
                  xbuf_ref, c_ref, n_ref, m_ref, *, tt):
    L = MLSTM_L
    hd = MLSTM_DIM
    nh = MLSTM_HEADS
    i_col, f_col = 72, 76

    @pl.when(pl.program_id(1) == 0)
    def _():
        xbuf_ref[0:8, :] = jnp.zeros((8, 2 * nh * hd), F32)
        c_ref[...] = jnp.zeros_like(c_ref)
        n_ref[...] = jnp.zeros_like(n_ref)
        m_ref[...] = jnp.zeros_like(m_ref)

    xbuf_ref[8:8 + tt, :] = qk_ref[...]
    cw = convw_ref[...]
    conv = xbuf_ref[8:8 + tt, :] * cw[CONV_WIDTH - 1:CONV_WIDTH]
    for sh in range(1, CONV_WIDTH):
        conv = conv + xbuf_ref[8 - sh:8 - sh + tt, :] * cw[CONV_WIDTH - 1 - sh:CONV_WIDTH - sh]
    xbuf_ref[0:8, :] = xbuf_ref[tt:tt + 8, :]
    qk = conv * jax.nn.sigmoid(conv)

    gates = gate_ref[...] + gbias_ref[...]
    logf = _log_sigmoid(gates)
    row = lax.broadcasted_iota(jnp.int32, (L, L), 0)
    col = lax.broadcasted_iota(jnp.int32, (L, L), 1)
    causal = col <= row
    tri = jnp.where(causal, 1.0, 0.0).astype(F32)

    for c in range(tt // L):
        r0 = c * L
        gc = gates[r0:r0 + L]
        bcum = jnp.dot(tri, logf[r0:r0 + L], precision=lax.Precision.HIGHEST, preferred_element_type=F32)
        gct = gc.T
        bcumt = bcum.T
        for h in range(nh):
            ig_col = gc[:, i_col + h:i_col + h + 1]
            b_col = bcum[:, f_col + h:f_col + h + 1]
            ig_row = gct[i_col + h:i_col + h + 1, :]
            b_row = bcumt[f_col + h:f_col + h + 1, :]
            b_last = b_col[L - 1:L, :]
            q = qk[r0:r0 + L, h * hd:(h + 1) * hd]
            k = qk[r0:r0 + L, (nh + h) * hd:(nh + h + 1) * hd] * (hd ** -0.5)
            vb = v_ref[r0:r0 + L, h * hd:(h + 1) * hd].astype(BF16)
            qb = q.astype(BF16)
            kb = k.astype(BF16)
            c_prev = c_ref[h]
            n_prev = n_ref[h]
            m_prev = m_ref[h][:, 0:1]

            g = b_last - b_col + ig_col
            m_loc = jnp.max(g, axis=0, keepdims=True)
            wk = jnp.exp(g - m_loc) * k
            c_loc = jnp.dot(wk.T.astype(BF16), vb, preferred_element_type=F32)
            n_loc = jnp.sum(wk, axis=0, keepdims=True)

            m_inter = b_col + m_prev
            log_d = jnp.where(causal, b_col - b_row + ig_row, -jnp.inf)
            m_j = jnp.maximum(jnp.max(log_d, axis=1, keepdims=True), m_inter)
            dqk = jnp.exp(log_d - m_j) * lax.dot_general(qb, kb, NT_DIMS, preferred_element_type=F32)
            inter = jnp.exp(m_inter - m_j)
            num = (jnp.dot(dqk.astype(BF16), vb, preferred_element_type=F32)
                   + inter * jnp.dot(qb, c_prev.astype(BF16), preferred_element_type=F32))
            den = jnp.sum(dqk, axis=1, keepdims=True) + inter * jnp.sum(q * n_prev, axis=1, keepdims=True)
            hid = num / jnp.maximum(jnp.abs(den), jnp.exp(-m_j))
            og = jax.nn.sigmoid(og_ref[r0:r0 + L, h * hd:(h + 1) * hd])
            o_ref[r0:r0 + L, h * hd:(h + 1) * hd] = (og * hid).astype(o_ref.dtype)

            m_new = jnp.maximum(b_last + m_prev, m_loc)
            a = jnp.exp(b_last + m_prev - m_new)
            bcoef = jnp.exp(m_loc - m_new)
            c_ref[h] = a * c_prev + bcoef * c_loc
            n_ref[h] = a * n_prev + bcoef * n_loc
            m_ref[h] = jnp.broadcast_to(m_new, (1, LANES))


def mlstm_branch(h_f32, misc, gate_bias, conv_w, bsz, seq, qk_blk, v_blk, o_blk, tt=512):
    nh, hd = MLSTM_HEADS, MLSTM_DIM
    w = nh * hd
    nt = seq // tt
    return pl.pallas_call(
        functools.partial(_mlstm_kernel, tt=tt),
        grid=(bsz, nt),
        in_specs=[pl.BlockSpec((tt, 2 * w), lambda b, t: (b * nt + t, qk_blk)),
                  pl.BlockSpec((tt, w), lambda b, t: (b * nt + t, v_blk)),
                  pl.BlockSpec((tt, w), lambda b, t: (b * nt + t, o_blk)),
                  pl.BlockSpec((tt, LANES), lambda b, t: (b * nt + t, 0)),
                  pl.BlockSpec((1, LANES), lambda b, t: (0, 0)),
                  pl.BlockSpec((CONV_WIDTH, 2 * w), lambda b, t: (0, 0))],
        out_specs=pl.BlockSpec((tt, w), lambda b, t: (b * nt + t, 0)),
        out_shape=jax.ShapeDtypeStruct((bsz * seq, w), BF16),
        scratch_shapes=[pltpu.VMEM((tt + 8, 2 * w), F32), pltpu.VMEM((nh, hd, hd), F32),
                        pltpu.VMEM((nh, 1, hd), F32), pltpu.VMEM((nh, 1, LANES), F32)],
        compiler_params=_params(("arbitrary", "arbitrary"), 48),
    )(h_f32, h_f32, h_f32, misc, gate_bias, conv_w)


def rel_bucket(dist):
    max_exact = REL_BUCKETS // 2
    n = jnp.maximum(dist, 0)
    large = max_exact + (jnp.log(jnp.maximum(n, 1).astype(F32) / max_exact)
                         / math.log(REL_MAX_DIST / max_exact)
                         * (REL_BUCKETS - max_exact)).astype(jnp.int32)
    large = jnp.minimum(large, REL_BUCKETS - 1)
    return jnp.where(n < max_exact, n, large)


def bias_tiles(rel, t):
    assert t >= REL_MAX_DIST
    kk = jnp.arange(t)[:, None]
    qq = jnp.arange(t)[None, :]

    def lookup(dist):
        onehot = (rel_bucket(dist)[..., None] == jnp.arange(REL_BUCKETS)).astype(F32)
        return jnp.einsum('kqb,bh->hkq', onehot, rel.astype(F32), precision=lax.Precision.HIGHEST) * LOG2E

    t0 = jnp.where((qq >= kk)[None], lookup(qq - kk), NEG)
    t1 = lookup(t + qq - kk)
    t2 = lookup(jnp.full((t, t), 2 * t))
    return jnp.stack([t0, t1, t2])


def tri_pairs(nq):
    qi = [q for q in range(nq) for _ in range(q + 1)]
    kj = [k for q in range(nq) for k in range(q + 1)]
    return jnp.asarray(qi, jnp.int32), jnp.asarray(kj, jnp.int32)


def values_t_ext(h_bf16, col0, bsz, seq, heads, dv):
    v = h_bf16[:, col0:col0 + heads * dv].reshape(bsz, seq, heads, dv)
    vt = jnp.transpose(v, (0, 2, 3, 1))
    ones = jnp.ones((bsz, heads, V_ONES, seq), BF16)
    return jnp.concatenate([vt, ones], axis=2).reshape(bsz * heads * (dv + V_ONES), seq)


def _attn_map_step(idx, kh, qz, vt_ext, bias_ref, head, mb_ref, s_ref, p_ref, m_ref, acc_ref):
    tk, tq = s_ref.shape[1:]
    sb = s_ref.at[idx % 2]
    pb = p_ref.at[idx % 2]
    sb[...] = lax.dot_general(kh, qz, NT_DIMS, preferred_element_type=F32)

    def logits(r0, n):
        t = sb[r0:r0 + n, :] + bias_ref[0, head, r0:r0 + n, :]
        if mb_ref is not None:
            t = t + mb_ref[r0:r0 + n, :]
        return t

    mx = [jnp.full((8, tq), -jnp.inf, F32) for _ in range(4)]
    for c in range(tk // 8):
        mx[c % 4] = jnp.maximum(mx[c % 4], logits(c * 8, 8))
    m_tile = jnp.max(jnp.maximum(jnp.maximum(mx[0], mx[1]), jnp.maximum(mx[2], mx[3])), axis=0, keepdims=True)
    m_prev = m_ref[idx]
    m_new = jnp.maximum(m_prev, m_tile)
    m_ref[idx] = m_new
    m_b = jnp.broadcast_to(m_new, (16, tq))
    for c in range(tk // 16):
        pb[c * 16:(c + 1) * 16, :] = jnp.exp2(logits(c * 16, 16) - m_b).astype(BF16)
    alpha = jnp.exp2(m_prev - m_new)
    acc_ref[idx] = alpha * acc_ref[idx] + jnp.dot(vt_ext, pb[...], preferred_element_type=F32)


def _init_softmax_state(m_ref, acc_ref):
    m_ref[...] = jnp.full(m_ref.shape, NEG, F32)
    acc_ref[...] = jnp.zeros_like(acc_ref)


def _diff_attn_kernel(qi_ref, kj_ref, q_ref, k_ref, vt_ref, bias_ref, lam_ref, gb_ref, o_ref,
                      s_ref, p_ref, m_ref, acc_ref, *, out_scale):
    p = pl.program_id(1)
    qi = qi_ref[p]
    kj = kj_ref[p]
    dq = DIFF_QK_DIM
    dv = DIFF_V_DIM
    vrows = dv + V_ONES

    @pl.when(kj == 0)
    def _():
        _init_softmax_state(m_ref, acc_ref)

    lane = lax.broadcasted_iota(jnp.int32, (q_ref.shape[0], 2 * dq), 1)
    for h in range(DIFF_HEADS):
        qh = q_ref[:, h * 2 * dq:(h + 1) * 2 * dq]
        kh = k_ref[:, h * 2 * dq:(h + 1) * 2 * dq]
        vt = vt_ref[h * vrows:(h + 1) * vrows, :]
        for mi in range(2):
            qz = jnp.where((lane < dq) if mi == 0 else (lane >= dq), qh, jnp.zeros_like(qh))
            _attn_map_step(2 * h + mi, kh, qz, vt, bias_ref, h, None, s_ref, p_ref, m_ref, acc_ref)

    @pl.when(kj == qi)
    def _():
        lam = lam_ref[0:1, 0:1]
        for h in range(DIFF_HEADS):
            a1 = acc_ref[2 * h]
            a2 = acc_ref[2 * h + 1]
            a = a1[0:dv] / a1[dv:dv + 1] - lam * (a2[0:dv] / a2[dv:dv + 1])
            y = a * lax.rsqrt(jnp.mean(a * a, axis=0, keepdims=True) + EPS) * gb_ref[...] * out_scale
            o_ref[:, h * dv:(h + 1) * dv] = y.T.astype(o_ref.dtype)


def diff_attention(h_bf16, vt_ext, bsz, seq, q_blk, k_blk, bias, lam_row, subln_g, out_scale, t=512):
    nq = seq // t
    qi, kj = tri_pairs(nq)
    w = DIFF_HEADS * DIFF_V_DIM
    nmap = 2 * DIFF_HEADS
    vrows = DIFF_HEADS * (DIFF_V_DIM + V_ONES)
    g_b = jnp.broadcast_to(subln_g.reshape(DIFF_V_DIM, 1), (DIFF_V_DIM, t))
    grid_spec = pltpu.PrefetchScalarGridSpec(
        num_scalar_prefetch=2,
        grid=(bsz, qi.shape[0]),
        in_specs=[pl.BlockSpec((t, w), lambda b, p, qi, kj: (b * nq + qi[p], q_blk)),
                  pl.BlockSpec((t, w), lambda b, p, qi, kj: (b * nq + kj[p], k_blk)),
                  pl.BlockSpec((vrows, t), lambda b, p, qi, kj: (b, kj[p])),
                  pl.BlockSpec((1, DIFF_HEADS, t, t),
                               lambda b, p, qi, kj: (jnp.minimum(qi[p] - kj[p], 2), 0, 0, 0)),
                  pl.BlockSpec((1, LANES), lambda b, p, qi, kj: (0, 0)),
                  pl.BlockSpec((DIFF_V_DIM, t), lambda b, p, qi, kj: (0, 0))],
        out_specs=pl.BlockSpec((t, w), lambda b, p, qi, kj: (b * nq + qi[p], 0)),
        scratch_shapes=[pltpu.VMEM((2, t, t), F32), pltpu.VMEM((2, t, t), BF16),
                        pltpu.VMEM((nmap, 1, t), F32), pltpu.VMEM((nmap, DIFF_V_DIM + V_ONES, t), F32)],
    )
    return pl.pallas_call(
        functools.partial(_diff_attn_kernel, out_scale=out_scale),
        grid_spec=grid_spec,
        out_shape=jax.ShapeDtypeStruct((bsz * seq, w), BF16),
        compiler_params=_params(("arbitrary", "arbitrary"), 48),
    )(qi, kj, h_bf16, h_bf16, vt_ext, bias, lam_row, g_b)


def _sort_key(x):
    b = lax.bitcast_convert_type(x, jnp.int32)
    return b ^ ((b >> 31) & 0x7FFFFFFF)


def _dsa_select_kernel(iq_ref, w_ref, ikz_ref, o_ref, key_ref, a_ref, wb_ref, *, tq, tk, topk, seq):
    i = pl.program_id(1)
    last = (i * tq + tq - 1) // tk
    idx_scale = (IDX_DIM * IDX_HEADS) ** -0.5
    iw_row = 64

    wt = w_ref[...].T
    for h in range(IDX_HEADS):
        wb_ref[h] = jnp.broadcast_to(wt[iw_row + h:iw_row + h + 1, :], (8, tq))

    def in_prefix(off, n):
        key_pos = off + lax.broadcasted_iota(jnp.int32, (n, tq), 0)
        q_pos = i * tq + lax.broadcasted_iota(jnp.int32, (n, tq), 1)
        return key_pos <= q_pos

    def score_tile(j, diagonal):
        off = pl.multiple_of(j * tk, tk)
        for pair in range(IDX_HEADS // 2):
            qp = iq_ref[:, pair * LANES:(pair + 1) * LANES]
            for half in range(2):
                a_ref[2 * pair + half] = lax.dot_general(
                    ikz_ref[pl.ds(off, tk), half * LANES:(half + 1) * LANES], qp, NT_DIMS,
                    preferred_element_type=F32)
        for c in range(tk // 8):
            r0 = c * 8
            sc = wb_ref[0] * jnp.maximum(a_ref[0, r0:r0 + 8, :], 0.0)
            for h in range(1, IDX_HEADS):
                sc = sc + wb_ref[h] * jnp.maximum(a_ref[h, r0:r0 + 8, :], 0.0)
            sc = sc * idx_scale
            if diagonal:
                sc = jnp.where(in_prefix(off + r0, 8), sc, -jnp.inf)
            key_ref[pl.ds(pl.multiple_of(off + r0, 8), 8), :] = _sort_key(sc)

    def score_body(j, carry):
        score_tile(j, False)
        return carry

    lax.fori_loop(0, last, score_body, 0)
    score_tile(last, True)

    def bit_body(it, thr):
        cand = thr + lax.shift_left(jnp.int32(1), 31 - it)
        cand_b = jnp.broadcast_to(cand, (8, tq))

        def count_body(j, accs):
            off = pl.multiple_of(j * tk, tk)
            accs = list(accs)
            for c in range(tk // 8):
                blk = key_ref[pl.ds(pl.multiple_of(off + c * 8, 8), 8), :]
                accs[c % 4] = accs[c % 4] + jnp.where(blk >= cand_b, 1.0, 0.0)
            return tuple(accs)

        zero = jnp.zeros((8, tq), F32)
        accs = lax.fori_loop(0, last + 1, count_body, (zero, zero, zero, zero))
        cnt = jnp.sum((accs[0] + accs[1]) + (accs[2] + accs[3]), axis=0, keepdims=True)
        return jnp.where(cnt >= topk, cand, thr)

    thr = lax.fori_loop(0, 32, bit_body, jnp.full((1, tq), INT_MIN, jnp.int32))
    thr_b = jnp.broadcast_to(thr, (16, tq))

    def out_tile(j, diagonal):
        off = pl.multiple_of(j * tk, tk)
        for c in range(tk // 16):
            rows = pl.ds(pl.multiple_of(off + c * 16, 16), 16)
            keep = key_ref[rows, :] >= thr_b
            if diagonal:
                keep = keep & in_prefix(off + c * 16, 16)
            o_ref[0, rows, :] = jnp.where(keep, 0.0, NEG).astype(o_ref.dtype)

    def out_body(j, carry):
        out_tile(j, False)
        return carry

    lax.fori_loop(0, last, out_body, 0)
    out_tile(last, True)

    def fill_body(j, carry):
        o_ref[0, pl.ds(pl.multiple_of(j * tk, tk), tk), :] = jnp.full((tk, tq), NEG, o_ref.dtype)
        return carry

    lax.fori_loop(last + 1, seq // tk, fill_body, 0)


def dsa_select(h_bf16, misc, ikz, bsz, seq, iq_blk, tq=256, tk=512):
    nq = seq // tq
    topk = min(TOPK_MAX, seq // 4)
    w = IDX_HEADS * IDX_DIM
    return pl.pallas_call(
        functools.partial(_dsa_select_kernel, tq=tq, tk=tk, topk=topk, seq=seq),
        grid=(bsz, nq),
        in_specs=[pl.BlockSpec((tq, w), lambda b, i: (b * nq + i, iq_blk)),
                  pl.BlockSpec((tq, LANES), lambda b, i: (b * nq + i, 0)),
                  pl.BlockSpec((seq, 2 * LANES), lambda b, i: (b, 0), pipeline_mode=pl.Buffered(1))],
        out_specs=pl.BlockSpec((1, seq, tq), lambda b, i: (b, 0, i)),
        out_shape=jax.ShapeDtypeStruct((bsz, seq, seq), BF16),
        scratch_shapes=[pltpu.VMEM((seq, tq), jnp.int32), pltpu.VMEM((IDX_HEADS, tk, tq), F32),
                        pltpu.VMEM((IDX_HEADS, 8, tq), F32)],
        compiler_params=_params(("arbitrary", "arbitrary"), 56),
    )(h_bf16, misc, ikz)


def _dsa_attn_kernel(qi_ref, kj_ref, q_ref, k_ref, vt_ref, bias_ref, mask_ref, o_ref,
                     s_ref, p_ref, mb_ref, m_ref, acc_ref):
    p = pl.program_id(1)
    qi = qi_ref[p]
    kj = kj_ref[p]
    dh = HEAD_DIM
    vrows = dh + V_ONES

    @pl.when(kj == 0)
    def _():
        _init_softmax_state(m_ref, acc_ref)

    mb_ref[...] = mask_ref[0].astype(F32)
    for h in range(DSA_HEADS):
        _attn_map_step(h, k_ref[:, h * dh:(h + 1) * dh], q_ref[:, h * dh:(h + 1) * dh],
                       vt_ref[h * vrows:(h + 1) * vrows, :], bias_ref, h, mb_ref, s_ref, p_ref, m_ref, acc_ref)

    @pl.when(kj == qi)
    def _():
        for h in range(DSA_HEADS):
            a = acc_ref[h]
            o_ref[:, h * dh:(h + 1) * dh] = (a[0:dh] / a[dh:dh + 1]).T.astype(o_ref.dtype)


def dsa_attention(h_bf16, vt_ext, mask, bsz, seq, q_blk, k_blk, bias, t=512):
    nq = seq // t
    qi, kj = tri_pairs(nq)
    w = DSA_HEADS * HEAD_DIM
    vrows = DSA_HEADS * (HEAD_DIM + V_ONES)
    grid_spec = pltpu.PrefetchScalarGridSpec(
        num_scalar_prefetch=2,
        grid=(bsz, qi.shape[0]),
        in_specs=[pl.BlockSpec((t, w), lambda b, p, qi, kj: (b * nq + qi[p], q_blk)),
                  pl.BlockSpec((t, w), lambda b, p, qi, kj: (b * nq + kj[p], k_blk)),
                  pl.BlockSpec((vrows, t), lambda b, p, qi, kj: (b, kj[p])),
                  pl.BlockSpec((1, DSA_HEADS, t, t),
                               lambda b, p, qi, kj: (jnp.minimum(qi[p] - kj[p], 2), 0, 0, 0)),
                  pl.BlockSpec((1, t, t), lambda b, p, qi, kj: (b, kj[p], qi[p]))],
        out_specs=pl.BlockSpec((t, w), lambda b, p, qi, kj: (b * nq + qi[p], 0)),
        scratch_shapes=[pltpu.VMEM((2, t, t), F32), pltpu.VMEM((2, t, t), BF16), pltpu.VMEM((t, t), F32),
                        pltpu.VMEM((DSA_HEADS, 1, t), F32), pltpu.VMEM((DSA_HEADS, HEAD_DIM + V_ONES, t), F32)],
    )
    return pl.pallas_call(
        _dsa_attn_kernel,
        grid_spec=grid_spec,
        out_shape=jax.ShapeDtypeStruct((bsz * seq, w), BF16),
        compiler_params=_params(("arbitrary", "arbitrary"), 48),
    )(qi, kj, h_bf16, h_bf16, vt_ext, bias, mask)


_SPLITS = (512, 512, 512, 512, 64, 8, 512, 512, 512, 512, 1024, 512, 512, 4, 4)
_NAMES = ('a_q', 'a_k', 'a_v', 'a_iq', 'a_ik', 'a_iw', 'b_u', 'c_q', 'c_k', 'c_v', 'd_qk', 'd_v', 'd_o', 'd_i', 'd_f')


def _split_w_in(w_in):
    out, off = {}, 0
    for name, width in zip(_NAMES, _SPLITS):
        out[name] = w_in[:, off:off + width]
        off += width
    return out


def _group_w_in(w_in):
    c = _split_w_in(w_in)
    d = w_in.shape[0]
    z = lambda n: jnp.zeros((d, n), w_in.dtype)
    w_attn = jnp.concatenate([c['a_q'], c['a_k'], c['a_v'], c['a_iq'], c['c_q'], c['c_k'], c['c_v']], axis=1)
    w_scan = jnp.concatenate([c['d_qk'], c['b_u'], c['d_v'], c['d_o']], axis=1)
    w_misc = jnp.concatenate([c['a_ik'], z(64), z(64), c['a_ik'],
                              z(64), c['a_iw'], c['d_i'], c['d_f'], z(48)], axis=1)
    return w_attn.astype(BF16), w_scan.astype(BF16), w_misc.astype(BF16)


def _pad_ffn(w_in, w_out):
    d = w_in.shape[0]
    pad = D_FF_PAD - D_FF
    zi = jnp.zeros((d, pad), w_in.dtype)
    w_in_pad = jnp.concatenate([w_in[:, :D_FF], zi, w_in[:, D_FF:], zi], axis=1).astype(BF16)
    w_out_pad = jnp.concatenate([w_out, jnp.zeros((pad, d), w_out.dtype)], axis=0).astype(BF16)
    return w_in_pad, w_out_pad


def _branches(xn, l, bsz, seq, w_in, conv_w, i_bias, f_bias, s5, s5_d, s5_w_glu, diff_lambda,
              diff_subln_g, bias_a, bias_c):
    lambda_init = 0.8 - 0.6 * math.exp(-0.3 * l)
    w_attn, w_scan, w_misc = _group_w_in(w_in)
    ones = jnp.ones((512,), F32)
    attn_scale = jnp.concatenate([ones * (HEAD_DIM ** -0.5 * LOG2E), ones, ones, ones,
                                  ones * (DIFF_QK_DIM ** -0.5 * LOG2E), ones, ones]).reshape(1, -1)
    h_attn = matmul(xn, w_attn, attn_scale, BF16)
    h_scan = matmul(xn, w_scan, jnp.ones((1, w_scan.shape[1]), F32), F32)
    ikz, misc = misc_proj(xn, w_misc)

    mask = dsa_select(h_attn, misc, ikz, bsz, seq, iq_blk=3)
    vt_a = values_t_ext(h_attn, 2 * 512, bsz, seq, DSA_HEADS, HEAD_DIM)
    y_a = dsa_attention(h_attn, vt_a, mask, bsz, seq, 0, 1, bias_a)

    bbd, abar_re, abar_im, cbd = s5
    y_b = s5_branch(h_scan, 2, bsz, seq, bbd, abar_re, abar_im, cbd, s5_d.reshape(1, -1), s5_w_glu.astype(BF16))

    dl = diff_lambda.astype(F32)
    lam = jnp.exp(jnp.sum(dl[0] * dl[1])) - jnp.exp(jnp.sum(dl[2] * dl[3])) + lambda_init
    lam_row = jnp.full((1, LANES), lam, F32)
    vt_c = values_t_ext(h_attn, 6 * 512, bsz, seq, DIFF_HEADS, DIFF_V_DIM)
    y_c = diff_attention(h_attn, vt_c, bsz, seq, 4, 5, bias_c, lam_row, diff_subln_g, 1.0 - lambda_init)

    gate_bias = jnp.zeros((1, LANES), F32).at[0, 72:76].set(i_bias).at[0, 76:80].set(f_bias)
    y_d = mlstm_branch(h_scan, misc, gate_bias, conv_w, bsz, seq, qk_blk=0, v_blk=3, o_blk=4)
    return y_a, y_b, y_c, y_d


def _mixer(x, l, bsz, seq, norm_g, w_in, conv_w, i_bias, f_bias, s5, s5_d, s5_w_glu, diff_lambda,
           diff_subln_g, bias_a, bias_c, w_gate, b_gate, w_branch, w_out):
    xn = rmsnorm(x, norm_g, BF16)
    ys = _branches(xn, l, bsz, seq, w_in, conv_w, i_bias, f_bias, s5, s5_d, s5_w_glu, diff_lambda,
                   diff_subln_g, bias_a, bias_c)
    merged = gated_merge(xn, ys, w_gate.astype(BF16), b_gate, w_branch.astype(BF16))
    return matmul_residual(merged, w_out.astype(BF16), x)


def kernel(x, norm_g, w_ffn_in, w_ffn_out, w_in, conv_w, mlstm_i_bias, mlstm_f_bias, s5_a_re, s5_a_im,
           s5_log_dt, s5_b_re, s5_b_im, s5_c_re, s5_c_im, s5_d, s5_w_glu, diff_lambda, diff_subln_g,
           rel_table, w_gate, b_gate, w_branch, w_out, final_g):
    bsz, seq, d = x.shape
    t_attn = 512
    bias_a = bias_tiles(rel_table[:, :DSA_HEADS], t_attn)
    bias_c = bias_tiles(rel_table[:, DSA_HEADS:], t_attn)
    xf = x.reshape(bsz * seq, d)
    for l in range(DEPTH):
        xf = ffn_block(xf, norm_g[l, 0], *_pad_ffn(w_ffn_in[l, 0], w_ffn_out[l, 0]))
        s5 = s5_tables(s5_a_re[l], s5_a_im[l], s5_log_dt[l], s5_b_re[l], s5_b_im[l], s5_c_re[l], s5_c_im[l])
        xf = _mixer(xf, l, bsz, seq, norm_g[l, 1], w_in[l], conv_w[l], mlstm_i_bias[l], mlstm_f_bias[l],
                    s5, s5_d[l], s5_w_glu[l], diff_lambda[l], diff_subln_g[l], bias_a, bias_c,
                    w_gate[l], b_gate[l], w_branch[l], w_out[l])
        xf = ffn_block(xf, norm_g[l, 2], *_pad_ffn(w_ffn_in[l, 1], w_ffn_out[l, 1]))
    return rmsnorm(xf, final_g, F32).reshape(bsz, seq, d)
```

```python
import functools
import math

import jax
import jax.numpy as jnp
from jax import lax
from jax.experimental import pallas as pl
from jax.experimental.pallas import tpu as pltpu

F32 = jnp.float32
BF16 = jnp.bfloat16

D_MODEL = 2048
DEPTH = 4
N_BRANCH = 4
BRANCH_WIDTH = 512
HEAD_DIM = 128
DSA_HEADS = 4
IDX_HEADS = 8
IDX_DIM = 64
TOPK_MAX = 256
S5_WIDTH = 512
S5_GROUP = 16
S5_GROUPS = S5_WIDTH // S5_GROUP
S5_STATE = 64
DIFF_HEADS = 4
DIFF_QK_DIM = 64
DIFF_V_DIM = 128
MLSTM_HEADS = 4
MLSTM_DIM = 128
CONV_WIDTH = 4
D_FF = 5504
REL_BUCKETS = 32
REL_MAX_DIST = 128
EPS = 1e-6

LANES = 128
NEG = -1e30
INT_MIN = -(2 ** 31)
D_FF_PAD = 5632
MLSTM_L = 128

V_ONES = 16
LOG2E = math.log2(math.e)

NT_DIMS = (((1,), (1,)), ((), ()))


def _params(sem, vmem_mb):
    return pltpu.CompilerParams(dimension_semantics=sem, vmem_limit_bytes=vmem_mb * 1024 * 1024)


def _rms(x, g):
    return x * lax.rsqrt(jnp.mean(x * x, axis=-1, keepdims=True) + EPS) * g


def _rmsnorm_kernel(x_ref, g_ref, o_ref):
    o_ref[...] = _rms(x_ref[...], g_ref[...]).astype(o_ref.dtype)


def rmsnorm(x, g, out_dtype, tm=512):
    m, d = x.shape
    return pl.pallas_call(
        _rmsnorm_kernel,
        grid=(m // tm,),
        in_specs=[pl.BlockSpec((tm, d), lambda i: (i, 0)), pl.BlockSpec((1, d), lambda i: (0, 0))],
        out_specs=pl.BlockSpec((tm, d), lambda i: (i, 0)),
        out_shape=jax.ShapeDtypeStruct((m, d), out_dtype),
        compiler_params=_params(("parallel",), 40),
    )(x, g.reshape(1, d))


def _mm_kernel(x_ref, w_ref, cs_ref, o_ref):
    acc = jnp.dot(x_ref[...], w_ref[...], preferred_element_type=F32)
    o_ref[...] = (acc * cs_ref[...]).astype(o_ref.dtype)


def matmul(x, w, col_scale, out_dtype, tm=1024, tn=512):
    m, k = x.shape
    n = w.shape[1]
    return pl.pallas_call(
        _mm_kernel,
        grid=(m // tm, n // tn),
        in_specs=[pl.BlockSpec((tm, k), lambda i, j: (i, 0)), pl.BlockSpec((k, tn), lambda i, j: (0, j)),
                  pl.BlockSpec((1, tn), lambda i, j: (0, j))],
        out_specs=pl.BlockSpec((tm, tn), lambda i, j: (i, j)),
        out_shape=jax.ShapeDtypeStruct((m, n), out_dtype),
        compiler_params=_params(("parallel", "arbitrary"), 40),
    )(x, w, col_scale)


def _mm_res_kernel(x_ref, w_ref, r_ref, o_ref):
    o_ref[...] = r_ref[...] + jnp.dot(x_ref[...], w_ref[...], preferred_element_type=F32)


def matmul_residual(x, w, r, tm=1024, tn=512):
    m, k = x.shape
    n = w.shape[1]
    return pl.pallas_call(
        _mm_res_kernel,
        grid=(m // tm, n // tn),
        in_specs=[pl.BlockSpec((tm, k), lambda i, j: (i, 0)), pl.BlockSpec((k, tn), lambda i, j: (0, j)),
                  pl.BlockSpec((tm, tn), lambda i, j: (i, j))],
        out_specs=pl.BlockSpec((tm, tn), lambda i, j: (i, j)),
        out_shape=jax.ShapeDtypeStruct((m, n), F32),
        compiler_params=_params(("parallel", "arbitrary"), 40),
    )(x, w, r)


def _misc_proj_kernel(x_ref, w_ref, ikz_ref, misc_ref):
    acc = jnp.dot(x_ref[...], w_ref[...], preferred_element_type=F32)
    ikz_ref[...] = acc[:, :2 * LANES].astype(BF16)
    misc_ref[...] = acc[:, 2 * LANES:]


def misc_proj(xn, w, tm=1024):
    m, k = xn.shape
    n = w.shape[1]
    return pl.pallas_call(
        _misc_proj_kernel,
        grid=(m // tm,),
        in_specs=[pl.BlockSpec((tm, k), lambda i: (i, 0)), pl.BlockSpec((k, n), lambda i: (0, 0))],
        out_specs=[pl.BlockSpec((tm, 2 * LANES), lambda i: (i, 0)), pl.BlockSpec((tm, LANES), lambda i: (i, 0))],
        out_shape=[jax.ShapeDtypeStruct((m, 2 * LANES), BF16), jax.ShapeDtypeStruct((m, LANES), F32)],
        compiler_params=_params(("parallel",), 40),
    )(xn, w)


def _ffn_kernel(x_ref, g_ref, wg_ref, wu_ref, wo_ref, o_ref, xn_ref, acc_ref):
    j = pl.program_id(1)

    @pl.when(j == 0)
    def _():
        xn_ref[...] = _rms(x_ref[...], g_ref[...]).astype(BF16)
        acc_ref[...] = jnp.zeros_like(acc_ref)

    xn = xn_ref[...]
    g = jnp.dot(xn, wg_ref[...], preferred_element_type=F32)
    u = jnp.dot(xn, wu_ref[...], preferred_element_type=F32)
    a = (g * jax.nn.sigmoid(g) * u).astype(BF16)
    acc_ref[...] += jnp.dot(a, wo_ref[...], preferred_element_type=F32)

    @pl.when(j == pl.num_programs(1) - 1)
    def _():
        o_ref[...] = x_ref[...] + 0.5 * acc_ref[...]


def ffn_block(x, g, w_in_pad, w_out_pad, tm=512, tf=512):
    m, d = x.shape
    nf = w_out_pad.shape[0] // tf
    return pl.pallas_call(
        _ffn_kernel,
        grid=(m // tm, nf),
        in_specs=[pl.BlockSpec((tm, d), lambda i, j: (i, 0)),
                  pl.BlockSpec((1, d), lambda i, j: (0, 0)),
                  pl.BlockSpec((d, tf), lambda i, j: (0, j)),
                  pl.BlockSpec((d, tf), lambda i, j: (0, j + nf)),
                  pl.BlockSpec((tf, d), lambda i, j: (j, 0))],
        out_specs=pl.BlockSpec((tm, d), lambda i, j: (i, 0)),
        out_shape=jax.ShapeDtypeStruct((m, d), F32),
        scratch_shapes=[pltpu.VMEM((tm, d), BF16), pltpu.VMEM((tm, d), F32)],
        compiler_params=_params(("parallel", "arbitrary"), 48),
    )(x, g.reshape(1, d), w_in_pad, w_in_pad, w_out_pad)


def _merge_kernel(xn_ref, ya_ref, yb_ref, yc_ref, yd_ref, wg_ref, bg_ref, wb_ref, o_ref):
    xn = xn_ref[...]
    acc = None
    for n, y_ref in enumerate((ya_ref, yb_ref, yc_ref, yd_ref)):
        gate = jnp.dot(xn, wg_ref[n], preferred_element_type=F32) + bg_ref[n]
        proj = jnp.dot(y_ref[...], wb_ref[n], preferred_element_type=F32)
        term = jax.nn.sigmoid(gate) * proj
        acc = term if acc is None else acc + term
    o_ref[...] = acc.astype(o_ref.dtype)


def gated_merge(xn, ys, w_gate, b_gate, w_branch, tm=1024, tn=256):
    m, d = xn.shape
    bw = ys[0].shape[1]
    y_spec = pl.BlockSpec((tm, bw), lambda i, j: (i, 0))
    return pl.pallas_call(
        _merge_kernel,
        grid=(m // tm, d // tn),
        in_specs=[pl.BlockSpec((tm, d), lambda i, j: (i, 0)), y_spec, y_spec, y_spec, y_spec,
                  pl.BlockSpec((N_BRANCH, d, tn), lambda i, j: (0, 0, j)),
                  pl.BlockSpec((N_BRANCH, 1, tn), lambda i, j: (0, 0, j)),
                  pl.BlockSpec((N_BRANCH, bw, tn), lambda i, j: (0, 0, j))],
        out_specs=pl.BlockSpec((tm, tn), lambda i, j: (i, j)),
        out_shape=jax.ShapeDtypeStruct((m, d), BF16),
        compiler_params=_params(("parallel", "arbitrary"), 48),
    )(xn, *ys, w_gate, b_gate.reshape(N_BRANCH, 1, d), w_branch)


def _s5_kernel(u_ref, bbd_ref, ar_ref, ai_ref, cbd_ref, d_ref, wglu_ref, o_ref, x_ref, st_ref, *, tt):
    n = S5_GROUPS * S5_STATE

    @pl.when(pl.program_id(1) == 0)
    def _():
        st_ref[...] = jnp.zeros_like(st_ref)

    u = u_ref[...]
    x_ref[...] = jnp.dot(u.astype(BF16), bbd_ref[...], preferred_element_type=F32)
    ar = ar_ref[...]
    ai = ai_ref[...]

    def step8(i, carry):
        xr, xi = carry
        base = pl.multiple_of(i * 8, 8)
        br = x_ref[pl.ds(base, 8), 0:n]
        bi = x_ref[pl.ds(base, 8), n:2 * n]
        rows_r, rows_i = [], []
        for r in range(8):
            nr = ar * xr - ai * xi + br[r:r + 1]
            ni = ar * xi + ai * xr + bi[r:r + 1]
            xr, xi = nr, ni
            rows_r.append(xr)
            rows_i.append(xi)
        x_ref[pl.ds(base, 8), 0:n] = jnp.concatenate(rows_r, axis=0)
        x_ref[pl.ds(base, 8), n:2 * n] = jnp.concatenate(rows_i, axis=0)
        return xr, xi

    xr, xi = lax.fori_loop(0, tt // 8, step8, (st_ref[0:1, :], st_ref[1:2, :]))
    st_ref[0:1, :] = xr
    st_ref[1:2, :] = xi

    y = jnp.dot(x_ref[...].astype(BF16), cbd_ref[...], preferred_element_type=F32) + d_ref[...] * u
    z = jax.nn.gelu(y)
    gate = jnp.dot(z.astype(BF16), wglu_ref[...], preferred_element_type=F32)
    o_ref[...] = (z * jax.nn.sigmoid(gate)).astype(o_ref.dtype)


def s5_branch(h_f32, col_block, bsz, seq, bbd, abar_re, abar_im, cbd, d_skip, w_glu, tt=256):
    n = S5_GROUPS * S5_STATE
    nt = seq // tt
    const = lambda shape: pl.BlockSpec(shape, lambda b, t: (0, 0))
    return pl.pallas_call(
        functools.partial(_s5_kernel, tt=tt),
        grid=(bsz, nt),
        in_specs=[pl.BlockSpec((tt, S5_WIDTH), lambda b, t: (b * nt + t, col_block)),
                  const((S5_WIDTH, 2 * n)), const((1, n)), const((1, n)), const((2 * n, S5_WIDTH)),
                  const((1, S5_WIDTH)), const((S5_WIDTH, S5_WIDTH))],
        out_specs=pl.BlockSpec((tt, S5_WIDTH), lambda b, t: (b * nt + t, 0)),
        out_shape=jax.ShapeDtypeStruct((bsz * seq, S5_WIDTH), BF16),
        scratch_shapes=[pltpu.VMEM((tt, 2 * n), F32), pltpu.VMEM((8, n), F32)],
        compiler_params=_params(("arbitrary", "arbitrary"), 48),
    )(h_f32, bbd, abar_re, abar_im, cbd, d_skip, w_glu)


def s5_tables(a_re, a_im, log_dt, b_re, b_im, c_re, c_im):
    g, p = a_re.shape
    dt = jnp.exp(log_dt)[:, None]
    mag = jnp.exp(a_re * dt)
    ab_re, ab_im = mag * jnp.cos(a_im * dt), mag * jnp.sin(a_im * dt)
    den = a_re * a_re + a_im * a_im
    nr, ni = ab_re - 1.0, ab_im
    coef_re = (nr * a_re + ni * a_im) / den
    coef_im = (ni * a_re - nr * a_im) / den
    bb_re = coef_re[..., None] * b_re - coef_im[..., None] * b_im
    bb_im = coef_re[..., None] * b_im + coef_im[..., None] * b_re
    eye = jnp.eye(g, dtype=F32)
    to_bd_in = lambda t: jnp.einsum('gpc,gh->gchp', t, eye).reshape(g * S5_GROUP, g * p)
    to_bd_out = lambda t: jnp.einsum('gcp,gh->gphc', t, eye).reshape(g * p, g * S5_GROUP)
    bbd = jnp.concatenate([to_bd_in(bb_re), to_bd_in(bb_im)], axis=1).astype(BF16)
    cbd = jnp.concatenate([to_bd_out(c_re), to_bd_out(-c_im)], axis=0).astype(BF16)
    return bbd, ab_re.reshape(1, g * p), ab_im.reshape(1, g * p), cbd


def _log_sigmoid(x):
    return jnp.minimum(x, 0.0) - jnp.log1p(jnp.exp(-jnp.abs(x)))


def _mlstm_kernel(qk_ref, v_ref, og_ref, gate_ref, gbias_ref, convw_ref, o_ref,
                  xbuf_ref, c_ref, n_ref, m_ref, *, tt):
    L = MLSTM_L
    hd = MLSTM_DIM
    nh = MLSTM_HEADS
    i_col, f_col = 72, 76

    @pl.when(pl.program_id(1) == 0)
    def _():
        xbuf_ref[0:8, :] = jnp.zeros((8, 2 * nh * hd), F32)
        c_ref[...] = jnp.zeros_like(c_ref)
        n_ref[...] = jnp.zeros_like(n_ref)
        m_ref[...] = jnp.zeros_like(m_ref)

    xbuf_ref[8:8 + tt, :] = qk_ref[...]
    cw = convw_ref[...]
    conv = xbuf_ref[8:8 + tt, :] * cw[CONV_WIDTH - 1:CONV_WIDTH]
    for sh in range(1, CONV_WIDTH):
        conv = conv + xbuf_ref[8 - sh:8 - sh + tt, :] * cw[CONV_WIDTH - 1 - sh:CONV_WIDTH - sh]
    xbuf_ref[0:8, :] = xbuf_ref[tt:tt + 8, :]
    qk = conv * jax.nn.sigmoid(conv)

    gates = gate_ref[...] + gbias_ref[...]
    logf = _log_sigmoid(gates)
    row = lax.broadcasted_iota(jnp.int32, (L, L), 0)
    col = lax.broadcasted_iota(jnp.int32, (L, L), 1)
    causal = col <= row
    tri = jnp.where(causal, 1.0, 0.0).astype(F32)

    for c in range(tt // L):
        r0 = c * L
        gc = gates[r0:r0 + L]
        bcum = jnp.dot(tri, logf[r0:r0 + L], precision=lax.Precision.HIGHEST, preferred_element_type=F32)
        gct = gc.T
        bcumt = bcum.T
        for h in range(nh):
            ig_col = gc[:, i_col + h:i_col + h + 1]
            b_col = bcum[:, f_col + h:f_col + h + 1]
            ig_row = gct[i_col + h:i_col + h + 1, :]
            b_row = bcumt[f_col + h:f_col + h + 1, :]
            b_last = b_col[L - 1:L, :]
            q = qk[r0:r0 + L, h * hd:(h + 1) * hd]
            k = qk[r0:r0 + L, (nh + h) * hd:(nh + h + 1) * hd] * (hd ** -0.5)
            vb = v_ref[r0:r0 + L, h * hd:(h + 1) * hd].astype(BF16)
            qb = q.astype(BF16)
            kb = k.astype(BF16)
            c_prev = c_ref[h]
            n_prev = n_ref[h]
            m_prev = m_ref[h][:, 0:1]

            g = b_last - b_col + ig_col
            m_loc = jnp.max(g, axis=0, keepdims=True)
            wk = jnp.exp(g - m_loc) * k
            c_loc = jnp.dot(wk.T.astype(BF16), vb, preferred_element_type=F32)
            n_loc = jnp.sum(wk, axis=0, keepdims=True)

            m_inter = b_col + m_prev
            log_d = jnp.where(causal, b_col - b_row + ig_row, -jnp.inf)
            m_j = jnp.maximum(jnp.max(log_d, axis=1, keepdims=True), m_inter)
            dqk = jnp.exp(log_d - m_j) * lax.dot_general(qb, kb, NT_DIMS, preferred_element_type=F32)
            inter = jnp.exp(m_inter - m_j)
            num = (jnp.dot(dqk.astype(BF16), vb, preferred_element_type=F32)
                   + inter * jnp.dot(qb, c_prev.astype(BF16), preferred_element_type=F32))
            den = jnp.sum(dqk, axis=1, keepdims=True) + inter * jnp.sum(q * n_prev, axis=1, keepdims=True)
            hid = num / jnp.maximum(jnp.abs(den), jnp.exp(-m_j))
            og = jax.nn.sigmoid(og_ref[r0:r0 + L, h * hd:(h + 1) * hd])
            o_ref[r0:r0 + L, h * hd:(h + 1) * hd] = (og * hid).astype(o_ref.dtype)

            m_new = jnp.maximum(b_last + m_prev, m_loc)
            a = jnp.exp(b_last + m_prev - m_new)
            bcoef = jnp.exp(m_loc - m_new)
            c_ref[h] = a * c_prev + bcoef * c_loc
            n_ref[h] = a * n_prev + bcoef * n_loc
            m_ref[h] = jnp.broadcast_to(m_new, (1, LANES))


def mlstm_branch(h_f32, misc, gate_bias, conv_w, bsz, seq, qk_blk, v_blk, o_blk, tt=512):
    nh, hd = MLSTM_HEADS, MLSTM_DIM
    w = nh * hd
    nt = seq // tt
    return pl.pallas_call(
        functools.partial(_mlstm_kernel, tt=tt),
        grid=(bsz, nt),
        in_specs=[pl.BlockSpec((tt, 2 * w), lambda b, t: (b * nt + t, qk_blk)),
                  pl.BlockSpec((tt, w), lambda b, t: (b * nt + t, v_blk)),
                  pl.BlockSpec((tt, w), lambda b, t: (b * nt + t, o_blk)),
                  pl.BlockSpec((tt, LANES), lambda b, t: (b * nt + t, 0)),
                  pl.BlockSpec((1, LANES), lambda b, t: (0, 0)),
                  pl.BlockSpec((CONV_WIDTH, 2 * w), lambda b, t: (0, 0))],
        out_specs=pl.BlockSpec((tt, w), lambda b, t: (b * nt + t, 0)),
        out_shape=jax.ShapeDtypeStruct((bsz * seq, w), BF16),
        scratch_shapes=[pltpu.VMEM((tt + 8, 2 * w), F32), pltpu.VMEM((nh, hd, hd), F32),
                        pltpu.VMEM((nh, 1, hd), F32), pltpu.VMEM((nh, 1, LANES), F32)],
        compiler_params=_params(("arbitrary", "arbitrary"), 48),
    )(h_f32, h_f32, h_f32, misc, gate_bias, conv_w)


def rel_bucket(dist):
    max_exact = REL_BUCKETS // 2
    n = jnp.maximum(dist, 0)
    large = max_exact + (jnp.log(jnp.maximum(n, 1).astype(F32) / max_exact)
                         / math.log(REL_MAX_DIST / max_exact)
                         * (REL_BUCKETS - max_exact)).astype(jnp.int32)
    large = jnp.minimum(large, REL_BUCKETS - 1)
    return jnp.where(n < max_exact, n, large)


def bias_tiles(rel, t):
    assert t >= REL_MAX_DIST
    kk = jnp.arange(t)[:, None]
    qq = jnp.arange(t)[None, :]

    def lookup(dist):
        onehot = (rel_bucket(dist)[..., None] == jnp.arange(REL_BUCKETS)).astype(F32)
        return jnp.einsum('kqb,bh->hkq', onehot, rel.astype(F32), precision=lax.Precision.HIGHEST) * LOG2E

    t0 = jnp.where((qq >= kk)[None], lookup(qq - kk), NEG)
    t1 = lookup(t + qq - kk)
    t2 = lookup(jnp.full((t, t), 2 * t))
    return jnp.stack([t0, t1, t2])


def tri_pairs(nq):
    qi = [q for q in range(nq) for _ in range(q + 1)]
    kj = [k for q in range(nq) for k in range(q + 1)]
    return jnp.asarray(qi, jnp.int32), jnp.asarray(kj, jnp.int32)


def values_t_ext(h_bf16, col0, bsz, seq, heads, dv):
    v = h_bf16[:, col0:col0 + heads * dv].reshape(bsz, seq, heads, dv)
    vt = jnp.transpose(v, (0, 2, 3, 1))
    ones = jnp.ones((bsz, heads, V_ONES, seq), BF16)
    return jnp.concatenate([vt, ones], axis=2).reshape(bsz * heads * (dv + V_ONES), seq)


def _attn_maps(qi, kj, maps, bias_ref, mb_ref, s_ref, p_ref, m_ref, acc_ref):
    def run(const_bias):
        def qk(i):
            kh, qz, _, _ = maps[i]
            s_ref[i % 2] = lax.dot_general(kh, qz, NT_DIMS, preferred_element_type=F32)

        qk(0)
        for i, (_, _, vt_ext, head) in enumerate(maps):
            if i + 1 < len(maps):
                qk(i + 1)
            _softmax_pv(i, vt_ext, bias_ref, head, mb_ref, const_bias, s_ref, p_ref, m_ref, acc_ref)

    far = qi - kj >= 2

    @pl.when(far)
    def _():
        run(True)

    @pl.when(jnp.logical_not(far))
    def _():
        run(False)


def _softmax_pv(idx, vt_ext, bias_ref, head, mb_ref, const_bias, s_ref, p_ref, m_ref, acc_ref):
    tk, tq = s_ref.shape[1:]
    sb = s_ref.at[idx % 2]
    pb = p_ref.at[idx % 2]

    def logits(r0, n):
        t = sb[r0:r0 + n, :]
        if not const_bias:
            t = t + bias_ref[0, head, r0:r0 + n, :]
        if mb_ref is not None:
            t = t + mb_ref[r0:r0 + n, :]
        return t

    mx = [jnp.full((8, tq), -jnp.inf, F32) for _ in range(4)]
    for c in range(tk // 8):
        mx[c % 4] = jnp.maximum(mx[c % 4], logits(c * 8, 8))
    m_tile = jnp.max(jnp.maximum(jnp.maximum(mx[0], mx[1]), jnp.maximum(mx[2], mx[3])), axis=0, keepdims=True)
    m_prev = m_ref[idx]
    if const_bias:
        bias_c = bias_ref[0, head, 0:1, :]
        m_new = jnp.maximum(m_prev, m_tile + bias_c)
        m_b = jnp.broadcast_to(m_new - bias_c, (16, tq))
    else:
        m_new = jnp.maximum(m_prev, m_tile)
        m_b = jnp.broadcast_to(m_new, (16, tq))
    m_ref[idx] = m_new
    for c in range(tk // 16):
        pb[c * 16:(c + 1) * 16, :] = jnp.exp2(logits(c * 16, 16) - m_b).astype(BF16)
    alpha = jnp.exp2(m_prev - m_new)
    acc_ref[idx] = alpha * acc_ref[idx] + jnp.dot(vt_ext, pb[...], preferred_element_type=F32)


def _init_softmax_state(m_ref, acc_ref):
    m_ref[...] = jnp.full(m_ref.shape, NEG, F32)
    acc_ref[...] = jnp.zeros_like(acc_ref)


def _diff_attn_kernel(qi_ref, kj_ref, q_ref, k_ref, vt_ref, bias_ref, lam_ref, gb_ref, o_ref,
                      s_ref, p_ref, m_ref, acc_ref, *, out_scale):
    p = pl.program_id(1)
    qi = qi_ref[p]
    kj = kj_ref[p]
    dq = DIFF_QK_DIM
    dv = DIFF_V_DIM
    vrows = dv + V_ONES

    @pl.when(kj == 0)
    def _():
        _init_softmax_state(m_ref, acc_ref)

    lane = lax.broadcasted_iota(jnp.int32, (q_ref.shape[0], 2 * dq), 1)
    maps = []
    for h in range(DIFF_HEADS):
        qh = q_ref[:, h * 2 * dq:(h + 1) * 2 * dq]
        kh = k_ref[:, h * 2 * dq:(h + 1) * 2 * dq]
        vt = vt_ref[h * vrows:(h + 1) * vrows, :]
        for mi in range(2):
            qz = jnp.where((lane < dq) if mi == 0 else (lane >= dq), qh, jnp.zeros_like(qh))
            maps.append((kh, qz, vt, h))
    _attn_maps(qi, kj, maps, bias_ref, None, s_ref, p_ref, m_ref, acc_ref)

    @pl.when(kj == qi)
    def _():
        lam = lam_ref[0:1, 0:1]
        for h in range(DIFF_HEADS):
            a1 = acc_ref[2 * h]
            a2 = acc_ref[2 * h + 1]
            a = a1[0:dv] / a1[dv:dv + 1] - lam * (a2[0:dv] / a2[dv:dv + 1])
            y = a * lax.rsqrt(jnp.mean(a * a, axis=0, keepdims=True) + EPS) * gb_ref[...] * out_scale
            o_ref[:, h * dv:(h + 1) * dv] = y.T.astype(o_ref.dtype)


def diff_attention(h_bf16, vt_ext, bsz, seq, q_blk, k_blk, bias, lam_row, subln_g, out_scale, t=512):
    nq = seq // t
    qi, kj = tri_pairs(nq)
    w = DIFF_HEADS * DIFF_V_DIM
    nmap = 2 * DIFF_HEADS
    vrows = DIFF_HEADS * (DIFF_V_DIM + V_ONES)
    g_b = jnp.broadcast_to(subln_g.reshape(DIFF_V_DIM, 1), (DIFF_V_DIM, t))
    grid_spec = pltpu.PrefetchScalarGridSpec(
        num_scalar_prefetch=2,
        grid=(bsz, qi.shape[0]),
        in_specs=[pl.BlockSpec((t, w), lambda b, p, qi, kj: (b * nq + qi[p], q_blk)),
                  pl.BlockSpec((t, w), lambda b, p, qi, kj: (b * nq + kj[p], k_blk)),
                  pl.BlockSpec((vrows, t), lambda b, p, qi, kj: (b, kj[p])),
                  pl.BlockSpec((1, DIFF_HEADS, t, t),
                               lambda b, p, qi, kj: (jnp.minimum(qi[p] - kj[p], 2), 0, 0, 0)),
                  pl.BlockSpec((1, LANES), lambda b, p, qi, kj: (0, 0)),
                  pl.BlockSpec((DIFF_V_DIM, t), lambda b, p, qi, kj: (0, 0))],
        out_specs=pl.BlockSpec((t, w), lambda b, p, qi, kj: (b * nq + qi[p], 0)),
        scratch_shapes=[pltpu.VMEM((2, t, t), F32), pltpu.VMEM((2, t, t), BF16),
                        pltpu.VMEM((nmap, 1, t), F32), pltpu.VMEM((nmap, DIFF_V_DIM + V_ONES, t), F32)],
    )
    return pl.pallas_call(
        functools.partial(_diff_attn_kernel, out_scale=out_scale),
        grid_spec=grid_spec,
        out_shape=jax.ShapeDtypeStruct((bsz * seq, w), BF16),
        compiler_params=_params(("arbitrary", "arbitrary"), 48),
    )(qi, kj, h_bf16, h_bf16, vt_ext, bias, lam_row, g_b)


def _sort_key(x):
    b = lax.bitcast_convert_type(x, jnp.int32)
    return b ^ ((b >> 31) & 0x7FFFFFFF)


def _dsa_select_kernel(iq_ref, w_ref, ikz_ref, o_ref, key_ref, a_ref, wb_ref, *, tq, tk, topk, seq):
    i = pl.program_id(1)
    last = (i * tq + tq - 1) // tk
    idx_scale = (IDX_DIM * IDX_HEADS) ** -0.5
    iw_row = 64

    wt = w_ref[...].T
    for h in range(IDX_HEADS):
        wb_ref[h] = jnp.broadcast_to(wt[iw_row + h:iw_row + h + 1, :], (8, tq))

    def in_prefix(off, n):
        key_pos = off + lax.broadcasted_iota(jnp.int32, (n, tq), 0)
        q_pos = i * tq + lax.broadcasted_iota(jnp.int32, (n, tq), 1)
        return key_pos <= q_pos

    def score_tile(j, diagonal):
        off = pl.multiple_of(j * tk, tk)
        for pair in range(IDX_HEADS // 2):
            qp = iq_ref[:, pair * LANES:(pair + 1) * LANES]
            for half in range(2):
                a_ref[2 * pair + half] = lax.dot_general(
                    ikz_ref[pl.ds(off, tk), half * LANES:(half + 1) * LANES], qp, NT_DIMS,
                    preferred_element_type=F32)
        for c in range(tk // 8):
            r0 = c * 8
            sc = wb_ref[0] * jnp.maximum(a_ref[0, r0:r0 + 8, :], 0.0)
            for h in range(1, IDX_HEADS):
                sc = sc + wb_ref[h] * jnp.maximum(a_ref[h, r0:r0 + 8, :], 0.0)
            sc = sc * idx_scale
            if diagonal:
                sc = jnp.where(in_prefix(off + r0, 8), sc, -jnp.inf)
            key_ref[pl.ds(pl.multiple_of(off + r0, 8), 8), :] = _sort_key(sc)

    def score_body(j, carry):
        score_tile(j, False)
        return carry

    lax.fori_loop(0, last, score_body, 0)
    score_tile(last, True)

    n_acc = 8

    def bit_body(it, thr):
        cand = thr + lax.shift_left(jnp.int32(1), 31 - it)
        cand_b = jnp.broadcast_to(cand, (8, tq))

        def count_body(j, accs):
            blk = key_ref[pl.ds(pl.multiple_of(j * tk, tk), tk), :]
            accs = list(accs)
            for c in range(tk // 8):
                accs[c % n_acc] = accs[c % n_acc] + jnp.where(blk[c * 8:(c + 1) * 8, :] >= cand_b, 1.0, 0.0)
            return tuple(accs)

        zero = jnp.zeros((8, tq), F32)
        accs = list(lax.fori_loop(0, last + 1, count_body, (zero,) * n_acc))
        while len(accs) > 1:
            accs = [a + b for a, b in zip(accs[0::2], accs[1::2])]
        cnt = jnp.sum(accs[0], axis=0, keepdims=True)
        return jnp.where(cnt >= topk, cand, thr)

    thr = lax.fori_loop(0, 32, bit_body, jnp.full((1, tq), INT_MIN, jnp.int32))
    thr_b = jnp.broadcast_to(thr, (16, tq))

    def out_tile(j, diagonal):
        off = pl.multiple_of(j * tk, tk)
        for c in range(tk // 16):
            rows = pl.ds(pl.multiple_of(off + c * 16, 16), 16)
            keep = key_ref[rows, :] >= thr_b
            if diagonal:
                keep = keep & in_prefix(off + c * 16, 16)
            o_ref[0, rows, :] = jnp.where(keep, 0.0, NEG).astype(o_ref.dtype)

    def out_body(j, carry):
        out_tile(j, False)
        return carry

    lax.fori_loop(0, last, out_body, 0)
    out_tile(last, True)

    def fill_body(j, carry):
        o_ref[0, pl.ds(pl.multiple_of(j * tk, tk), tk), :] = jnp.full((tk, tq), NEG, o_ref.dtype)
        return carry

    lax.fori_loop(last + 1, seq // tk, fill_body, 0)


def dsa_select(h_bf16, misc, ikz, bsz, seq, iq_blk, tq=256, tk=512):
    nq = seq // tq
    topk = min(TOPK_MAX, seq // 4)
    w = IDX_HEADS * IDX_DIM
    return pl.pallas_call(
        functools.partial(_dsa_select_kernel, tq=tq, tk=tk, topk=topk, seq=seq),
        grid=(bsz, nq),
        in_specs=[pl.BlockSpec((tq, w), lambda b, i: (b * nq + i, iq_blk)),
                  pl.BlockSpec((tq, LANES), lambda b, i: (b * nq + i, 0)),
                  pl.BlockSpec((seq, 2 * LANES), lambda b, i: (b, 0), pipeline_mode=pl.Buffered(1))],
        out_specs=pl.BlockSpec((1, seq, tq), lambda b, i: (b, 0, i)),
        out_shape=jax.ShapeDtypeStruct((bsz, seq, seq), BF16),
        scratch_shapes=[pltpu.VMEM((seq, tq), jnp.int32), pltpu.VMEM((IDX_HEADS, tk, tq), F32),
                        pltpu.VMEM((IDX_HEADS, 8, tq), F32)],
        compiler_params=_params(("arbitrary", "arbitrary"), 56),
    )(h_bf16, misc, ikz)


def _dsa_attn_kernel(qi_ref, kj_ref, q_ref, k_ref, vt_ref, bias_ref, mask_ref, o_ref,
                     s_ref, p_ref, mb_ref, m_ref, acc_ref):
    p = pl.program_id(1)
    qi = qi_ref[p]
    kj = kj_ref[p]
    dh = HEAD_DIM
    vrows = dh + V_ONES

    @pl.when(kj == 0)
    def _():
        _init_softmax_state(m_ref, acc_ref)

    mb_ref[...] = mask_ref[0].astype(F32)
    maps = [(k_ref[:, h * dh:(h + 1) * dh], q_ref[:, h * dh:(h + 1) * dh], vt_ref[h * vrows:(h + 1) * vrows, :], h)
            for h in range(DSA_HEADS)]
    _attn_maps(qi, kj, maps, bias_ref, mb_ref, s_ref, p_ref, m_ref, acc_ref)

    @pl.when(kj == qi)
    def _():
        for h in range(DSA_HEADS):
            a = acc_ref[h]
            o_ref[:, h * dh:(h + 1) * dh] = (a[0:dh] / a[dh:dh + 1]).T.astype(o_ref.dtype)


def dsa_attention(h_bf16, vt_ext, mask, bsz, seq, q_blk, k_blk, bias, t=512):
    nq = seq // t
    qi, kj = tri_pairs(nq)
    w = DSA_HEADS * HEAD_DIM
    vrows = DSA_HEADS * (HEAD_DIM + V_ONES)
    grid_spec = pltpu.PrefetchScalarGridSpec(
        num_scalar_prefetch=2,
        grid=(bsz, qi.shape[0]),
        in_specs=[pl.BlockSpec((t, w), lambda b, p, qi, kj: (b * nq + qi[p], q_blk)),
                  pl.BlockSpec((t, w), lambda b, p, qi, kj: (b * nq + kj[p], k_blk)),
                  pl.BlockSpec((vrows, t), lambda b, p, qi, kj: (b, kj[p])),
                  pl.BlockSpec((1, DSA_HEADS, t, t),
                               lambda b, p, qi, kj: (jnp.minimum(qi[p] - kj[p], 2), 0, 0, 0)),
                  pl.BlockSpec((1, t, t), lambda b, p, qi, kj: (b, kj[p], qi[p]))],
        out_specs=pl.BlockSpec((t, w), lambda b, p, qi, kj: (b * nq + qi[p], 0)),
        scratch_shapes=[pltpu.VMEM((2, t, t), F32), pltpu.VMEM((2, t, t), BF16), pltpu.VMEM((t, t), F32),
                        pltpu.VMEM((DSA_HEADS, 1, t), F32), pltpu.VMEM((DSA_HEADS, HEAD_DIM + V_ONES, t), F32)],
    )
    return pl.pallas_call(
        _dsa_attn_kernel,
        grid_spec=grid_spec,
        out_shape=jax.ShapeDtypeStruct((bsz * seq, w), BF16),
        compiler_params=_params(("arbitrary", "arbitrary"), 48),
    )(qi, kj, h_bf16, h_bf16, vt_ext, bias, mask)


_SPLITS = (512, 512, 512, 512, 64, 8, 512, 512, 512, 512, 1024, 512, 512, 4, 4)
_NAMES = ('a_q', 'a_k', 'a_v', 'a_iq', 'a_ik', 'a_iw', 'b_u', 'c_q', 'c_k', 'c_v', 'd_qk', 'd_v', 'd_o', 'd_i', 'd_f')


def _split_w_in(w_in):
    out, off = {}, 0
    for name, width in zip(_NAMES, _SPLITS):
        out[name] = w_in[:, off:off + width]
        off += width
    return out


def _group_w_in(w_in):
    c = _split_w_in(w_in)
    d = w_in.shape[0]
    z = lambda n: jnp.zeros((d, n), w_in.dtype)
    w_attn = jnp.concatenate([c['a_q'], c['a_k'], c['a_v'], c['a_iq'], c['c_q'], c['c_k'], c['c_v']], axis=1)
    w_scan = jnp.concatenate([c['d_qk'], c['b_u'], c['d_v'], c['d_o']], axis=1)
    w_misc = jnp.concatenate([c['a_ik'], z(64), z(64), c['a_ik'],
                              z(64), c['a_iw'], c['d_i'], c['d_f'], z(48)], axis=1)
    return w_attn.astype(BF16), w_scan.astype(BF16), w_misc.astype(BF16)


def _pad_ffn(w_in, w_out):
    d = w_in.shape[0]
    pad = D_FF_PAD - D_FF
    zi = jnp.zeros((d, pad), w_in.dtype)
    w_in_pad = jnp.concatenate([w_in[:, :D_FF], zi, w_in[:, D_FF:], zi], axis=1).astype(BF16)
    w_out_pad = jnp.concatenate([w_out, jnp.zeros((pad, d), w_out.dtype)], axis=0).astype(BF16)
    return w_in_pad, w_out_pad


def _branches(xn, l, bsz, seq, w_in, conv_w, i_bias, f_bias, s5, s5_d, s5_w_glu, diff_lambda,
              diff_subln_g, bias_a, bias_c):
    lambda_init = 0.8 - 0.6 * math.exp(-0.3 * l)
    w_attn, w_scan, w_misc = _group_w_in(w_in)
    ones = jnp.ones((512,), F32)
    attn_scale = jnp.concatenate([ones * (HEAD_DIM ** -0.5 * LOG2E), ones, ones, ones,
                                  ones * (DIFF_QK_DIM ** -0.5 * LOG2E), ones, ones]).reshape(1, -1)
    h_attn = matmul(xn, w_attn, attn_scale, BF16)
    h_scan = matmul(xn, w_scan, jnp.ones((1, w_scan.shape[1]), F32), F32)
    ikz, misc = misc_proj(xn, w_misc)

    mask = dsa_select(h_attn, misc, ikz, bsz, seq, iq_blk=3)
    vt_a = values_t_ext(h_attn, 2 * 512, bsz, seq, DSA_HEADS, HEAD_DIM)
    y_a = dsa_attention(h_attn, vt_a, mask, bsz, seq, 0, 1, bias_a)

    bbd, abar_re, abar_im, cbd = s5
    y_b = s5_branch(h_scan, 2, bsz, seq, bbd, abar_re, abar_im, cbd, s5_d.reshape(1, -1), s5_w_glu.astype(BF16))

    dl = diff_lambda.astype(F32)
    lam = jnp.exp(jnp.sum(dl[0] * dl[1])) - jnp.exp(jnp.sum(dl[2] * dl[3])) + lambda_init
    lam_row = jnp.full((1, LANES), lam, F32)
    vt_c = values_t_ext(h_attn, 6 * 512, bsz, seq, DIFF_HEADS, DIFF_V_DIM)
    y_c = diff_attention(h_attn, vt_c, bsz, seq, 4, 5, bias_c, lam_row, diff_subln_g, 1.0 - lambda_init)

    gate_bias = jnp.zeros((1, LANES), F32).at[0, 72:76].set(i_bias).at[0, 76:80].set(f_bias)
    y_d = mlstm_branch(h_scan, misc, gate_bias, conv_w, bsz, seq, qk_blk=0, v_blk=3, o_blk=4)
    return y_a, y_b, y_c, y_d


def _mixer(x, l, bsz, seq, norm_g, w_in, conv_w, i_bias, f_bias, s5, s5_d, s5_w_glu, diff_lambda,
           diff_subln_g, bias_a, bias_c, w_gate, b_gate, w_branch, w_out):
    xn = rmsnorm(x, norm_g, BF16)
    ys = _branches(xn, l, bsz, seq, w_in, conv_w, i_bias, f_bias, s5, s5_d, s5_w_glu, diff_lambda,
                   diff_subln_g, bias_a, bias_c)
    merged = gated_merge(xn, ys, w_gate.astype(BF16), b_gate, w_branch.astype(BF16))
    return matmul_residual(merged, w_out.astype(BF16), x)


def kernel(x, norm_g, w_ffn_in, w_ffn_out, w_in, conv_w, mlstm_i_bias, mlstm_f_bias, s5_a_re, s5_a_im,
           s5_log_dt, s5_b_re, s5_b_im, s5_c_re, s5_c_im, s5_d, s5_w_glu, diff_lambda, diff_subln_g,
           rel_table, w_gate, b_gate, w_branch, w_out, final_g):
    bsz, seq, d = x.shape
    t_attn = 512
    bias_a = bias_tiles(rel_table[:, :DSA_HEADS], t_attn)
    bias_c = bias_tiles(rel_table[:, DSA_HEADS:], t_attn)
    xf = x.reshape(bsz * seq, d)
    for l in range(DEPTH):
        xf = ffn_block(xf, norm_g[l, 0], *_pad_ffn(w_ffn_in[l, 0], w_ffn_out[l, 0]))
        s5 = s5_tables(s5_a_re[l], s5_a_im[l], s5_log_dt[l], s5_b_re[l], s5_b_im[l], s5_c_re[l], s5_c_im[l])
        xf = _mixer(xf, l, bsz, seq, norm_g[l, 1], w_in[l], conv_w[l], mlstm_i_bias[l], mlstm_f_bias[l],
                    s5, s5_d[l], s5_w_glu[l], diff_lambda[l], diff_subln_g[l], bias_a, bias_c,
                    w_gate[l], b_gate[l], w_branch[l], w_out[l])
        xf = ffn_block(xf, norm_g[l, 2], *_pad_ffn(w_ffn_in[l, 1], w_ffn_out[l, 1]))
    return rmsnorm(xf, final_g, F32).reshape(bsz, seq, d)
```

```python
import functools
import math

import jax
import jax.numpy as jnp
from jax import lax
from jax.experimental import pallas as pl
from jax.experimental.pallas import tpu as pltpu

F32 = jnp.float32
BF16 = jnp.bfloat16

D_MODEL = 2048
DEPTH = 4
N_BRANCH = 4
BRANCH_WIDTH = 512
HEAD_DIM = 128
DSA_HEADS = 4
IDX_HEADS = 8
IDX_DIM = 64
TOPK_MAX = 256
S5_WIDTH = 512
S5_GROUP = 16
S5_GROUPS = S5_WIDTH // S5_GROUP
S5_STATE = 64
DIFF_HEADS = 4
DIFF_QK_DIM = 64
DIFF_V_DIM = 128
MLSTM_HEADS = 4
MLSTM_DIM = 128
CONV_WIDTH = 4
D_FF = 5504
REL_BUCKETS = 32
REL_MAX_DIST = 128
EPS = 1e-6

LANES = 128
NEG = -1e30
INT_MIN = -(2 ** 31)
D_FF_PAD = 5632
MLSTM_L = 128

V_ONES = 16
LOG2E = math.log2(math.e)

NT_DIMS = (((1,), (1,)), ((), ()))


def _params(sem, vmem_mb):
    return pltpu.CompilerParams(dimension_semantics=sem, vmem_limit_bytes=vmem_mb * 1024 * 1024)


def _rms(x, g):
    return x * lax.rsqrt(jnp.mean(x * x, axis=-1, keepdims=True) + EPS) * g


def _rmsnorm_kernel(x_ref, g_ref, o_ref):
    o_ref[...] = _rms(x_ref[...], g_ref[...]).astype(o_ref.dtype)


def rmsnorm(x, g, out_dtype, tm=512):
    m, d = x.shape
    return pl.pallas_call(
        _rmsnorm_kernel,
        grid=(m // tm,),
        in_specs=[pl.BlockSpec((tm, d), lambda i: (i, 0)), pl.BlockSpec((1, d), lambda i: (0, 0))],
        out_specs=pl.BlockSpec((tm, d), lambda i: (i, 0)),
        out_shape=jax.ShapeDtypeStruct((m, d), out_dtype),
        compiler_params=_params(("parallel",), 40),
    )(x, g.reshape(1, d))


def _mm_kernel(x_ref, w_ref, cs_ref, o_ref):
    acc = jnp.dot(x_ref[...], w_ref[...], preferred_element_type=F32)
    o_ref[...] = (acc * cs_ref[...]).astype(o_ref.dtype)


def matmul(x, w, col_scale, out_dtype, tm=1024, tn=512):
    m, k = x.shape
    n = w.shape[1]
    return pl.pallas_call(
        _mm_kernel,
        grid=(m // tm, n // tn),
        in_specs=[pl.BlockSpec((tm, k), lambda i, j: (i, 0)), pl.BlockSpec((k, tn), lambda i, j: (0, j)),
                  pl.BlockSpec((1, tn), lambda i, j: (0, j))],
        out_specs=pl.BlockSpec((tm, tn), lambda i, j: (i, j)),
        out_shape=jax.ShapeDtypeStruct((m, n), out_dtype),
        compiler_params=_params(("parallel", "arbitrary"), 40),
    )(x, w, col_scale)


def _mm_res_kernel(x_ref, w_ref, r_ref, o_ref):
    o_ref[...] = r_ref[...] + jnp.dot(x_ref[...], w_ref[...], preferred_element_type=F32)


def matmul_residual(x, w, r, tm=1024, tn=512):
    m, k = x.shape
    n = w.shape[1]
    return pl.pallas_call(
        _mm_res_kernel,
        grid=(m // tm, n // tn),
        in_specs=[pl.BlockSpec((tm, k), lambda i, j: (i, 0)), pl.BlockSpec((k, tn), lambda i, j: (0, j)),
                  pl.BlockSpec((tm, tn), lambda i, j: (i, j))],
        out_specs=pl.BlockSpec((tm, tn), lambda i, j: (i, j)),
        out_shape=jax.ShapeDtypeStruct((m, n), F32),
        compiler_params=_params(("parallel", "arbitrary"), 40),
    )(x, w, r)


def _misc_proj_kernel(x_ref, w_ref, ikz_ref, misc_ref):
    acc = jnp.dot(x_ref[...], w_ref[...], preferred_element_type=F32)
    ikz_ref[...] = acc[:, :2 * LANES].astype(BF16)
    misc_ref[...] = acc[:, 2 * LANES:]


def misc_proj(xn, w, tm=1024):
    m, k = xn.shape
    n = w.shape[1]
    return pl.pallas_call(
        _misc_proj_kernel,
        grid=(m // tm,),
        in_specs=[pl.BlockSpec((tm, k), lambda i: (i, 0)), pl.BlockSpec((k, n), lambda i: (0, 0))],
        out_specs=[pl.BlockSpec((tm, 2 * LANES), lambda i: (i, 0)), pl.BlockSpec((tm, LANES), lambda i: (i, 0))],
        out_shape=[jax.ShapeDtypeStruct((m, 2 * LANES), BF16), jax.ShapeDtypeStruct((m, LANES), F32)],
        compiler_params=_params(("parallel",), 40),
    )(xn, w)


def _ffn_kernel(x_ref, g_ref, wg_ref, wu_ref, wo_ref, o_ref, xn_ref):
    j = pl.program_id(1)

    @pl.when(j == 0)
    def _():
        x = x_ref[...]
        xn_ref[...] = _rms(x, g_ref[...]).astype(BF16)
        o_ref[...] = x

    xn = xn_ref[...]
    g = jnp.dot(xn, wg_ref[...], preferred_element_type=F32)
    u = jnp.dot(xn, wu_ref[...], preferred_element_type=F32)
    a = (g * jax.nn.sigmoid(g) * u * 0.5).astype(BF16)
    o_ref[...] += jnp.dot(a, wo_ref[...], preferred_element_type=F32)


def ffn_block(x, g, w_in_pad, w_out_pad, tm=1024, tf=512):
    m, d = x.shape
    nf = w_out_pad.shape[0] // tf
    return pl.pallas_call(
        _ffn_kernel,
        grid=(m // tm, nf),
        in_specs=[pl.BlockSpec((tm, d), lambda i, j: (i, 0), pipeline_mode=pl.Buffered(1)),
                  pl.BlockSpec((1, d), lambda i, j: (0, 0)),
                  pl.BlockSpec((d, tf), lambda i, j: (0, j)),
                  pl.BlockSpec((d, tf), lambda i, j: (0, j + nf)),
                  pl.BlockSpec((tf, d), lambda i, j: (j, 0))],
        out_specs=pl.BlockSpec((tm, d), lambda i, j: (i, 0)),
        out_shape=jax.ShapeDtypeStruct((m, d), F32),
        scratch_shapes=[pltpu.VMEM((tm, d), BF16)],
        compiler_params=_params(("parallel", "arbitrary"), 52),
    )(x, g.reshape(1, d), w_in_pad, w_in_pad, w_out_pad)


def _merge_kernel(xn_ref, ya_ref, yb_ref, yc_ref, yd_ref, wg_ref, bg_ref, wb_ref, o_ref):
    xn = xn_ref[...]
    acc = None
    for n, y_ref in enumerate((ya_ref, yb_ref, yc_ref, yd_ref)):
        gate = jnp.dot(xn, wg_ref[n], preferred_element_type=F32) + bg_ref[n]
        proj = jnp.dot(y_ref[...], wb_ref[n], preferred_element_type=F32)
        term = jax.nn.sigmoid(gate) * proj
        acc = term if acc is None else acc + term
    o_ref[...] = acc.astype(o_ref.dtype)


def gated_merge(xn, ys, w_gate, b_gate, w_branch, tm=1024, tn=256):
    m, d = xn.shape
    bw = ys[0].shape[1]
    y_spec = pl.BlockSpec((tm, bw), lambda i, j: (i, 0))
    return pl.pallas_call(
        _merge_kernel,
        grid=(m // tm, d // tn),
        in_specs=[pl.BlockSpec((tm, d), lambda i, j: (i, 0)), y_spec, y_spec, y_spec, y_spec,
                  pl.BlockSpec((N_BRANCH, d, tn), lambda i, j: (0, 0, j)),
                  pl.BlockSpec((N_BRANCH, 1, tn), lambda i, j: (0, 0, j)),
                  pl.BlockSpec((N_BRANCH, bw, tn), lambda i, j: (0, 0, j))],
        out_specs=pl.BlockSpec((tm, tn), lambda i, j: (i, j)),
        out_shape=jax.ShapeDtypeStruct((m, d), BF16),
        compiler_params=_params(("parallel", "arbitrary"), 48),
    )(xn, *ys, w_gate, b_gate.reshape(N_BRANCH, 1, d), w_branch)


def _s5_kernel(u_ref, perm_ref, permt_ref, bbd_ref, pwr_ref, pwi_ref, cbd_ref, d_ref, wglu_ref, o_ref,
               x_ref, st_ref, *, tt):
    n = S5_GROUPS * S5_STATE
    sl = tt // 8
    lc = 512

    @pl.when(pl.program_id(1) == 0)
    def _():
        st_ref[...] = jnp.zeros_like(st_ref)

    u = u_ref[...]
    u_perm = jnp.dot(perm_ref[...], u.astype(BF16), preferred_element_type=F32).astype(BF16)
    x_ref[...] = jnp.dot(u_perm, bbd_ref[...], preferred_element_type=F32)

    for c0 in range(0, n, lc):
        re = slice(c0, c0 + lc)
        im = slice(n + c0, n + c0 + lc)
        ar = pwr_ref[0:1, re]
        ai = pwi_ref[0:1, re]
        xr = jnp.zeros((8, lc), F32)
        xi = jnp.zeros((8, lc), F32)
        for j in range(sl):
            rows = slice(8 * j, 8 * j + 8)
            nr = ar * xr - ai * xi + x_ref[rows, re]
            ni = ar * xi + ai * xr + x_ref[rows, im]
            xr, xi = nr, ni
            x_ref[rows, re] = xr
            x_ref[rows, im] = xi

        alr = pwr_ref[sl - 1:sl, re]
        ali = pwi_ref[sl - 1:sl, re]
        cr = [st_ref[0:1, re]]
        ci = [st_ref[1:2, re]]
        for s in range(1, 9):
            pr, pi = cr[-1], ci[-1]
            cr.append(xr[s - 1:s] + alr * pr - ali * pi)
            ci.append(xi[s - 1:s] + alr * pi + ali * pr)
        st_ref[0:1, re] = cr[8]
        st_ref[1:2, re] = ci[8]
        car = jnp.concatenate(cr[:8], axis=0)
        cai = jnp.concatenate(ci[:8], axis=0)
        for j in range(sl):
            rows = slice(8 * j, 8 * j + 8)
            pr = pwr_ref[j:j + 1, re]
            pi = pwi_ref[j:j + 1, re]
            x_ref[rows, re] = x_ref[rows, re] + (pr * car - pi * cai)
            x_ref[rows, im] = x_ref[rows, im] + (pr * cai + pi * car)

    y_perm = jnp.dot(x_ref[...].astype(BF16), cbd_ref[...], preferred_element_type=F32)
    y = jnp.dot(permt_ref[...], y_perm, precision=lax.Precision.HIGHEST, preferred_element_type=F32) + d_ref[...] * u
    z = jax.nn.gelu(y)
    gate = jnp.dot(z.astype(BF16), wglu_ref[...], preferred_element_type=F32)
    o_ref[...] = (z * jax.nn.sigmoid(gate)).astype(o_ref.dtype)


S5_TILE = 256


def s5_branch(h_f32, col_block, bsz, seq, bbd, pw_re, pw_im, cbd, d_skip, w_glu, tt=S5_TILE):
    n = S5_GROUPS * S5_STATE
    nt = seq // tt
    sl = tt // 8
    row = jnp.arange(tt)
    perm = jax.nn.one_hot((row % 8) * sl + row // 8, tt, dtype=F32)
    const = lambda shape: pl.BlockSpec(shape, lambda b, t: (0, 0))
    return pl.pallas_call(
        functools.partial(_s5_kernel, tt=tt),
        grid=(bsz, nt),
        in_specs=[pl.BlockSpec((tt, S5_WIDTH), lambda b, t: (b * nt + t, col_block)),
                  const((tt, tt)), const((tt, tt)),
                  const((S5_WIDTH, 2 * n)), const((sl, n)), const((sl, n)), const((2 * n, S5_WIDTH)),
                  const((1, S5_WIDTH)), const((S5_WIDTH, S5_WIDTH))],
        out_specs=pl.BlockSpec((tt, S5_WIDTH), lambda b, t: (b * nt + t, 0)),
        out_shape=jax.ShapeDtypeStruct((bsz * seq, S5_WIDTH), BF16),
        scratch_shapes=[pltpu.VMEM((tt, 2 * n), F32), pltpu.VMEM((8, n), F32)],
        compiler_params=_params(("arbitrary", "arbitrary"), 48),
    )(h_f32, perm.astype(BF16), perm.T, bbd, pw_re, pw_im, cbd, d_skip, w_glu)


def s5_tables(a_re, a_im, log_dt, b_re, b_im, c_re, c_im, n_pow=S5_TILE // 8):
    g, p = a_re.shape
    dt = jnp.exp(log_dt)[:, None]
    mag = jnp.exp(a_re * dt)
    ab_re, ab_im = mag * jnp.cos(a_im * dt), mag * jnp.sin(a_im * dt)
    steps = jnp.arange(1, n_pow + 1, dtype=F32)[:, None, None]
    pw_mag = jnp.exp(a_re * dt * steps)
    pw_re = (pw_mag * jnp.cos(a_im * dt * steps)).reshape(n_pow, g * p)
    pw_im = (pw_mag * jnp.sin(a_im * dt * steps)).reshape(n_pow, g * p)
    den = a_re * a_re + a_im * a_im
    nr, ni = ab_re - 1.0, ab_im
    coef_re = (nr * a_re + ni * a_im) / den
    coef_im = (ni * a_re - nr * a_im) / den
    bb_re = coef_re[..., None] * b_re - coef_im[..., None] * b_im
    bb_im = coef_re[..., None] * b_im + coef_im[..., None] * b_re
    eye = jnp.eye(g, dtype=F32)
    to_bd_in = lambda t: jnp.einsum('gpc,gh->gchp', t, eye).reshape(g * S5_GROUP, g * p)
    to_bd_out = lambda t: jnp.einsum('gcp,gh->gphc', t, eye).reshape(g * p, g * S5_GROUP)
    bbd = jnp.concatenate([to_bd_in(bb_re), to_bd_in(bb_im)], axis=1).astype(BF16)
    cbd = jnp.concatenate([to_bd_out(c_re), to_bd_out(-c_im)], axis=0).astype(BF16)
    return bbd, pw_re, pw_im, cbd


def _log_sigmoid(x):
    return jnp.minimum(x, 0.0) - jnp.log1p(jnp.exp(-jnp.abs(x)))


def _mlstm_kernel(qk_ref, v_ref, og_ref, gate_ref, gbias_ref, convw_ref, o_ref,
                  xbuf_ref, c_ref, n_ref, m_ref, *, tt):
    L = MLSTM_L
    hd = MLSTM_DIM
    nh = MLSTM_HEADS
    i_col, f_col = 72, 76

    @pl.when(pl.program_id(1) == 0)
    def _():
        xbuf_ref[0:8, :] = jnp.zeros((8, 2 * nh * hd), F32)
        c_ref[...] = jnp.zeros_like(c_ref)
        n_ref[...] = jnp.zeros_like(n_ref)
        m_ref[...] = jnp.zeros_like(m_ref)

    xbuf_ref[8:8 + tt, :] = qk_ref[...]
    cw = convw_ref[...]
    conv = xbuf_ref[8:8 + tt, :] * cw[CONV_WIDTH - 1:CONV_WIDTH]
    for sh in range(1, CONV_WIDTH):
        conv = conv + xbuf_ref[8 - sh:8 - sh + tt, :] * cw[CONV_WIDTH - 1 - sh:CONV_WIDTH - sh]
    xbuf_ref[0:8, :] = xbuf_ref[tt:tt + 8, :]
    qk = conv * jax.nn.sigmoid(conv)

    gates = gate_ref[...] + gbias_ref[...]
    logf = _log_sigmoid(gates)
    row = lax.broadcasted_iota(jnp.int32, (L, L), 0)
    col = lax.broadcasted_iota(jnp.int32, (L, L), 1)
    causal = col <= row
    tri = jnp.where(causal, 1.0, 0.0).astype(F32)

    for c in range(tt // L):
        r0 = c * L
        gc = gates[r0:r0 + L]
        bcum = jnp.dot(tri, logf[r0:r0 + L], precision=lax.Precision.HIGHEST, preferred_element_type=F32)
        gct = gc.T
        bcumt = bcum.T
        for h in range(nh):
            ig_col = gc[:, i_col + h:i_col + h + 1]
            b_col = bcum[:, f_col + h:f_col + h + 1]
            ig_row = gct[i_col + h:i_col + h + 1, :]
            b_row = bcumt[f_col + h:f_col + h + 1, :]
            b_last = b_col[L - 1:L, :]
            q = qk[r0:r0 + L, h * hd:(h + 1) * hd]
            k = qk[r0:r0 + L, (nh + h) * hd:(nh + h + 1) * hd] * (hd ** -0.5)
            vb = v_ref[r0:r0 + L, h * hd:(h + 1) * hd].astype(BF16)
            qb = q.astype(BF16)
            kb = k.astype(BF16)
            c_prev = c_ref[h]
            n_prev = n_ref[h]
            m_prev = m_ref[h][:, 0:1]

            g = b_last - b_col + ig_col
            m_loc = jnp.max(g, axis=0, keepdims=True)
            wk = jnp.exp(g - m_loc) * k
            c_loc = jnp.dot(wk.T.astype(BF16), vb, preferred_element_type=F32)
            n_loc = jnp.sum(wk, axis=0, keepdims=True)

            m_inter = b_col + m_prev
            log_d = jnp.where(causal, b_col - b_row + ig_row, -jnp.inf)
            m_j = jnp.maximum(jnp.max(log_d, axis=1, keepdims=True), m_inter)
            dqk = jnp.exp(log_d - m_j) * lax.dot_general(qb, kb, NT_DIMS, preferred_element_type=F32)
            inter = jnp.exp(m_inter - m_j)
            num = (jnp.dot(dqk.astype(BF16), vb, preferred_element_type=F32)
                   + inter * jnp.dot(qb, c_prev.astype(BF16), preferred_element_type=F32))
            den = jnp.sum(dqk, axis=1, keepdims=True) + inter * jnp.sum(q * n_prev, axis=1, keepdims=True)
            hid = num / jnp.maximum(jnp.abs(den), jnp.exp(-m_j))
            og = jax.nn.sigmoid(og_ref[r0:r0 + L, h * hd:(h + 1) * hd])
            o_ref[r0:r0 + L, h * hd:(h + 1) * hd] = (og * hid).astype(o_ref.dtype)

            m_new = jnp.maximum(b_last + m_prev, m_loc)
            a = jnp.exp(b_last + m_prev - m_new)
            bcoef = jnp.exp(m_loc - m_new)
            c_ref[h] = a * c_prev + bcoef * c_loc
            n_ref[h] = a * n_prev + bcoef * n_loc
            m_ref[h] = jnp.broadcast_to(m_new, (1, LANES))


def mlstm_branch(h_f32, misc, gate_bias, conv_w, bsz, seq, qk_blk, v_blk, o_blk, tt=512):
    nh, hd = MLSTM_HEADS, MLSTM_DIM
    w = nh * hd
    nt = seq // tt
    return pl.pallas_call(
        functools.partial(_mlstm_kernel, tt=tt),
        grid=(bsz, nt),
        in_specs=[pl.BlockSpec((tt, 2 * w), lambda b, t: (b * nt + t, qk_blk)),
                  pl.BlockSpec((tt, w), lambda b, t: (b * nt + t, v_blk)),
                  pl.BlockSpec((tt, w), lambda b, t: (b * nt + t, o_blk)),
                  pl.BlockSpec((tt, LANES), lambda b, t: (b * nt + t, 0)),
                  pl.BlockSpec((1, LANES), lambda b, t: (0, 0)),
                  pl.BlockSpec((CONV_WIDTH, 2 * w), lambda b, t: (0, 0))],
        out_specs=pl.BlockSpec((tt, w), lambda b, t: (b * nt + t, 0)),
        out_shape=jax.ShapeDtypeStruct((bsz * seq, w), BF16),
        scratch_shapes=[pltpu.VMEM((tt + 8, 2 * w), F32), pltpu.VMEM((nh, hd, hd), F32),
                        pltpu.VMEM((nh, 1, hd), F32), pltpu.VMEM((nh, 1, LANES), F32)],
        compiler_params=_params(("arbitrary", "arbitrary"), 48),
    )(h_f32, h_f32, h_f32, misc, gate_bias, conv_w)


def rel_bucket(dist):
    max_exact = REL_BUCKETS // 2
    n = jnp.maximum(dist, 0)
    large = max_exact + (jnp.log(jnp.maximum(n, 1).astype(F32) / max_exact)
                         / math.log(REL_MAX_DIST / max_exact)
                         * (REL_BUCKETS - max_exact)).astype(jnp.int32)
    large = jnp.minimum(large, REL_BUCKETS - 1)
    return jnp.where(n < max_exact, n, large)


def bias_tiles(rel, t):
    assert t >= REL_MAX_DIST
    kk = jnp.arange(t)[:, None]
    qq = jnp.arange(t)[None, :]

    def lookup(dist):
        onehot = (rel_bucket(dist)[..., None] == jnp.arange(REL_BUCKETS)).astype(F32)
        return jnp.einsum('kqb,bh->hkq', onehot, rel.astype(F32), precision=lax.Precision.HIGHEST) * LOG2E

    t0 = jnp.where((qq >= kk)[None], lookup(qq - kk), NEG)
    t1 = lookup(t + qq - kk)
    t2 = lookup(jnp.full((t, t), 2 * t))
    return jnp.stack([t0, t1, t2])


def tri_pairs(nq):
    qi = [q for q in range(nq) for _ in range(q + 1)]
    kj = [k for q in range(nq) for k in range(q + 1)]
    return jnp.asarray(qi, jnp.int32), jnp.asarray(kj, jnp.int32)


def values_t_ext(h_bf16, col0, bsz, seq, heads, dv):
    v = h_bf16[:, col0:col0 + heads * dv].reshape(bsz, seq, heads, dv)
    vt = jnp.transpose(v, (0, 2, 3, 1))
    ones = jnp.ones((bsz, heads, V_ONES, seq), BF16)
    return jnp.concatenate([vt, ones], axis=2).reshape(bsz * heads * (dv + V_ONES), seq)


def _attn_maps(qi, kj, maps, bias_ref, mb_ref, s_ref, p_ref, m_ref, acc_ref):
    def run(const_bias):
        def qk(i):
            kh, qz, _, _ = maps[i]
            s_ref[i % 2] = lax.dot_general(kh, qz, NT_DIMS, preferred_element_type=F32)

        qk(0)
        for i, (_, _, vt_ext, head) in enumerate(maps):
            if i + 1 < len(maps):
                qk(i + 1)
            _softmax_pv(i, vt_ext, bias_ref, head, mb_ref, const_bias, s_ref, p_ref, m_ref, acc_ref)

    far = qi - kj >= 2

    @pl.when(far)
    def _():
        run(True)

    @pl.when(jnp.logical_not(far))
    def _():
        run(False)


def _softmax_pv(idx, vt_ext, bias_ref, head, mb_ref, const_bias, s_ref, p_ref, m_ref, acc_ref):
    tk, tq = s_ref.shape[1:]
    sb = s_ref.at[idx % 2]
    pb = p_ref.at[idx % 2]

    def logits(r0, n):
        t = sb[r0:r0 + n, :]
        if not const_bias:
            t = t + bias_ref[0, head, r0:r0 + n, :]
        if mb_ref is not None:
            t = t + mb_ref[r0:r0 + n, :]
        return t

    mx = [jnp.full((8, tq), -jnp.inf, F32) for _ in range(4)]
    for c in range(tk // 8):
        mx[c % 4] = jnp.maximum(mx[c % 4], logits(c * 8, 8))
    m_tile = jnp.max(jnp.maximum(jnp.maximum(mx[0], mx[1]), jnp.maximum(mx[2], mx[3])), axis=0, keepdims=True)
    m_prev = m_ref[idx]
    if const_bias:
        bias_c = bias_ref[0, head, 0:1, :]
        m_new = jnp.maximum(m_prev, m_tile + bias_c)
        m_b = jnp.broadcast_to(m_new - bias_c, (16, tq))
    else:
        m_new = jnp.maximum(m_prev, m_tile)
        m_b = jnp.broadcast_to(m_new, (16, tq))
    m_ref[idx] = m_new
    for c in range(tk // 16):
        pb[c * 16:(c + 1) * 16, :] = jnp.exp2(logits(c * 16, 16) - m_b).astype(BF16)
    alpha = jnp.exp2(m_prev - m_new)
    acc_ref[idx] = alpha * acc_ref[idx] + jnp.dot(vt_ext, pb[...], preferred_element_type=F32)


def _init_softmax_state(m_ref, acc_ref):
    m_ref[...] = jnp.full(m_ref.shape, NEG, F32)
    acc_ref[...] = jnp.zeros_like(acc_ref)


def _diff_attn_kernel(qi_ref, kj_ref, q_ref, k_ref, vt_ref, bias_ref, lam_ref, gb_ref, o_ref,
                      s_ref, p_ref, m_ref, acc_ref, *, out_scale):
    p = pl.program_id(1)
    qi = qi_ref[p]
    kj = kj_ref[p]
    dq = DIFF_QK_DIM
    dv = DIFF_V_DIM
    vrows = dv + V_ONES

    @pl.when(kj == 0)
    def _():
        _init_softmax_state(m_ref, acc_ref)

    lane = lax.broadcasted_iota(jnp.int32, (q_ref.shape[0], 2 * dq), 1)
    maps = []
    for h in range(DIFF_HEADS):
        qh = q_ref[:, h * 2 * dq:(h + 1) * 2 * dq]
        kh = k_ref[:, h * 2 * dq:(h + 1) * 2 * dq]
        vt = vt_ref[h * vrows:(h + 1) * vrows, :]
        for mi in range(2):
            qz = jnp.where((lane < dq) if mi == 0 else (lane >= dq), qh, jnp.zeros_like(qh))
            maps.append((kh, qz, vt, h))
    _attn_maps(qi, kj, maps, bias_ref, None, s_ref, p_ref, m_ref, acc_ref)

    @pl.when(kj == qi)
    def _():
        lam = lam_ref[0:1, 0:1]
        for h in range(DIFF_HEADS):
            a1 = acc_ref[2 * h]
            a2 = acc_ref[2 * h + 1]
            a = a1[0:dv] / a1[dv:dv + 1] - lam * (a2[0:dv] / a2[dv:dv + 1])
            y = a * lax.rsqrt(jnp.mean(a * a, axis=0, keepdims=True) + EPS) * gb_ref[...] * out_scale
            o_ref[:, h * dv:(h + 1) * dv] = y.T.astype(o_ref.dtype)


def diff_attention(h_bf16, vt_ext, bsz, seq, q_blk, k_blk, bias, lam_row, subln_g, out_scale, t=512):
    nq = seq // t
    qi, kj = tri_pairs(nq)
    w = DIFF_HEADS * DIFF_V_DIM
    nmap = 2 * DIFF_HEADS
    vrows = DIFF_HEADS * (DIFF_V_DIM + V_ONES)
    g_b = jnp.broadcast_to(subln_g.reshape(DIFF_V_DIM, 1), (DIFF_V_DIM, t))
    grid_spec = pltpu.PrefetchScalarGridSpec(
        num_scalar_prefetch=2,
        grid=(bsz, qi.shape[0]),
        in_specs=[pl.BlockSpec((t, w), lambda b, p, qi, kj: (b * nq + qi[p], q_blk)),
                  pl.BlockSpec((t, w), lambda b, p, qi, kj: (b * nq + kj[p], k_blk)),
                  pl.BlockSpec((vrows, t), lambda b, p, qi, kj: (b, kj[p])),
                  pl.BlockSpec((1, DIFF_HEADS, t, t),
                               lambda b, p, qi, kj: (jnp.minimum(qi[p] - kj[p], 2), 0, 0, 0)),
                  pl.BlockSpec((1, LANES), lambda b, p, qi, kj: (0, 0)),
                  pl.BlockSpec((DIFF_V_DIM, t), lambda b, p, qi, kj: (0, 0))],
        out_specs=pl.BlockSpec((t, w), lambda b, p, qi, kj: (b * nq + qi[p], 0)),
        scratch_shapes=[pltpu.VMEM((2, t, t), F32), pltpu.VMEM((2, t, t), BF16),
                        pltpu.VMEM((nmap, 1, t), F32), pltpu.VMEM((nmap, DIFF_V_DIM + V_ONES, t), F32)],
    )
    return pl.pallas_call(
        functools.partial(_diff_attn_kernel, out_scale=out_scale),
        grid_spec=grid_spec,
        out_shape=jax.ShapeDtypeStruct((bsz * seq, w), BF16),
        compiler_params=_params(("arbitrary", "arbitrary"), 48),
    )(qi, kj, h_bf16, h_bf16, vt_ext, bias, lam_row, g_b)


def _sort_key(x):
    b = lax.bitcast_convert_type(x, jnp.int32)
    return b ^ ((b >> 31) & 0x7FFFFFFF)


def _dsa_select_kernel(iq_ref, w_ref, ikz_ref, o_ref, key_ref, a_ref, wb_ref, *, tq, tk, topk, seq):
    i = pl.program_id(1)
    last = (i * tq + tq - 1) // tk
    idx_scale = (IDX_DIM * IDX_HEADS) ** -0.5
    iw_row = 64

    wt = w_ref[...].T
    for h in range(IDX_HEADS):
        wb_ref[h] = jnp.broadcast_to(wt[iw_row + h:iw_row + h + 1, :], (8, tq))

    def in_prefix(off, n):
        key_pos = off + lax.broadcasted_iota(jnp.int32, (n, tq), 0)
        q_pos = i * tq + lax.broadcasted_iota(jnp.int32, (n, tq), 1)
        return key_pos <= q_pos

    def score_tile(j, diagonal):
        off = pl.multiple_of(j * tk, tk)
        for pair in range(IDX_HEADS // 2):
            qp = iq_ref[:, pair * LANES:(pair + 1) * LANES]
            for half in range(2):
                a_ref[2 * pair + half] = lax.dot_general(
                    ikz_ref[pl.ds(off, tk), half * LANES:(half + 1) * LANES], qp, NT_DIMS,
                    preferred_element_type=F32)
        for c in range(tk // 8):
            r0 = c * 8
            sc = wb_ref[0] * jnp.maximum(a_ref[0, r0:r0 + 8, :], 0.0)
            for h in range(1, IDX_HEADS):
                sc = sc + wb_ref[h] * jnp.maximum(a_ref[h, r0:r0 + 8, :], 0.0)
            sc = sc * idx_scale
            if diagonal:
                sc = jnp.where(in_prefix(off + r0, 8), sc, -jnp.inf)
            key_ref[pl.ds(pl.multiple_of(off + r0, 8), 8), :] = _sort_key(sc)

    def score_body(j, carry):
        score_tile(j, False)
        return carry

    lax.fori_loop(0, last, score_body, 0)
    score_tile(last, True)

    n_acc = 8

    def bit_body(it, thr):
        cand = thr + lax.shift_left(jnp.int32(1), 31 - it)
        cand_b = jnp.broadcast_to(cand, (8, tq))

        def count_body(j, accs):
            blk = key_ref[pl.ds(pl.multiple_of(j * tk, tk), tk), :]
            accs = list(accs)
            for c in range(tk // 8):
                accs[c % n_acc] = accs[c % n_acc] + jnp.where(blk[c * 8:(c + 1) * 8, :] >= cand_b, 1.0, 0.0)
            return tuple(accs)

        zero = jnp.zeros((8, tq), F32)
        accs = list(lax.fori_loop(0, last + 1, count_body, (zero,) * n_acc))
        while len(accs) > 1:
            accs = [a + b for a, b in zip(accs[0::2], accs[1::2])]
        cnt = jnp.sum(accs[0], axis=0, keepdims=True)
        return jnp.where(cnt >= topk, cand, thr)

    thr = lax.fori_loop(0, 32, bit_body, jnp.full((1, tq), INT_MIN, jnp.int32))
    thr_b = jnp.broadcast_to(thr, (16, tq))

    def out_tile(j, diagonal):
        off = pl.multiple_of(j * tk, tk)
        for c in range(tk // 16):
            rows = pl.ds(pl.multiple_of(off + c * 16, 16), 16)
            keep = key_ref[rows, :] >= thr_b
            if diagonal:
                keep = keep & in_prefix(off + c * 16, 16)
            o_ref[0, rows, :] = jnp.where(keep, 0.0, NEG).astype(o_ref.dtype)

    def out_body(j, carry):
        out_tile(j, False)
        return carry

    lax.fori_loop(0, last, out_body, 0)
    out_tile(last, True)

    def fill_body(j, carry):
        o_ref[0, pl.ds(pl.multiple_of(j * tk, tk), tk), :] = jnp.full((tk, tq), NEG, o_ref.dtype)
        return carry

    lax.fori_loop(last + 1, seq // tk, fill_body, 0)


def dsa_select(h_bf16, misc, ikz, bsz, seq, iq_blk, tq=256, tk=512):
    nq = seq // tq
    topk = min(TOPK_MAX, seq // 4)
    w = IDX_HEADS * IDX_DIM
    return pl.pallas_call(
        functools.partial(_dsa_select_kernel, tq=tq, tk=tk, topk=topk, seq=seq),
        grid=(bsz, nq),
        in_specs=[pl.BlockSpec((tq, w), lambda b, i: (b * nq + i, iq_blk)),
                  pl.BlockSpec((tq, LANES), lambda b, i: (b * nq + i, 0)),
                  pl.BlockSpec((seq, 2 * LANES), lambda b, i: (b, 0), pipeline_mode=pl.Buffered(1))],
        out_specs=pl.BlockSpec((1, seq, tq), lambda b, i: (b, 0, i)),
        out_shape=jax.ShapeDtypeStruct((bsz, seq, seq), BF16),
        scratch_shapes=[pltpu.VMEM((seq, tq), jnp.int32), pltpu.VMEM((IDX_HEADS, tk, tq), F32),
                        pltpu.VMEM((IDX_HEADS, 8, tq), F32)],
        compiler_params=_params(("arbitrary", "arbitrary"), 56),
    )(h_bf16, misc, ikz)


def _dsa_attn_kernel(qi_ref, kj_ref, q_ref, k_ref, vt_ref, bias_ref, mask_ref, o_ref,
                     s_ref, p_ref, mb_ref, m_ref, acc_ref):
    p = pl.program_id(1)
    qi = qi_ref[p]
    kj = kj_ref[p]
    dh = HEAD_DIM
    vrows = dh + V_ONES

    @pl.when(kj == 0)
    def _():
        _init_softmax_state(m_ref, acc_ref)

    mb_ref[...] = mask_ref[0].astype(F32)
    maps = [(k_ref[:, h * dh:(h + 1) * dh], q_ref[:, h * dh:(h + 1) * dh], vt_ref[h * vrows:(h + 1) * vrows, :], h)
            for h in range(DSA_HEADS)]
    _attn_maps(qi, kj, maps, bias_ref, mb_ref, s_ref, p_ref, m_ref, acc_ref)

    @pl.when(kj == qi)
    def _():
        for h in range(DSA_HEADS):
            a = acc_ref[h]
            o_ref[:, h * dh:(h + 1) * dh] = (a[0:dh] / a[dh:dh + 1]).T.astype(o_ref.dtype)


def dsa_attention(h_bf16, vt_ext, mask, bsz, seq, q_blk, k_blk, bias, t=512):
    nq = seq // t
    qi, kj = tri_pairs(nq)
    w = DSA_HEADS * HEAD_DIM
    vrows = DSA_HEADS * (HEAD_DIM + V_ONES)
    grid_spec = pltpu.PrefetchScalarGridSpec(
        num_scalar_prefetch=2,
        grid=(bsz, qi.shape[0]),
        in_specs=[pl.BlockSpec((t, w), lambda b, p, qi, kj: (b * nq + qi[p], q_blk)),
                  pl.BlockSpec((t, w), lambda b, p, qi, kj: (b * nq + kj[p], k_blk)),
                  pl.BlockSpec((vrows, t), lambda b, p, qi, kj: (b, kj[p])),
                  pl.BlockSpec((1, DSA_HEADS, t, t),
                               lambda b, p, qi, kj: (jnp.minimum(qi[p] - kj[p], 2), 0, 0, 0)),
                  pl.BlockSpec((1, t, t), lambda b, p, qi, kj: (b, kj[p], qi[p]))],
        out_specs=pl.BlockSpec((t, w), lambda b, p, qi, kj: (b * nq + qi[p], 0)),
        scratch_shapes=[pltpu.VMEM((2, t, t), F32), pltpu.VMEM((2, t, t), BF16), pltpu.VMEM((t, t), F32),
                        pltpu.VMEM((DSA_HEADS, 1, t), F32), pltpu.VMEM((DSA_HEADS, HEAD_DIM + V_ONES, t), F32)],
    )
    return pl.pallas_call(
        _dsa_attn_kernel,
        grid_spec=grid_spec,
        out_shape=jax.ShapeDtypeStruct((bsz * seq, w), BF16),
        compiler_params=_params(("arbitrary", "arbitrary"), 48),
    )(qi, kj, h_bf16, h_bf16, vt_ext, bias, mask)


_SPLITS = (512, 512, 512, 512, 64, 8, 512, 512, 512, 512, 1024, 512, 512, 4, 4)
_NAMES = ('a_q', 'a_k', 'a_v', 'a_iq', 'a_ik', 'a_iw', 'b_u', 'c_q', 'c_k', 'c_v', 'd_qk', 'd_v', 'd_o', 'd_i', 'd_f')


def _split_w_in(w_in):
    out, off = {}, 0
    for name, width in zip(_NAMES, _SPLITS):
        out[name] = w_in[:, off:off + width]
        off += width
    return out


def _group_w_in(w_in):
    c = _split_w_in(w_in)
    d = w_in.shape[0]
    z = lambda n: jnp.zeros((d, n), w_in.dtype)
    w_attn = jnp.concatenate([c['a_q'], c['a_k'], c['a_v'], c['a_iq'], c['c_q'], c['c_k'], c['c_v']], axis=1)
    w_scan = jnp.concatenate([c['d_qk'], c['b_u'], c['d_v'], c['d_o']], axis=1)
    w_misc = jnp.concatenate([c['a_ik'], z(64), z(64), c['a_ik'],
                              z(64), c['a_iw'], c['d_i'], c['d_f'], z(48)], axis=1)
    return w_attn.astype(BF16), w_scan.astype(BF16), w_misc.astype(BF16)


def _pad_ffn(w_in, w_out):
    d = w_in.shape[0]
    pad = D_FF_PAD - D_FF
    zi = jnp.zeros((d, pad), w_in.dtype)
    w_in_pad = jnp.concatenate([w_in[:, :D_FF], zi, w_in[:, D_FF:], zi], axis=1).astype(BF16)
    w_out_pad = jnp.concatenate([w_out, jnp.zeros((pad, d), w_out.dtype)], axis=0).astype(BF16)
    return w_in_pad, w_out_pad


def _branches(xn, l, bsz, seq, w_in, conv_w, i_bias, f_bias, s5, s5_d, s5_w_glu, diff_lambda,
              diff_subln_g, bias_a, bias_c):
    lambda_init = 0.8 - 0.6 * math.exp(-0.3 * l)
    w_attn, w_scan, w_misc = _group_w_in(w_in)
    ones = jnp.ones((512,), F32)
    attn_scale = jnp.concatenate([ones * (HEAD_DIM ** -0.5 * LOG2E), ones, ones, ones,
                                  ones * (DIFF_QK_DIM ** -0.5 * LOG2E), ones, ones]).reshape(1, -1)
    h_attn = matmul(xn, w_attn, attn_scale, BF16)
    h_scan = matmul(xn, w_scan, jnp.ones((1, w_scan.shape[1]), F32), F32)
    ikz, misc = misc_proj(xn, w_misc)

    mask = dsa_select(h_attn, misc, ikz, bsz, seq, iq_blk=3)
    vt_a = values_t_ext(h_attn, 2 * 512, bsz, seq, DSA_HEADS, HEAD_DIM)
    y_a = dsa_attention(h_attn, vt_a, mask, bsz, seq, 0, 1, bias_a)

    bbd, abar_re, abar_im, cbd = s5
    y_b = s5_branch(h_scan, 2, bsz, seq, bbd, abar_re, abar_im, cbd, s5_d.reshape(1, -1), s5_w_glu.astype(BF16))

    dl = diff_lambda.astype(F32)
    lam = jnp.exp(jnp.sum(dl[0] * dl[1])) - jnp.exp(jnp.sum(dl[2] * dl[3])) + lambda_init
    lam_row = jnp.full((1, LANES), lam, F32)
    vt_c = values_t_ext(h_attn, 6 * 512, bsz, seq, DIFF_HEADS, DIFF_V_DIM)
    y_c = diff_attention(h_attn, vt_c, bsz, seq, 4, 5, bias_c, lam_row, diff_subln_g, 1.0 - lambda_init)

    gate_bias = jnp.zeros((1, LANES), F32).at[0, 72:76].set(i_bias).at[0, 76:80].set(f_bias)
    y_d = mlstm_branch(h_scan, misc, gate_bias, conv_w, bsz, seq, qk_blk=0, v_blk=3, o_blk=4)
    return y_a, y_b, y_c, y_d


def _mixer(x, l, bsz, seq, norm_g, w_in, conv_w, i_bias, f_bias, s5, s5_d, s5_w_glu, diff_lambda,
           diff_subln_g, bias_a, bias_c, w_gate, b_gate, w_branch, w_out):
    xn = rmsnorm(x, norm_g, BF16)
    ys = _branches(xn, l, bsz, seq, w_in, conv_w, i_bias, f_bias, s5, s5_d, s5_w_glu, diff_lambda,
                   diff_subln_g, bias_a, bias_c)
    merged = gated_merge(xn, ys, w_gate.astype(BF16), b_gate, w_branch.astype(BF16))
    return matmul_residual(merged, w_out.astype(BF16), x)


def kernel(x, norm_g, w_ffn_in, w_ffn_out, w_in, conv_w, mlstm_i_bias, mlstm_f_bias, s5_a_re, s5_a_im,
           s5_log_dt, s5_b_re, s5_b_im, s5_c_re, s5_c_im, s5_d, s5_w_glu, diff_lambda, diff_subln_g,
           rel_table, w_gate, b_gate, w_branch, w_out, final_g):
    bsz, seq, d = x.shape
    t_attn = 512
    bias_a = bias_tiles(rel_table[:, :DSA_HEADS], t_attn)
    bias_c = bias_tiles(rel_table[:, DSA_HEADS:], t_attn)
    xf = x.reshape(bsz * seq, d)
    for l in range(DEPTH):
        xf = ffn_block(xf, norm_g[l, 0], *_pad_ffn(w_ffn_in[l, 0], w_ffn_out[l, 0]))
        s5 = s5_tables(s5_a_re[l], s5_a_im[l], s5_log_dt[l], s5_b_re[l], s5_b_im[l], s5_c_re[l], s5_c_im[l])
        xf = _mixer(xf, l, bsz, seq, norm_g[l, 1], w_in[l], conv_w[l], mlstm_i_bias[l], mlstm_f_bias[l],
                    s5, s5_d[l], s5_w_glu[l], diff_lambda[l], diff_subln_g[l], bias_a, bias_c,
                    w_gate[l], b_gate[l], w_branch[l], w_out[l])
        xf = ffn_block(xf, norm_g[l, 2], *_pad_ffn(w_ffn_in[l, 1], w_ffn_out[l, 1]))
    return rmsnorm(xf, final_g, F32).reshape(bsz, seq, d)
```

```python
import functools
import math

import jax
import jax.numpy as jnp
from jax import lax
from jax.experimental import pallas as pl
from jax.experimental.pallas import tpu as pltpu

F32 = jnp.float32
BF16 = jnp.bfloat16

D_MODEL = 2048
DEPTH = 4
N_BRANCH = 4
BRANCH_WIDTH = 512
HEAD_DIM = 128
DSA_HEADS = 4
IDX_HEADS = 8
IDX_DIM = 64
TOPK_MAX = 256
S5_WIDTH = 512
S5_GROUP = 16
S5_GROUPS = S5_WIDTH // S5_GROUP
S5_STATE = 64
DIFF_HEADS = 4
DIFF_QK_DIM = 64
DIFF_V_DIM = 128
MLSTM_HEADS = 4
MLSTM_DIM = 128
CONV_WIDTH = 4
D_FF = 5504
REL_BUCKETS = 32
REL_MAX_DIST = 128
EPS = 1e-6

LANES = 128
NEG = -1e30
INT_MIN = -(2 ** 31)
D_FF_PAD = 5632
MLSTM_L = 128

V_ONES = 16
LOG2E = math.log2(math.e)

NT_DIMS = (((1,), (1,)), ((), ()))


def _params(sem, vmem_mb):
    return pltpu.CompilerParams(dimension_semantics=sem, vmem_limit_bytes=vmem_mb * 1024 * 1024)


def _rms(x, g):
    return x * lax.rsqrt(jnp.mean(x * x, axis=-1, keepdims=True) + EPS) * g


def _rmsnorm_kernel(x_ref, g_ref, o_ref):
    o_ref[...] = _rms(x_ref[...], g_ref[...]).astype(o_ref.dtype)


def rmsnorm(x, g, out_dtype, tm=512):
    m, d = x.shape
    return pl.pallas_call(
        _rmsnorm_kernel,
        grid=(m // tm,),
        in_specs=[pl.BlockSpec((tm, d), lambda i: (i, 0)), pl.BlockSpec((1, d), lambda i: (0, 0))],
        out_specs=pl.BlockSpec((tm, d), lambda i: (i, 0)),
        out_shape=jax.ShapeDtypeStruct((m, d), out_dtype),
        compiler_params=_params(("parallel",), 40),
    )(x, g.reshape(1, d))


def _mm_kernel(x_ref, w_ref, cs_ref, o_ref):
    acc = jnp.dot(x_ref[...], w_ref[...], preferred_element_type=F32)
    o_ref[...] = (acc * cs_ref[...]).astype(o_ref.dtype)


def matmul(x, w, col_scale, out_dtype, tm=1024, tn=512):
    m, k = x.shape
    n = w.shape[1]
    return pl.pallas_call(
        _mm_kernel,
        grid=(m // tm, n // tn),
        in_specs=[pl.BlockSpec((tm, k), lambda i, j: (i, 0)), pl.BlockSpec((k, tn), lambda i, j: (0, j)),
                  pl.BlockSpec((1, tn), lambda i, j: (0, j))],
        out_specs=pl.BlockSpec((tm, tn), lambda i, j: (i, j)),
        out_shape=jax.ShapeDtypeStruct((m, n), out_dtype),
        compiler_params=_params(("parallel", "arbitrary"), 40),
    )(x, w, col_scale)


def _mm_res_kernel(x_ref, w_ref, r_ref, o_ref):
    o_ref[...] = r_ref[...] + jnp.dot(x_ref[...], w_ref[...], preferred_element_type=F32)


def matmul_residual(x, w, r, tm=1024, tn=512):
    m, k = x.shape
    n = w.shape[1]
    return pl.pallas_call(
        _mm_res_kernel,
        grid=(m // tm, n // tn),
        in_specs=[pl.BlockSpec((tm, k), lambda i, j: (i, 0)), pl.BlockSpec((k, tn), lambda i, j: (0, j)),
                  pl.BlockSpec((tm, tn), lambda i, j: (i, j))],
        out_specs=pl.BlockSpec((tm, tn), lambda i, j: (i, j)),
        out_shape=jax.ShapeDtypeStruct((m, n), F32),
        compiler_params=_params(("parallel", "arbitrary"), 40),
    )(x, w, r)


def _misc_proj_kernel(x_ref, w_ref, ikz_ref, misc_ref):
    acc = jnp.dot(x_ref[...], w_ref[...], preferred_element_type=F32)
    ikz_ref[...] = acc[:, :2 * LANES].astype(BF16)
    misc_ref[...] = acc[:, 2 * LANES:]


def misc_proj(xn, w, tm=1024):
    m, k = xn.shape
    n = w.shape[1]
    return pl.pallas_call(
        _misc_proj_kernel,
        grid=(m // tm,),
        in_specs=[pl.BlockSpec((tm, k), lambda i: (i, 0)), pl.BlockSpec((k, n), lambda i: (0, 0))],
        out_specs=[pl.BlockSpec((tm, 2 * LANES), lambda i: (i, 0)), pl.BlockSpec((tm, LANES), lambda i: (i, 0))],
        out_shape=[jax.ShapeDtypeStruct((m, 2 * LANES), BF16), jax.ShapeDtypeStruct((m, LANES), F32)],
        compiler_params=_params(("parallel",), 40),
    )(xn, w)


def _ffn_kernel(x_ref, g_ref, wg_ref, wu_ref, wo_ref, o_ref, xn_ref):
    j = pl.program_id(1)

    @pl.when(j == 0)
    def _():
        x = x_ref[...]
        xn_ref[...] = _rms(x, g_ref[...]).astype(BF16)
        o_ref[...] = x

    xn = xn_ref[...]
    g = jnp.dot(xn, wg_ref[...], preferred_element_type=F32)
    u = jnp.dot(xn, wu_ref[...], preferred_element_type=F32)
    a = (g * jax.nn.sigmoid(g) * u * 0.5).astype(BF16)
    o_ref[...] += jnp.dot(a, wo_ref[...], preferred_element_type=F32)


def ffn_block(x, g, w_in_pad, w_out_pad, tm=1024, tf=512):
    m, d = x.shape
    nf = w_out_pad.shape[0] // tf
    return pl.pallas_call(
        _ffn_kernel,
        grid=(m // tm, nf),
        in_specs=[pl.BlockSpec((tm, d), lambda i, j: (i, 0)),
                  pl.BlockSpec((1, d), lambda i, j: (0, 0)),
                  pl.BlockSpec((d, tf), lambda i, j: (0, j)),
                  pl.BlockSpec((d, tf), lambda i, j: (0, j + nf)),
                  pl.BlockSpec((tf, d), lambda i, j: (j, 0))],
        out_specs=pl.BlockSpec((tm, d), lambda i, j: (i, 0)),
        out_shape=jax.ShapeDtypeStruct((m, d), F32),
        scratch_shapes=[pltpu.VMEM((tm, d), BF16)],
        compiler_params=_params(("parallel", "arbitrary"), 56),
    )(x, g.reshape(1, d), w_in_pad, w_in_pad, w_out_pad)


def _merge_kernel(xn_ref, ya_ref, yb_ref, yc_ref, yd_ref, wg_ref, bg_ref, wb_ref, o_ref):
    xn = xn_ref[...]
    acc = None
    for n, y_ref in enumerate((ya_ref, yb_ref, yc_ref, yd_ref)):
        gate = jnp.dot(xn, wg_ref[n], preferred_element_type=F32) + bg_ref[n]
        proj = jnp.dot(y_ref[...], wb_ref[n], preferred_element_type=F32)
        term = jax.nn.sigmoid(gate) * proj
        acc = term if acc is None else acc + term
    o_ref[...] = acc.astype(o_ref.dtype)


def gated_merge(xn, ys, w_gate, b_gate, w_branch, tm=1024, tn=256):
    m, d = xn.shape
    bw = ys[0].shape[1]
    y_spec = pl.BlockSpec((tm, bw), lambda i, j: (i, 0))
    return pl.pallas_call(
        _merge_kernel,
        grid=(m // tm, d // tn),
        in_specs=[pl.BlockSpec((tm, d), lambda i, j: (i, 0)), y_spec, y_spec, y_spec, y_spec,
                  pl.BlockSpec((N_BRANCH, d, tn), lambda i, j: (0, 0, j)),
                  pl.BlockSpec((N_BRANCH, 1, tn), lambda i, j: (0, 0, j)),
                  pl.BlockSpec((N_BRANCH, bw, tn), lambda i, j: (0, 0, j))],
        out_specs=pl.BlockSpec((tm, tn), lambda i, j: (i, j)),
        out_shape=jax.ShapeDtypeStruct((m, d), BF16),
        compiler_params=_params(("parallel", "arbitrary"), 48),
    )(xn, *ys, w_gate, b_gate.reshape(N_BRANCH, 1, d), w_branch)


def _s5_kernel(u_ref, perm_ref, permt_ref, bbd_ref, pwr_ref, pwi_ref, cbd_ref, d_ref, wglu_ref, o_ref,
               x_ref, st_ref, *, tt):
    n = S5_GROUPS * S5_STATE
    sl = tt // 8
    lc = 512

    @pl.when(pl.program_id(1) == 0)
    def _():
        st_ref[...] = jnp.zeros_like(st_ref)

    u = u_ref[...]
    u_perm = jnp.dot(perm_ref[...], u.astype(BF16), preferred_element_type=F32).astype(BF16)
    x_ref[...] = jnp.dot(u_perm, bbd_ref[...], preferred_element_type=F32)

    for c0 in range(0, n, lc):
        re = slice(c0, c0 + lc)
        im = slice(n + c0, n + c0 + lc)
        ar = pwr_ref[0:1, re]
        ai = pwi_ref[0:1, re]
        xr = jnp.zeros((8, lc), F32)
        xi = jnp.zeros((8, lc), F32)
        for j in range(sl):
            rows = slice(8 * j, 8 * j + 8)
            nr = ar * xr - ai * xi + x_ref[rows, re]
            ni = ar * xi + ai * xr + x_ref[rows, im]
            xr, xi = nr, ni
            x_ref[rows, re] = xr
            x_ref[rows, im] = xi

        alr = pwr_ref[sl - 1:sl, re]
        ali = pwi_ref[sl - 1:sl, re]
        cr = [st_ref[0:1, re]]
        ci = [st_ref[1:2, re]]
        for s in range(1, 9):
            pr, pi = cr[-1], ci[-1]
            cr.append(xr[s - 1:s] + alr * pr - ali * pi)
            ci.append(xi[s - 1:s] + alr * pi + ali * pr)
        st_ref[0:1, re] = cr[8]
        st_ref[1:2, re] = ci[8]
        car = jnp.concatenate(cr[:8], axis=0)
        cai = jnp.concatenate(ci[:8], axis=0)
        for j in range(sl):
            rows = slice(8 * j, 8 * j + 8)
            pr = pwr_ref[j:j + 1, re]
            pi = pwi_ref[j:j + 1, re]
            x_ref[rows, re] = x_ref[rows, re] + (pr * car - pi * cai)
            x_ref[rows, im] = x_ref[rows, im] + (pr * cai + pi * car)

    y_perm = jnp.dot(x_ref[...].astype(BF16), cbd_ref[...], preferred_element_type=F32)
    y = jnp.dot(permt_ref[...], y_perm, precision=lax.Precision.HIGHEST, preferred_element_type=F32) + d_ref[...] * u
    z = jax.nn.gelu(y)
    gate = jnp.dot(z.astype(BF16), wglu_ref[...], preferred_element_type=F32)
    o_ref[...] = (z * jax.nn.sigmoid(gate)).astype(o_ref.dtype)


S5_TILE = 256


def s5_branch(h_f32, col_block, bsz, seq, bbd, pw_re, pw_im, cbd, d_skip, w_glu, tt=S5_TILE):
    n = S5_GROUPS * S5_STATE
    nt = seq // tt
    sl = tt // 8
    row = jnp.arange(tt)
    perm = jax.nn.one_hot((row % 8) * sl + row // 8, tt, dtype=F32)
    const = lambda shape: pl.BlockSpec(shape, lambda b, t: (0, 0))
    return pl.pallas_call(
        functools.partial(_s5_kernel, tt=tt),
        grid=(bsz, nt),
        in_specs=[pl.BlockSpec((tt, S5_WIDTH), lambda b, t: (b * nt + t, col_block)),
                  const((tt, tt)), const((tt, tt)),
                  const((S5_WIDTH, 2 * n)), const((sl, n)), const((sl, n)), const((2 * n, S5_WIDTH)),
                  const((1, S5_WIDTH)), const((S5_WIDTH, S5_WIDTH))],
        out_specs=pl.BlockSpec((tt, S5_WIDTH), lambda b, t: (b * nt + t, 0)),
        out_shape=jax.ShapeDtypeStruct((bsz * seq, S5_WIDTH), BF16),
        scratch_shapes=[pltpu.VMEM((tt, 2 * n), F32), pltpu.VMEM((8, n), F32)],
        compiler_params=_params(("arbitrary", "arbitrary"), 48),
    )(h_f32, perm.astype(BF16), perm.T, bbd, pw_re, pw_im, cbd, d_skip, w_glu)


def s5_tables(a_re, a_im, log_dt, b_re, b_im, c_re, c_im, n_pow=S5_TILE // 8):
    g, p = a_re.shape
    dt = jnp.exp(log_dt)[:, None]
    mag = jnp.exp(a_re * dt)
    ab_re, ab_im = mag * jnp.cos(a_im * dt), mag * jnp.sin(a_im * dt)
    steps = jnp.arange(1, n_pow + 1, dtype=F32)[:, None, None]
    pw_mag = jnp.exp(a_re * dt * steps)
    pw_re = (pw_mag * jnp.cos(a_im * dt * steps)).reshape(n_pow, g * p)
    pw_im = (pw_mag * jnp.sin(a_im * dt * steps)).reshape(n_pow, g * p)
    den = a_re * a_re + a_im * a_im
    nr, ni = ab_re - 1.0, ab_im
    coef_re = (nr * a_re + ni * a_im) / den
    coef_im = (ni * a_re - nr * a_im) / den
    bb_re = coef_re[..., None] * b_re - coef_im[..., None] * b_im
    bb_im = coef_re[..., None] * b_im + coef_im[..., None] * b_re
    eye = jnp.eye(g, dtype=F32)
    to_bd_in = lambda t: jnp.einsum('gpc,gh->gchp', t, eye).reshape(g * S5_GROUP, g * p)
    to_bd_out = lambda t: jnp.einsum('gcp,gh->gphc', t, eye).reshape(g * p, g * S5_GROUP)
    bbd = jnp.concatenate([to_bd_in(bb_re), to_bd_in(bb_im)], axis=1).astype(BF16)
    cbd = jnp.concatenate([to_bd_out(c_re), to_bd_out(-c_im)], axis=0).astype(BF16)
    return bbd, pw_re, pw_im, cbd


def _log_sigmoid(x):
    return jnp.minimum(x, 0.0) - jnp.log1p(jnp.exp(-jnp.abs(x)))


def _mlstm_kernel(qk_ref, v_ref, og_ref, gate_ref, gbias_ref, convw_ref, o_ref,
                  xbuf_ref, c_ref, n_ref, m_ref, *, tt):
    L = MLSTM_L
    hd = MLSTM_DIM
    nh = MLSTM_HEADS
    i_col, f_col = 72, 76

    @pl.when(pl.program_id(1) == 0)
    def _():
        xbuf_ref[0:8, :] = jnp.zeros((8, 2 * nh * hd), F32)
        c_ref[...] = jnp.zeros_like(c_ref)
        n_ref[...] = jnp.zeros_like(n_ref)
        m_ref[...] = jnp.zeros_like(m_ref)

    xbuf_ref[8:8 + tt, :] = qk_ref[...]
    cw = convw_ref[...]
    conv = xbuf_ref[8:8 + tt, :] * cw[CONV_WIDTH - 1:CONV_WIDTH]
    for sh in range(1, CONV_WIDTH):
        conv = conv + xbuf_ref[8 - sh:8 - sh + tt, :] * cw[CONV_WIDTH - 1 - sh:CONV_WIDTH - sh]
    xbuf_ref[0:8, :] = xbuf_ref[tt:tt + 8, :]
    qk = conv * jax.nn.sigmoid(conv)

    gates = gate_ref[...] + gbias_ref[...]
    logf = _log_sigmoid(gates)
    row = lax.broadcasted_iota(jnp.int32, (L, L), 0)
    col = lax.broadcasted_iota(jnp.int32, (L, L), 1)
    causal = col <= row
    tri = jnp.where(causal, 1.0, 0.0).astype(F32)

    for c in range(tt // L):
        r0 = c * L
        gc = gates[r0:r0 + L]
        bcum = jnp.dot(tri, logf[r0:r0 + L], precision=lax.Precision.HIGHEST, preferred_element_type=F32)
        gct = gc.T
        bcumt = bcum.T
        for h in range(nh):
            ig_col = gc[:, i_col + h:i_col + h + 1]
            b_col = bcum[:, f_col + h:f_col + h + 1]
            ig_row = gct[i_col + h:i_col + h + 1, :]
            b_row = bcumt[f_col + h:f_col + h + 1, :]
            b_last = b_col[L - 1:L, :]
            q = qk[r0:r0 + L, h * hd:(h + 1) * hd]
            k = qk[r0:r0 + L, (nh + h) * hd:(nh + h + 1) * hd] * (hd ** -0.5)
            vb = v_ref[r0:r0 + L, h * hd:(h + 1) * hd].astype(BF16)
            qb = q.astype(BF16)
            kb = k.astype(BF16)
            c_prev = c_ref[h]
            n_prev = n_ref[h]
            m_prev = m_ref[h][:, 0:1]

            g = b_last - b_col + ig_col
            m_loc = jnp.max(g, axis=0, keepdims=True)
            wk = jnp.exp(g - m_loc) * k
            c_loc = jnp.dot(wk.T.astype(BF16), vb, preferred_element_type=F32)
            n_loc = jnp.sum(wk, axis=0, keepdims=True)

            m_inter = b_col + m_prev
            log_d = jnp.where(causal, b_col - b_row + ig_row, -jnp.inf)
            m_j = jnp.maximum(jnp.max(log_d, axis=1, keepdims=True), m_inter)
            dqk = jnp.exp(log_d - m_j) * lax.dot_general(qb, kb, NT_DIMS, preferred_element_type=F32)
            inter = jnp.exp(m_inter - m_j)
            num = (jnp.dot(dqk.astype(BF16), vb, preferred_element_type=F32)
                   + inter * jnp.dot(qb, c_prev.astype(BF16), preferred_element_type=F32))
            den = jnp.sum(dqk, axis=1, keepdims=True) + inter * jnp.sum(q * n_prev, axis=1, keepdims=True)
            hid = num / jnp.maximum(jnp.abs(den), jnp.exp(-m_j))
            og = jax.nn.sigmoid(og_ref[r0:r0 + L, h * hd:(h + 1) * hd])
            o_ref[r0:r0 + L, h * hd:(h + 1) * hd] = (og * hid).astype(o_ref.dtype)

            m_new = jnp.maximum(b_last + m_prev, m_loc)
            a = jnp.exp(b_last + m_prev - m_new)
            bcoef = jnp.exp(m_loc - m_new)
            c_ref[h] = a * c_prev + bcoef * c_loc
            n_ref[h] = a * n_prev + bcoef * n_loc
            m_ref[h] = jnp.broadcast_to(m_new, (1, LANES))


def mlstm_branch(h_f32, misc, gate_bias, conv_w, bsz, seq, qk_blk, v_blk, o_blk, tt=512):
    nh, hd = MLSTM_HEADS, MLSTM_DIM
    w = nh * hd
    nt = seq // tt
    return pl.pallas_call(
        functools.partial(_mlstm_kernel, tt=tt),
        grid=(bsz, nt),
        in_specs=[pl.BlockSpec((tt, 2 * w), lambda b, t: (b * nt + t, qk_blk)),
                  pl.BlockSpec((tt, w), lambda b, t: (b * nt + t, v_blk)),
                  pl.BlockSpec((tt, w), lambda b, t: (b * nt + t, o_blk)),
                  pl.BlockSpec((tt, LANES), lambda b, t: (b * nt + t, 0)),
                  pl.BlockSpec((1, LANES), lambda b, t: (0, 0)),
                  pl.BlockSpec((CONV_WIDTH, 2 * w), lambda b, t: (0, 0))],
        out_specs=pl.BlockSpec((tt, w), lambda b, t: (b * nt + t, 0)),
        out_shape=jax.ShapeDtypeStruct((bsz * seq, w), BF16),
        scratch_shapes=[pltpu.VMEM((tt + 8, 2 * w), F32), pltpu.VMEM((nh, hd, hd), F32),
                        pltpu.VMEM((nh, 1, hd), F32), pltpu.VMEM((nh, 1, LANES), F32)],
        compiler_params=_params(("arbitrary", "arbitrary"), 48),
    )(h_f32, h_f32, h_f32, misc, gate_bias, conv_w)


def rel_bucket(dist):
    max_exact = REL_BUCKETS // 2
    n = jnp.maximum(dist, 0)
    large = max_exact + (jnp.log(jnp.maximum(n, 1).astype(F32) / max_exact)
                         / math.log(REL_MAX_DIST / max_exact)
                         * (REL_BUCKETS - max_exact)).astype(jnp.int32)
    large = jnp.minimum(large, REL_BUCKETS - 1)
    return jnp.where(n < max_exact, n, large)


def bias_tiles(rel, t):
    assert t >= REL_MAX_DIST
    kk = jnp.arange(t)[:, None]
    qq = jnp.arange(t)[None, :]

    def lookup(dist):
        onehot = (rel_bucket(dist)[..., None] == jnp.arange(REL_BUCKETS)).astype(F32)
        return jnp.einsum('kqb,bh->hkq', onehot, rel.astype(F32), precision=lax.Precision.HIGHEST) * LOG2E

    t0 = jnp.where((qq >= kk)[None], lookup(qq - kk), NEG)
    t1 = lookup(t + qq - kk)
    t2 = lookup(jnp.full((t, t), 2 * t))
    return jnp.stack([t0, t1, t2])


def tri_pairs(nq):
    qi = [q for q in range(nq) for _ in range(q + 1)]
    kj = [k for q in range(nq) for k in range(q + 1)]
    return jnp.asarray(qi, jnp.int32), jnp.asarray(kj, jnp.int32)


def values_t_ext(h_bf16, col0, bsz, seq, heads, dv):
    v = h_bf16[:, col0:col0 + heads * dv].reshape(bsz, seq, heads, dv)
    vt = jnp.transpose(v, (0, 2, 3, 1))
    ones = jnp.ones((bsz, heads, V_ONES, seq), BF16)
    return jnp.concatenate([vt, ones], axis=2).reshape(bsz * heads * (dv + V_ONES), seq)


def _attn_maps(qi, kj, maps, bias_ref, mb_ref, s_ref, p_ref, m_ref, acc_ref):
    def run(const_bias):
        def qk(i):
            kh, qz, _, _ = maps[i]
            s_ref[i % 2] = lax.dot_general(kh, qz, NT_DIMS, preferred_element_type=F32)

        qk(0)
        for i, (_, _, vt_ext, head) in enumerate(maps):
            if i + 1 < len(maps):
                qk(i + 1)
            _softmax_pv(i, vt_ext, bias_ref, head, mb_ref, const_bias, s_ref, p_ref, m_ref, acc_ref)

    far = qi - kj >= 2

    @pl.when(far)
    def _():
        run(True)

    @pl.when(jnp.logical_not(far))
    def _():
        run(False)


def _softmax_pv(idx, vt_ext, bias_ref, head, mb_ref, const_bias, s_ref, p_ref, m_ref, acc_ref):
    tk, tq = s_ref.shape[1:]
    sb = s_ref.at[idx % 2]
    pb = p_ref.at[idx % 2]

    def logits(r0, n):
        t = sb[r0:r0 + n, :]
        if not const_bias:
            t = t + bias_ref[0, head, r0:r0 + n, :]
        if mb_ref is not None:
            t = t + mb_ref[r0:r0 + n, :]
        return t

    mx = [jnp.full((8, tq), -jnp.inf, F32) for _ in range(4)]
    for c in range(tk // 8):
        mx[c % 4] = jnp.maximum(mx[c % 4], logits(c * 8, 8))
    m_tile = jnp.max(jnp.maximum(jnp.maximum(mx[0], mx[1]), jnp.maximum(mx[2], mx[3])), axis=0, keepdims=True)
    m_prev = m_ref[idx]
    if const_bias:
        bias_c = bias_ref[0, head, 0:1, :]
        m_new = jnp.maximum(m_prev, m_tile + bias_c)
        m_b = jnp.broadcast_to(m_new - bias_c, (16, tq))
    else:
        m_new = jnp.maximum(m_prev, m_tile)
        m_b = jnp.broadcast_to(m_new, (16, tq))
    m_ref[idx] = m_new
    for c in range(tk // 16):
        pb[c * 16:(c + 1) * 16, :] = jnp.exp2(logits(c * 16, 16) - m_b).astype(BF16)
    alpha = jnp.exp2(m_prev - m_new)
    acc_ref[idx] = alpha * acc_ref[idx] + jnp.dot(vt_ext, pb[...], preferred_element_type=F32)


def _init_softmax_state(m_ref, acc_ref):
    m_ref[...] = jnp.full(m_ref.shape, NEG, F32)
    acc_ref[...] = jnp.zeros_like(acc_ref)


def _diff_attn_kernel(qi_ref, kj_ref, q_ref, k_ref, vt_ref, bias_ref, lam_ref, gb_ref, o_ref,
                      s_ref, p_ref, m_ref, acc_ref, *, out_scale):
    p = pl.program_id(1)
    qi = qi_ref[p]
    kj = kj_ref[p]
    dq = DIFF_QK_DIM
    dv = DIFF_V_DIM
    vrows = dv + V_ONES

    @pl.when(kj == 0)
    def _():
        _init_softmax_state(m_ref, acc_ref)

    lane = lax.broadcasted_iota(jnp.int32, (q_ref.shape[0], 2 * dq), 1)
    maps = []
    for h in range(DIFF_HEADS):
        qh = q_ref[:, h * 2 * dq:(h + 1) * 2 * dq]
        kh = k_ref[:, h * 2 * dq:(h + 1) * 2 * dq]
        vt = vt_ref[h * vrows:(h + 1) * vrows, :]
        for mi in range(2):
            qz = jnp.where((lane < dq) if mi == 0 else (lane >= dq), qh, jnp.zeros_like(qh))
            maps.append((kh, qz, vt, h))
    _attn_maps(qi, kj, maps, bias_ref, None, s_ref, p_ref, m_ref, acc_ref)

    @pl.when(kj == qi)
    def _():
        lam = lam_ref[0:1, 0:1]
        for h in range(DIFF_HEADS):
            a1 = acc_ref[2 * h]
            a2 = acc_ref[2 * h + 1]
            a = a1[0:dv] / a1[dv:dv + 1] - lam * (a2[0:dv] / a2[dv:dv + 1])
            y = a * lax.rsqrt(jnp.mean(a * a, axis=0, keepdims=True) + EPS) * gb_ref[...] * out_scale
            o_ref[:, h * dv:(h + 1) * dv] = y.T.astype(o_ref.dtype)


def diff_attention(h_bf16, vt_ext, bsz, seq, q_blk, k_blk, bias, lam_row, subln_g, out_scale, t=512):
    nq = seq // t
    qi, kj = tri_pairs(nq)
    w = DIFF_HEADS * DIFF_V_DIM
    nmap = 2 * DIFF_HEADS
    vrows = DIFF_HEADS * (DIFF_V_DIM + V_ONES)
    g_b = jnp.broadcast_to(subln_g.reshape(DIFF_V_DIM, 1), (DIFF_V_DIM, t))
    grid_spec = pltpu.PrefetchScalarGridSpec(
        num_scalar_prefetch=2,
        grid=(bsz, qi.shape[0]),
        in_specs=[pl.BlockSpec((t, w), lambda b, p, qi, kj: (b * nq + qi[p], q_blk)),
                  pl.BlockSpec((t, w), lambda b, p, qi, kj: (b * nq + kj[p], k_blk)),
                  pl.BlockSpec((vrows, t), lambda b, p, qi, kj: (b, kj[p])),
                  pl.BlockSpec((1, DIFF_HEADS, t, t),
                               lambda b, p, qi, kj: (jnp.minimum(qi[p] - kj[p], 2), 0, 0, 0)),
                  pl.BlockSpec((1, LANES), lambda b, p, qi, kj: (0, 0)),
                  pl.BlockSpec((DIFF_V_DIM, t), lambda b, p, qi, kj: (0, 0))],
        out_specs=pl.BlockSpec((t, w), lambda b, p, qi, kj: (b * nq + qi[p], 0)),
        scratch_shapes=[pltpu.VMEM((2, t, t), F32), pltpu.VMEM((2, t, t), BF16),
                        pltpu.VMEM((nmap, 1, t), F32), pltpu.VMEM((nmap, DIFF_V_DIM + V_ONES, t), F32)],
    )
    return pl.pallas_call(
        functools.partial(_diff_attn_kernel, out_scale=out_scale),
        grid_spec=grid_spec,
        out_shape=jax.ShapeDtypeStruct((bsz * seq, w), BF16),
        compiler_params=_params(("arbitrary", "arbitrary"), 48),
    )(qi, kj, h_bf16, h_bf16, vt_ext, bias, lam_row, g_b)


def _sort_key(x):
    b = lax.bitcast_convert_type(x, jnp.int32)
    return b ^ ((b >> 31) & 0x7FFFFFFF)


def _batcher_pairs(n):
    def merge(lo, hi, r):
        step = r * 2
        if step < hi - lo:
            yield from merge(lo, hi, step)
            yield from merge(lo + r, hi, step)
            yield from [(a, a + r) for a in range(lo + r, hi - r, step)]
        else:
            yield (lo, lo + r)

    def sort(lo, hi):
        if hi - lo >= 1:
            mid = lo + (hi - lo) // 2
            yield from sort(lo, mid)
            yield from sort(mid + 1, hi)
            yield from merge(lo, hi, 1)

    return tuple(sort(0, n - 1))


SORT_GROUP = 16


def _count_ge_sorted(v, cand):
    g = len(v)
    out, masks, level = [], [], 1
    while (1 << level) <= g:
        pivots = [v[(2 * k + 1) * g // (1 << level) - 1] for k in range(1 << (level - 1))]

        def pick(lo, count, bit):
            if count == 1:
                return pivots[lo]
            half = count // 2
            return jnp.where(masks[bit], pick(lo, half, bit + 1), pick(lo + half, half, bit + 1))

        masks.append(pick(0, len(pivots), 0) >= cand)
        out.append((masks[-1], float(g >> level)))
        level += 1
    out.append((v[g - 1] >= cand, 1.0))
    return out


def _dsa_select_kernel(iq_ref, w_ref, ikz_ref, o_ref, key_ref, srt_ref, a_ref, wb_ref, *, tq, tk, topk, seq):
    i = pl.program_id(1)
    last = (i * tq + tq - 1) // tk
    idx_scale = (IDX_DIM * IDX_HEADS) ** -0.5
    iw_row = 64
    grp = SORT_GROUP
    pairs = _batcher_pairs(grp)

    wt = w_ref[...].T
    for h in range(IDX_HEADS):
        wb_ref[h] = jnp.broadcast_to(wt[iw_row + h:iw_row + h + 1, :], (8, tq))

    def in_prefix(off, n):
        key_pos = off + lax.broadcasted_iota(jnp.int32, (n, tq), 0)
        q_pos = i * tq + lax.broadcasted_iota(jnp.int32, (n, tq), 1)
        return key_pos <= q_pos

    q_stack = [jnp.concatenate([iq_ref[:, (2 * pg) * LANES:(2 * pg + 1) * LANES],
                                iq_ref[:, (2 * pg + 1) * LANES:(2 * pg + 2) * LANES]], axis=0) for pg in range(2)]

    def head_dots(j, slot):
        off = pl.multiple_of(j * tk, tk)
        for half in range(2):
            kt = ikz_ref[pl.ds(off, tk), half * LANES:(half + 1) * LANES]
            for pg in range(2):
                a_ref[slot, 2 * half + pg] = lax.dot_general(kt, q_stack[pg], NT_DIMS, preferred_element_type=F32)

    def score_tile(j, slot, diagonal):
        off = pl.multiple_of(j * tk, tk)
        for g in range(tk // (8 * grp)):
            keys = []
            for c in range(grp):
                r0 = (g * grp + c) * 8
                sc = None
                for half in range(2):
                    for pg in range(2):
                        for s in range(2):
                            head = 2 * (2 * pg + s) + half
                            term = wb_ref[head] * jnp.maximum(
                                a_ref[slot, 2 * half + pg, r0:r0 + 8, s * tq:(s + 1) * tq], 0.0)
                            sc = term if sc is None else sc + term
                sc = sc * idx_scale
                if diagonal:
                    sc = jnp.where(in_prefix(off + r0, 8), sc, -jnp.inf)
                keys.append(_sort_key(sc))
            base = off + g * grp * 8
            for c in range(grp):
                key_ref[pl.ds(pl.multiple_of(base + c * 8, 8), 8), :] = keys[c]
            for lo, hi in pairs:
                keys[lo], keys[hi] = jnp.minimum(keys[lo], keys[hi]), jnp.maximum(keys[lo], keys[hi])
            for c in range(grp):
                srt_ref[pl.ds(pl.multiple_of(base + c * 8, 8), 8), :] = keys[c]

    def score_body(p, carry):
        head_dots(2 * p + 1, 1)
        score_tile(2 * p, 0, False)
        head_dots(2 * p + 2, 0)
        score_tile(2 * p + 1, 1, False)
        return carry

    head_dots(0, 0)
    lax.fori_loop(0, last // 2, score_body, 0)

    @pl.when(last % 2 == 0)
    def _():
        score_tile(last, 0, True)

    @pl.when(last % 2 == 1)
    def _():
        head_dots(last, 1)
        score_tile(last - 1, 0, False)
        score_tile(last, 1, True)

    n_pairs = (last + 2) // 2

    @pl.when(n_pairs * 2 > last + 1)
    def _():
        srt_ref[pl.ds(pl.multiple_of((last + 1) * tk, tk), tk), :] = jnp.full((tk, tq), INT_MIN, jnp.int32)

    n_terms = grp.bit_length()

    def bit_body(it, thr):
        cand = thr + lax.shift_left(jnp.int32(1), 31 - it)
        cand_b = jnp.broadcast_to(cand, (8, tq))

        def count_body(j, accs):
            blk = srt_ref[pl.ds(pl.multiple_of(j * 2 * tk, 2 * tk), 2 * tk), :]
            accs = list(accs)
            for g in range(2 * tk // (8 * grp)):
                run = [blk[(g * grp + c) * 8:(g * grp + c + 1) * 8, :] for c in range(grp)]
                for t, (mask, _) in enumerate(_count_ge_sorted(run, cand_b)):
                    accs[t] = accs[t] + jnp.where(mask, 1.0, 0.0)
            return tuple(accs)

        zero = jnp.zeros((8, tq), F32)
        accs = lax.fori_loop(0, n_pairs, count_body, (zero,) * n_terms)
        weights = [float(grp >> (t + 1)) for t in range(n_terms - 1)] + [1.0]
        total = accs[0] * weights[0]
        for t in range(1, n_terms):
            total = total + accs[t] * weights[t]
        cnt = jnp.sum(total, axis=0, keepdims=True)
        return jnp.where(cnt >= topk, cand, thr)

    thr = lax.fori_loop(0, 32, bit_body, jnp.full((1, tq), INT_MIN, jnp.int32))
    thr_b = jnp.broadcast_to(thr, (16, tq))

    def out_tile(j, diagonal):
        off = pl.multiple_of(j * tk, tk)
        for c in range(tk // 16):
            rows = pl.ds(pl.multiple_of(off + c * 16, 16), 16)
            keep = key_ref[rows, :] >= thr_b
            if diagonal:
                keep = keep & in_prefix(off + c * 16, 16)
            o_ref[0, rows, :] = jnp.where(keep, 0.0, NEG).astype(o_ref.dtype)

    def out_body(j, carry):
        out_tile(j, False)
        return carry

    lax.fori_loop(0, last, out_body, 0)
    out_tile(last, True)

    def fill_body(j, carry):
        o_ref[0, pl.ds(pl.multiple_of(j * tk, tk), tk), :] = jnp.full((tk, tq), NEG, o_ref.dtype)
        return carry

    lax.fori_loop(last + 1, seq // tk, fill_body, 0)


def dsa_select(h_bf16, misc, ikz, bsz, seq, iq_blk, tq=128, tk=512):
    nq = seq // tq
    topk = min(TOPK_MAX, seq // 4)
    w = IDX_HEADS * IDX_DIM
    return pl.pallas_call(
        functools.partial(_dsa_select_kernel, tq=tq, tk=tk, topk=topk, seq=seq),
        grid=(bsz, nq),
        in_specs=[pl.BlockSpec((tq, w), lambda b, i: (b * nq + i, iq_blk)),
                  pl.BlockSpec((tq, LANES), lambda b, i: (b * nq + i, 0)),
                  pl.BlockSpec((seq, 2 * LANES), lambda b, i: (b, 0), pipeline_mode=pl.Buffered(1))],
        out_specs=pl.BlockSpec((1, seq, tq), lambda b, i: (b, 0, i)),
        out_shape=jax.ShapeDtypeStruct((bsz, seq, seq), BF16),
        scratch_shapes=[pltpu.VMEM((seq, tq), jnp.int32), pltpu.VMEM((seq, tq), jnp.int32),
                        pltpu.VMEM((2, IDX_HEADS // 2, tk, 2 * tq), F32), pltpu.VMEM((IDX_HEADS, 8, tq), F32)],
        compiler_params=_params(("arbitrary", "arbitrary"), 48),
    )(h_bf16, misc, ikz)


def _dsa_attn_kernel(qi_ref, kj_ref, q_ref, k_ref, vt_ref, bias_ref, mask_ref, o_ref,
                     s_ref, p_ref, mb_ref, m_ref, acc_ref):
    p = pl.program_id(1)
    qi = qi_ref[p]
    kj = kj_ref[p]
    dh = HEAD_DIM
    vrows = dh + V_ONES

    @pl.when(kj == 0)
    def _():
        _init_softmax_state(m_ref, acc_ref)

    mb_ref[...] = mask_ref[0].astype(F32)
    maps = [(k_ref[:, h * dh:(h + 1) * dh], q_ref[:, h * dh:(h + 1) * dh], vt_ref[h * vrows:(h + 1) * vrows, :], h)
            for h in range(DSA_HEADS)]
    _attn_maps(qi, kj, maps, bias_ref, mb_ref, s_ref, p_ref, m_ref, acc_ref)

    @pl.when(kj == qi)
    def _():
        for h in range(DSA_HEADS):
            a = acc_ref[h]
            o_ref[:, h * dh:(h + 1) * dh] = (a[0:dh] / a[dh:dh + 1]).T.astype(o_ref.dtype)


def dsa_attention(h_bf16, vt_ext, mask, bsz, seq, q_blk, k_blk, bias, t=512):
    nq = seq // t
    qi, kj = tri_pairs(nq)
    w = DSA_HEADS * HEAD_DIM
    vrows = DSA_HEADS * (HEAD_DIM + V_ONES)
    grid_spec = pltpu.PrefetchScalarGridSpec(
        num_scalar_prefetch=2,
        grid=(bsz, qi.shape[0]),
        in_specs=[pl.BlockSpec((t, w), lambda b, p, qi, kj: (b * nq + qi[p], q_blk)),
                  pl.BlockSpec((t, w), lambda b, p, qi, kj: (b * nq + kj[p], k_blk)),
                  pl.BlockSpec((vrows, t), lambda b, p, qi, kj: (b, kj[p])),
                  pl.BlockSpec((1, DSA_HEADS, t, t),
                               lambda b, p, qi, kj: (jnp.minimum(qi[p] - kj[p], 2), 0, 0, 0)),
                  pl.BlockSpec((1, t, t), lambda b, p, qi, kj: (b, kj[p], qi[p]))],
        out_specs=pl.BlockSpec((t, w), lambda b, p, qi, kj: (b * nq + qi[p], 0)),
        scratch_shapes=[pltpu.VMEM((2, t, t), F32), pltpu.VMEM((2, t, t), BF16), pltpu.VMEM((t, t), F32),
                        pltpu.VMEM((DSA_HEADS, 1, t), F32), pltpu.VMEM((DSA_HEADS, HEAD_DIM + V_ONES, t), F32)],
    )
    return pl.pallas_call(
        _dsa_attn_kernel,
        grid_spec=grid_spec,
        out_shape=jax.ShapeDtypeStruct((bsz * seq, w), BF16),
        compiler_params=_params(("arbitrary", "arbitrary"), 48),
    )(qi, kj, h_bf16, h_bf16, vt_ext, bias, mask)


_SPLITS = (512, 512, 512, 512, 64, 8, 512, 512, 512, 512, 1024, 512, 512, 4, 4)
_NAMES = ('a_q', 'a_k', 'a_v', 'a_iq', 'a_ik', 'a_iw', 'b_u', 'c_q', 'c_k', 'c_v', 'd_qk', 'd_v', 'd_o', 'd_i', 'd_f')


def _split_w_in(w_in):
    out, off = {}, 0
    for name, width in zip(_NAMES, _SPLITS):
        out[name] = w_in[:, off:off + width]
        off += width
    return out


def _group_w_in(w_in):
    c = _split_w_in(w_in)
    d = w_in.shape[0]
    z = lambda n: jnp.zeros((d, n), w_in.dtype)
    w_attn = jnp.concatenate([c['a_q'], c['a_k'], c['a_v'], c['a_iq'], c['c_q'], c['c_k'], c['c_v']], axis=1)
    w_scan = jnp.concatenate([c['d_qk'], c['b_u'], c['d_v'], c['d_o']], axis=1)
    w_misc = jnp.concatenate([c['a_ik'], z(64), z(64), c['a_ik'],
                              z(64), c['a_iw'], c['d_i'], c['d_f'], z(48)], axis=1)
    return w_attn.astype(BF16), w_scan.astype(BF16), w_misc.astype(BF16)


def _pad_ffn(w_in, w_out):
    d = w_in.shape[0]
    pad = D_FF_PAD - D_FF
    zi = jnp.zeros((d, pad), w_in.dtype)
    w_in_pad = jnp.concatenate([w_in[:, :D_FF], zi, w_in[:, D_FF:], zi], axis=1).astype(BF16)
    w_out_pad = jnp.concatenate([w_out, jnp.zeros((pad, d), w_out.dtype)], axis=0).astype(BF16)
    return w_in_pad, w_out_pad


def _branches(xn, l, bsz, seq, w_in, conv_w, i_bias, f_bias, s5, s5_d, s5_w_glu, diff_lambda,
              diff_subln_g, bias_a, bias_c):
    lambda_init = 0.8 - 0.6 * math.exp(-0.3 * l)
    w_attn, w_scan, w_misc = _group_w_in(w_in)
    ones = jnp.ones((512,), F32)
    attn_scale = jnp.concatenate([ones * (HEAD_DIM ** -0.5 * LOG2E), ones, ones, ones,
                                  ones * (DIFF_QK_DIM ** -0.5 * LOG2E), ones, ones]).reshape(1, -1)
    h_attn = matmul(xn, w_attn, attn_scale, BF16)
    h_scan = matmul(xn, w_scan, jnp.ones((1, w_scan.shape[1]), F32), F32)
    ikz, misc = misc_proj(xn, w_misc)

    mask = dsa_select(h_attn, misc, ikz, bsz, seq, iq_blk=3)
    vt_a = values_t_ext(h_attn, 2 * 512, bsz, seq, DSA_HEADS, HEAD_DIM)
    y_a = dsa_attention(h_attn, vt_a, mask, bsz, seq, 0, 1, bias_a)

    bbd, abar_re, abar_im, cbd = s5
    y_b = s5_branch(h_scan, 2, bsz, seq, bbd, abar_re, abar_im, cbd, s5_d.reshape(1, -1), s5_w_glu.astype(BF16))

    dl = diff_lambda.astype(F32)
    lam = jnp.exp(jnp.sum(dl[0] * dl[1])) - jnp.exp(jnp.sum(dl[2] * dl[3])) + lambda_init
    lam_row = jnp.full((1, LANES), lam, F32)
    vt_c = values_t_ext(h_attn, 6 * 512, bsz, seq, DIFF_HEADS, DIFF_V_DIM)
    y_c = diff_attention(h_attn, vt_c, bsz, seq, 4, 5, bias_c, lam_row, diff_subln_g, 1.0 - lambda_init)

    gate_bias = jnp.zeros((1, LANES), F32).at[0, 72:76].set(i_bias).at[0, 76:80].set(f_bias)
    y_d = mlstm_branch(h_scan, misc, gate_bias, conv_w, bsz, seq, qk_blk=0, v_blk=3, o_blk=4)
    return y_a, y_b, y_c, y_d


def _mixer(x, l, bsz, seq, norm_g, w_in, conv_w, i_bias, f_bias, s5, s5_d, s5_w_glu, diff_lambda,
           diff_subln_g, bias_a, bias_c, w_gate, b_gate, w_branch, w_out):
    xn = rmsnorm(x, norm_g, BF16)
    ys = _branches(xn, l, bsz, seq, w_in, conv_w, i_bias, f_bias, s5, s5_d, s5_w_glu, diff_lambda,
                   diff_subln_g, bias_a, bias_c)
    merged = gated_merge(xn, ys, w_gate.astype(BF16), b_gate, w_branch.astype(BF16))
    return matmul_residual(merged, w_out.astype(BF16), x)


def kernel(x, norm_g, w_ffn_in, w_ffn_out, w_in, conv_w, mlstm_i_bias, mlstm_f_bias, s5_a_re, s5_a_im,
           s5_log_dt, s5_b_re, s5_b_im, s5_c_re, s5_c_im, s5_d, s5_w_glu, diff_lambda, diff_subln_g,
           rel_table, w_gate, b_gate, w_branch, w_out, final_g):
    bsz, seq, d = x.shape
    t_attn = 512
    bias_a = bias_tiles(rel_table[:, :DSA_HEADS], t_attn)
    bias_c = bias_tiles(rel_table[:, DSA_HEADS:], t_attn)
    xf = x.reshape(bsz * seq, d)
    for l in range(DEPTH):
        xf = ffn_block(xf, norm_g[l, 0], *_pad_ffn(w_ffn_in[l, 0], w_ffn_out[l, 0]))
        s5 = s5_tables(s5_a_re[l], s5_a_im[l], s5_log_dt[l], s5_b_re[l], s5_b_im[l], s5_c_re[l], s5_c_im[l])
        xf = _mixer(xf, l, bsz, seq, norm_g[l, 1], w_in[l], conv_w[l], mlstm_i_bias[l], mlstm_f_bias[l],
                    s5, s5_d[l], s5_w_glu[l], diff_lambda[l], diff_subln_g[l], bias_a, bias_c,
                    w_gate[l], b_gate[l], w_branch[l], w_out[l])
        xf = ffn_block(xf, norm_g[l, 2], *_pad_ffn(w_ffn_in[l, 1], w_ffn_out[l, 1]))
    return rmsnorm(xf, final_g, F32).reshape(bsz, seq, d)
```

```python
import functools
import math

import jax
import jax.numpy as jnp
from jax import lax
from jax.experimental import pallas as pl
from jax.experimental.pallas import tpu as pltpu

F32 = jnp.float32
BF16 = jnp.bfloat16

D_MODEL = 2048
DEPTH = 4
N_BRANCH = 4
BRANCH_WIDTH = 512
HEAD_DIM = 128
DSA_HEADS = 4
IDX_HEADS = 8
IDX_DIM = 64
TOPK_MAX = 256
S5_WIDTH = 512
S5_GROUP = 16
S5_GROUPS = S5_WIDTH // S5_GROUP
S5_STATE = 64
DIFF_HEADS = 4
DIFF_QK_DIM = 64
DIFF_V_DIM = 128
MLSTM_HEADS = 4
MLSTM_DIM = 128
CONV_WIDTH = 4
D_FF = 5504
REL_BUCKETS = 32
REL_MAX_DIST = 128
EPS = 1e-6

LANES = 128
NEG = -1e30
INT_MIN = -(2 ** 31)
D_FF_PAD = 5632
MLSTM_L = 128

V_ONES = 16
LOG2E = math.log2(math.e)

NT_DIMS = (((1,), (1,)), ((), ()))


def _params(sem, vmem_mb):
    return pltpu.CompilerParams(dimension_semantics=sem, vmem_limit_bytes=vmem_mb * 1024 * 1024)


def _rms(x, g):
    return x * lax.rsqrt(jnp.mean(x * x, axis=-1, keepdims=True) + EPS) * g


def _rmsnorm_kernel(x_ref, g_ref, o_ref):
    o_ref[...] = _rms(x_ref[...], g_ref[...]).astype(o_ref.dtype)


def rmsnorm(x, g, out_dtype, tm=512):
    m, d = x.shape
    return pl.pallas_call(
        _rmsnorm_kernel,
        grid=(m // tm,),
        in_specs=[pl.BlockSpec((tm, d), lambda i: (i, 0)), pl.BlockSpec((1, d), lambda i: (0, 0))],
        out_specs=pl.BlockSpec((tm, d), lambda i: (i, 0)),
        out_shape=jax.ShapeDtypeStruct((m, d), out_dtype),
        compiler_params=_params(("parallel",), 40),
    )(x, g.reshape(1, d))


def _mm_kernel(x_ref, w_ref, cs_ref, o_ref):
    acc = jnp.dot(x_ref[...], w_ref[...], preferred_element_type=F32)
    o_ref[...] = (acc * cs_ref[...]).astype(o_ref.dtype)


def matmul(x, w, col_scale, out_dtype, tm=1024, tn=512):
    m, k = x.shape
    n = w.shape[1]
    return pl.pallas_call(
        _mm_kernel,
        grid=(m // tm, n // tn),
        in_specs=[pl.BlockSpec((tm, k), lambda i, j: (i, 0)), pl.BlockSpec((k, tn), lambda i, j: (0, j)),
                  pl.BlockSpec((1, tn), lambda i, j: (0, j))],
        out_specs=pl.BlockSpec((tm, tn), lambda i, j: (i, j)),
        out_shape=jax.ShapeDtypeStruct((m, n), out_dtype),
        compiler_params=_params(("parallel", "arbitrary"), 40),
    )(x, w, col_scale)


def _mm_res_kernel(x_ref, w_ref, r_ref, o_ref):
    o_ref[...] = r_ref[...] + jnp.dot(x_ref[...], w_ref[...], preferred_element_type=F32)


def matmul_residual(x, w, r, tm=1024, tn=512):
    m, k = x.shape
    n = w.shape[1]
    return pl.pallas_call(
        _mm_res_kernel,
        grid=(m // tm, n // tn),
        in_specs=[pl.BlockSpec((tm, k), lambda i, j: (i, 0)), pl.BlockSpec((k, tn), lambda i, j: (0, j)),
                  pl.BlockSpec((tm, tn), lambda i, j: (i, j))],
        out_specs=pl.BlockSpec((tm, tn), lambda i, j: (i, j)),
        out_shape=jax.ShapeDtypeStruct((m, n), F32),
        compiler_params=_params(("parallel", "arbitrary"), 40),
    )(x, w, r)


def _misc_proj_kernel(x_ref, w_ref, ikz_ref, misc_ref):
    acc = jnp.dot(x_ref[...], w_ref[...], preferred_element_type=F32)
    ikz_ref[...] = acc[:, :2 * LANES].astype(BF16)
    misc_ref[...] = acc[:, 2 * LANES:]


def misc_proj(xn, w, tm=1024):
    m, k = xn.shape
    n = w.shape[1]
    return pl.pallas_call(
        _misc_proj_kernel,
        grid=(m // tm,),
        in_specs=[pl.BlockSpec((tm, k), lambda i: (i, 0)), pl.BlockSpec((k, n), lambda i: (0, 0))],
        out_specs=[pl.BlockSpec((tm, 2 * LANES), lambda i: (i, 0)), pl.BlockSpec((tm, LANES), lambda i: (i, 0))],
        out_shape=[jax.ShapeDtypeStruct((m, 2 * LANES), BF16), jax.ShapeDtypeStruct((m, LANES), F32)],
        compiler_params=_params(("parallel",), 40),
    )(xn, w)


def _ffn_kernel(x_ref, g_ref, wg_ref, wu_ref, wo_ref, o_ref, xn_ref):
    j = pl.program_id(1)

    @pl.when(j == 0)
    def _():
        x = x_ref[...]
        xn_ref[...] = _rms(x, g_ref[...]).astype(BF16)
        o_ref[...] = x

    xn = xn_ref[...]
    g = jnp.dot(xn, wg_ref[...], preferred_element_type=F32)
    u = jnp.dot(xn, wu_ref[...], preferred_element_type=F32)
    a = (g * jax.nn.sigmoid(g) * u * 0.5).astype(BF16)
    o_ref[...] += jnp.dot(a, wo_ref[...], preferred_element_type=F32)


def ffn_block(x, g, w_in_pad, w_out_pad, tm=1024, tf=512):
    m, d = x.shape
    nf = w_out_pad.shape[0] // tf
    return pl.pallas_call(
        _ffn_kernel,
        grid=(m // tm, nf),
        in_specs=[pl.BlockSpec((tm, d), lambda i, j: (i, 0)),
                  pl.BlockSpec((1, d), lambda i, j: (0, 0)),
                  pl.BlockSpec((d, tf), lambda i, j: (0, j)),
                  pl.BlockSpec((d, tf), lambda i, j: (0, j + nf)),
                  pl.BlockSpec((tf, d), lambda i, j: (j, 0))],
        out_specs=pl.BlockSpec((tm, d), lambda i, j: (i, 0)),
        out_shape=jax.ShapeDtypeStruct((m, d), F32),
        scratch_shapes=[pltpu.VMEM((tm, d), BF16)],
        compiler_params=_params(("parallel", "arbitrary"), 56),
    )(x, g.reshape(1, d), w_in_pad, w_in_pad, w_out_pad)


def _merge_kernel(xn_ref, ya_ref, yb_ref, yc_ref, yd_ref, wg_ref, bg_ref, wb_ref, o_ref):
    xn = xn_ref[...]
    acc = None
    for n, y_ref in enumerate((ya_ref, yb_ref, yc_ref, yd_ref)):
        gate = jnp.dot(xn, wg_ref[n], preferred_element_type=F32) + bg_ref[n]
        proj = jnp.dot(y_ref[...], wb_ref[n], preferred_element_type=F32)
        term = jax.nn.sigmoid(gate) * proj
        acc = term if acc is None else acc + term
    o_ref[...] = acc.astype(o_ref.dtype)


def gated_merge(xn, ys, w_gate, b_gate, w_branch, tm=1024, tn=256):
    m, d = xn.shape
    bw = ys[0].shape[1]
    y_spec = pl.BlockSpec((tm, bw), lambda i, j: (i, 0))
    return pl.pallas_call(
        _merge_kernel,
        grid=(m // tm, d // tn),
        in_specs=[pl.BlockSpec((tm, d), lambda i, j: (i, 0)), y_spec, y_spec, y_spec, y_spec,
                  pl.BlockSpec((N_BRANCH, d, tn), lambda i, j: (0, 0, j)),
                  pl.BlockSpec((N_BRANCH, 1, tn), lambda i, j: (0, 0, j)),
                  pl.BlockSpec((N_BRANCH, bw, tn), lambda i, j: (0, 0, j))],
        out_specs=pl.BlockSpec((tm, tn), lambda i, j: (i, j)),
        out_shape=jax.ShapeDtypeStruct((m, d), BF16),
        compiler_params=_params(("parallel", "arbitrary"), 48),
    )(xn, *ys, w_gate, b_gate.reshape(N_BRANCH, 1, d), w_branch)


def _s5_kernel(u_ref, perm_ref, permt_ref, bbd_ref, pwr_ref, pwi_ref, cbd_ref, d_ref, wglu_ref, o_ref,
               x_ref, st_ref, *, tt):
    n = S5_GROUPS * S5_STATE
    sl = tt // 8
    lc = 512

    @pl.when(pl.program_id(1) == 0)
    def _():
        st_ref[...] = jnp.zeros_like(st_ref)

    u = u_ref[...]
    u_perm = jnp.dot(perm_ref[...], u.astype(BF16), preferred_element_type=F32).astype(BF16)
    x_ref[...] = jnp.dot(u_perm, bbd_ref[...], preferred_element_type=F32)

    for c0 in range(0, n, lc):
        re = slice(c0, c0 + lc)
        im = slice(n + c0, n + c0 + lc)
        ar = pwr_ref[0:1, re]
        ai = pwi_ref[0:1, re]
        xr = jnp.zeros((8, lc), F32)
        xi = jnp.zeros((8, lc), F32)
        for j in range(sl):
            rows = slice(8 * j, 8 * j + 8)
            nr = ar * xr - ai * xi + x_ref[rows, re]
            ni = ar * xi + ai * xr + x_ref[rows, im]
            xr, xi = nr, ni
            x_ref[rows, re] = xr
            x_ref[rows, im] = xi

        alr = pwr_ref[sl - 1:sl, re]
        ali = pwi_ref[sl - 1:sl, re]
        cr = [st_ref[0:1, re]]
        ci = [st_ref[1:2, re]]
        for s in range(1, 9):
            pr, pi = cr[-1], ci[-1]
            cr.append(xr[s - 1:s] + alr * pr - ali * pi)
            ci.append(xi[s - 1:s] + alr * pi + ali * pr)
        st_ref[0:1, re] = cr[8]
        st_ref[1:2, re] = ci[8]
        car = jnp.concatenate(cr[:8], axis=0)
        cai = jnp.concatenate(ci[:8], axis=0)
        for j in range(sl):
            rows = slice(8 * j, 8 * j + 8)
            pr = pwr_ref[j:j + 1, re]
            pi = pwi_ref[j:j + 1, re]
            x_ref[rows, re] = x_ref[rows, re] + (pr * car - pi * cai)
            x_ref[rows, im] = x_ref[rows, im] + (pr * cai + pi * car)

    y_perm = jnp.dot(x_ref[...].astype(BF16), cbd_ref[...], preferred_element_type=F32)
    y = jnp.dot(permt_ref[...], y_perm, precision=lax.Precision.HIGHEST, preferred_element_type=F32) + d_ref[...] * u
    z = jax.nn.gelu(y)
    gate = jnp.dot(z.astype(BF16), wglu_ref[...], preferred_element_type=F32)
    o_ref[...] = (z * jax.nn.sigmoid(gate)).astype(o_ref.dtype)


S5_TILE = 256


def s5_branch(h_f32, col_block, bsz, seq, bbd, pw_re, pw_im, cbd, d_skip, w_glu, tt=S5_TILE):
    n = S5_GROUPS * S5_STATE
    nt = seq // tt
    sl = tt // 8
    row = jnp.arange(tt)
    perm = jax.nn.one_hot((row % 8) * sl + row // 8, tt, dtype=F32)
    const = lambda shape: pl.BlockSpec(shape, lambda b, t: (0, 0))
    return pl.pallas_call(
        functools.partial(_s5_kernel, tt=tt),
        grid=(bsz, nt),
        in_specs=[pl.BlockSpec((tt, S5_WIDTH), lambda b, t: (b * nt + t, col_block)),
                  const((tt, tt)), const((tt, tt)),
                  const((S5_WIDTH, 2 * n)), const((sl, n)), const((sl, n)), const((2 * n, S5_WIDTH)),
                  const((1, S5_WIDTH)), const((S5_WIDTH, S5_WIDTH))],
        out_specs=pl.BlockSpec((tt, S5_WIDTH), lambda b, t: (b * nt + t, 0)),
        out_shape=jax.ShapeDtypeStruct((bsz * seq, S5_WIDTH), BF16),
        scratch_shapes=[pltpu.VMEM((tt, 2 * n), F32), pltpu.VMEM((8, n), F32)],
        compiler_params=_params(("arbitrary", "arbitrary"), 48),
    )(h_f32, perm.astype(BF16), perm.T, bbd, pw_re, pw_im, cbd, d_skip, w_glu)


def s5_tables(a_re, a_im, log_dt, b_re, b_im, c_re, c_im, n_pow=S5_TILE // 8):
    g, p = a_re.shape
    dt = jnp.exp(log_dt)[:, None]
    mag = jnp.exp(a_re * dt)
    ab_re, ab_im = mag * jnp.cos(a_im * dt), mag * jnp.sin(a_im * dt)
    steps = jnp.arange(1, n_pow + 1, dtype=F32)[:, None, None]
    pw_mag = jnp.exp(a_re * dt * steps)
    pw_re = (pw_mag * jnp.cos(a_im * dt * steps)).reshape(n_pow, g * p)
    pw_im = (pw_mag * jnp.sin(a_im * dt * steps)).reshape(n_pow, g * p)
    den = a_re * a_re + a_im * a_im
    nr, ni = ab_re - 1.0, ab_im
    coef_re = (nr * a_re + ni * a_im) / den
    coef_im = (ni * a_re - nr * a_im) / den
    bb_re = coef_re[..., None] * b_re - coef_im[..., None] * b_im
    bb_im = coef_re[..., None] * b_im + coef_im[..., None] * b_re
    eye = jnp.eye(g, dtype=F32)
    to_bd_in = lambda t: jnp.einsum('gpc,gh->gchp', t, eye).reshape(g * S5_GROUP, g * p)
    to_bd_out = lambda t: jnp.einsum('gcp,gh->gphc', t, eye).reshape(g * p, g * S5_GROUP)
    bbd = jnp.concatenate([to_bd_in(bb_re), to_bd_in(bb_im)], axis=1).astype(BF16)
    cbd = jnp.concatenate([to_bd_out(c_re), to_bd_out(-c_im)], axis=0).astype(BF16)
    return bbd, pw_re, pw_im, cbd


def _log_sigmoid(x):
    return jnp.minimum(x, 0.0) - jnp.log1p(jnp.exp(-jnp.abs(x)))


def _mlstm_kernel(qk_ref, v_ref, og_ref, gate_ref, gbias_ref, convw_ref, o_ref,
                  xbuf_ref, c_ref, n_ref, m_ref, *, tt):
    L = MLSTM_L
    hd = MLSTM_DIM
    nh = MLSTM_HEADS
    i_col, f_col = 72, 76

    @pl.when(pl.program_id(1) == 0)
    def _():
        xbuf_ref[0:8, :] = jnp.zeros((8, 2 * nh * hd), F32)
        c_ref[...] = jnp.zeros_like(c_ref)
        n_ref[...] = jnp.zeros_like(n_ref)
        m_ref[...] = jnp.zeros_like(m_ref)

    xbuf_ref[8:8 + tt, :] = qk_ref[...]
    cw = convw_ref[...]
    conv = xbuf_ref[8:8 + tt, :] * cw[CONV_WIDTH - 1:CONV_WIDTH]
    for sh in range(1, CONV_WIDTH):
        conv = conv + xbuf_ref[8 - sh:8 - sh + tt, :] * cw[CONV_WIDTH - 1 - sh:CONV_WIDTH - sh]
    xbuf_ref[0:8, :] = xbuf_ref[tt:tt + 8, :]
    qk = conv * jax.nn.sigmoid(conv)

    gates = gate_ref[...] + gbias_ref[...]
    logf = _log_sigmoid(gates)
    row = lax.broadcasted_iota(jnp.int32, (L, L), 0)
    col = lax.broadcasted_iota(jnp.int32, (L, L), 1)
    causal = col <= row
    tri = jnp.where(causal, 1.0, 0.0).astype(F32)

    for c in range(tt // L):
        r0 = c * L
        gc = gates[r0:r0 + L]
        bcum = jnp.dot(tri, logf[r0:r0 + L], precision=lax.Precision.HIGHEST, preferred_element_type=F32)
        gct = gc.T
        bcumt = bcum.T
        for h in range(nh):
            ig_col = gc[:, i_col + h:i_col + h + 1]
            b_col = bcum[:, f_col + h:f_col + h + 1]
            ig_row = gct[i_col + h:i_col + h + 1, :]
            b_row = bcumt[f_col + h:f_col + h + 1, :]
            b_last = b_col[L - 1:L, :]
            q = qk[r0:r0 + L, h * hd:(h + 1) * hd]
            k = qk[r0:r0 + L, (nh + h) * hd:(nh + h + 1) * hd] * (hd ** -0.5)
            vb = v_ref[r0:r0 + L, h * hd:(h + 1) * hd].astype(BF16)
            qb = q.astype(BF16)
            kb = k.astype(BF16)
            c_prev = c_ref[h]
            n_prev = n_ref[h]
            m_prev = m_ref[h][:, 0:1]

            g = b_last - b_col + ig_col
            m_loc = jnp.max(g, axis=0, keepdims=True)
            wk = jnp.exp(g - m_loc) * k
            c_loc = jnp.dot(wk.T.astype(BF16), vb, preferred_element_type=F32)
            n_loc = jnp.sum(wk, axis=0, keepdims=True)

            m_inter = b_col + m_prev
            log_d = jnp.where(causal, b_col - b_row + ig_row, -jnp.inf)
            m_j = jnp.maximum(jnp.max(log_d, axis=1, keepdims=True), m_inter)
            dqk = jnp.exp(log_d - m_j) * lax.dot_general(qb, kb, NT_DIMS, preferred_element_type=F32)
            inter = jnp.exp(m_inter - m_j)
            num = (jnp.dot(dqk.astype(BF16), vb, preferred_element_type=F32)
                   + inter * jnp.dot(qb, c_prev.astype(BF16), preferred_element_type=F32))
            den = jnp.sum(dqk, axis=1, keepdims=True) + inter * jnp.sum(q * n_prev, axis=1, keepdims=True)
            hid = num / jnp.maximum(jnp.abs(den), jnp.exp(-m_j))
            og = jax.nn.sigmoid(og_ref[r0:r0 + L, h * hd:(h + 1) * hd])
            o_ref[r0:r0 + L, h * hd:(h + 1) * hd] = (og * hid).astype(o_ref.dtype)

            m_new = jnp.maximum(b_last + m_prev, m_loc)
            a = jnp.exp(b_last + m_prev - m_new)
            bcoef = jnp.exp(m_loc - m_new)
            c_ref[h] = a * c_prev + bcoef * c_loc
            n_ref[h] = a * n_prev + bcoef * n_loc
            m_ref[h] = jnp.broadcast_to(m_new, (1, LANES))


def mlstm_branch(h_f32, misc, gate_bias, conv_w, bsz, seq, qk_blk, v_blk, o_blk, tt=512):
    nh, hd = MLSTM_HEADS, MLSTM_DIM
    w = nh * hd
    nt = seq // tt
    return pl.pallas_call(
        functools.partial(_mlstm_kernel, tt=tt),
        grid=(bsz, nt),
        in_specs=[pl.BlockSpec((tt, 2 * w), lambda b, t: (b * nt + t, qk_blk)),
                  pl.BlockSpec((tt, w), lambda b, t: (b * nt + t, v_blk)),
                  pl.BlockSpec((tt, w), lambda b, t: (b * nt + t, o_blk)),
                  pl.BlockSpec((tt, LANES), lambda b, t: (b * nt + t, 0)),
                  pl.BlockSpec((1, LANES), lambda b, t: (0, 0)),
                  pl.BlockSpec((CONV_WIDTH, 2 * w), lambda b, t: (0, 0))],
        out_specs=pl.BlockSpec((tt, w), lambda b, t: (b * nt + t, 0)),
        out_shape=jax.ShapeDtypeStruct((bsz * seq, w), BF16),
        scratch_shapes=[pltpu.VMEM((tt + 8, 2 * w), F32), pltpu.VMEM((nh, hd, hd), F32),
                        pltpu.VMEM((nh, 1, hd), F32), pltpu.VMEM((nh, 1, LANES), F32)],
        compiler_params=_params(("arbitrary", "arbitrary"), 48),
    )(h_f32, h_f32, h_f32, misc, gate_bias, conv_w)


def rel_bucket(dist):
    max_exact = REL_BUCKETS // 2
    n = jnp.maximum(dist, 0)
    large = max_exact + (jnp.log(jnp.maximum(n, 1).astype(F32) / max_exact)
                         / math.log(REL_MAX_DIST / max_exact)
                         * (REL_BUCKETS - max_exact)).astype(jnp.int32)
    large = jnp.minimum(large, REL_BUCKETS - 1)
    return jnp.where(n < max_exact, n, large)


def bias_tiles(rel, t):
    assert t >= REL_MAX_DIST
    kk = jnp.arange(t)[:, None]
    qq = jnp.arange(t)[None, :]

    def lookup(dist):
        onehot = (rel_bucket(dist)[..., None] == jnp.arange(REL_BUCKETS)).astype(F32)
        return jnp.einsum('kqb,bh->hkq', onehot, rel.astype(F32), precision=lax.Precision.HIGHEST) * LOG2E

    t0 = jnp.where((qq >= kk)[None], lookup(qq - kk), NEG)
    t1 = lookup(t + qq - kk)
    t2 = lookup(jnp.full((t, t), 2 * t))
    return jnp.stack([t0, t1, t2])


def tri_pairs(nq):
    qi = [q for q in range(nq) for _ in range(q + 1)]
    kj = [k for q in range(nq) for k in range(q + 1)]
    return jnp.asarray(qi, jnp.int32), jnp.asarray(kj, jnp.int32)


def values_t_ext(h_bf16, col0, bsz, seq, heads, dv):
    v = h_bf16[:, col0:col0 + heads * dv].reshape(bsz, seq, heads, dv)
    vt = jnp.transpose(v, (0, 2, 3, 1))
    ones = jnp.ones((bsz, heads, V_ONES, seq), BF16)
    return jnp.concatenate([vt, ones], axis=2).reshape(bsz * heads * (dv + V_ONES), seq)


def _attn_maps(qi, kj, maps, bias_ref, mb_ref, s_ref, p_ref, m_ref, acc_ref):
    def run(const_bias):
        def qk(i):
            load_k, load_q, _, head = maps[i]
            s = lax.dot_general(load_k(), load_q(), NT_DIMS, preferred_element_type=F32)
            if not const_bias:
                s = s + bias_ref[0, head]
            if mb_ref is not None:
                s = s + mb_ref[...]
            s_ref[i % 2] = s

        qk(0)
        for i, (_, _, load_vt, head) in enumerate(maps):
            if i + 1 < len(maps):
                qk(i + 1)
            _softmax_pv(i, load_vt, bias_ref, head, mb_ref, const_bias, s_ref, p_ref, m_ref, acc_ref)

    far = qi - kj >= 2

    @pl.when(far)
    def _():
        run(True)

    @pl.when(jnp.logical_not(far))
    def _():
        run(False)


def _softmax_pv(idx, load_vt, bias_ref, head, mb_ref, const_bias, s_ref, p_ref, m_ref, acc_ref):
    tk, tq = s_ref.shape[1:]
    sb = s_ref.at[idx % 2]
    pb = p_ref.at[idx % 2]

    def logits(r0, n):
        return sb[r0:r0 + n, :]

    mx = [jnp.full((8, tq), -jnp.inf, F32) for _ in range(4)]
    for c in range(tk // 8):
        mx[c % 4] = jnp.maximum(mx[c % 4], logits(c * 8, 8))
    m_tile = jnp.max(jnp.maximum(jnp.maximum(mx[0], mx[1]), jnp.maximum(mx[2], mx[3])), axis=0, keepdims=True)
    m_prev = m_ref[idx]
    if const_bias:
        bias_c = bias_ref[0, head, 0:1, :]
        m_new = jnp.maximum(m_prev, m_tile + bias_c)
        m_b = jnp.broadcast_to(m_new - bias_c, (16, tq))
    else:
        m_new = jnp.maximum(m_prev, m_tile)
        m_b = jnp.broadcast_to(m_new, (16, tq))
    m_ref[idx] = m_new
    for c in range(tk // 16):
        pb[c * 16:(c + 1) * 16, :] = jnp.exp2((logits(c * 16, 16) - m_b).astype(BF16))
    alpha = jnp.exp2(m_prev - m_new)
    acc_ref[idx] = alpha * acc_ref[idx] + jnp.dot(load_vt(), pb[...], preferred_element_type=F32)


def _init_softmax_state(m_ref, acc_ref):
    m_ref[...] = jnp.full(m_ref.shape, NEG, F32)
    acc_ref[...] = jnp.zeros_like(acc_ref)


def _diff_attn_kernel(qi_ref, kj_ref, q_ref, k_ref, vt_ref, bias_ref, lam_ref, gb_ref, o_ref,
                      s_ref, p_ref, m_ref, acc_ref, *, out_scale):
    p = pl.program_id(1)
    qi = qi_ref[p]
    kj = kj_ref[p]
    dq = DIFF_QK_DIM
    dv = DIFF_V_DIM
    vrows = dv + V_ONES

    @pl.when(kj == 0)
    def _():
        _init_softmax_state(m_ref, acc_ref)

    def load_q(h, mi):
        qh = q_ref[:, h * 2 * dq:(h + 1) * 2 * dq]
        lane = lax.broadcasted_iota(jnp.int32, qh.shape, 1)
        return jnp.where((lane < dq) if mi == 0 else (lane >= dq), qh, jnp.zeros_like(qh))

    maps = [(lambda h=h: k_ref[:, h * 2 * dq:(h + 1) * 2 * dq],
             functools.partial(load_q, h, mi),
             lambda h=h: vt_ref[h * vrows:(h + 1) * vrows, :], h)
            for h in range(DIFF_HEADS) for mi in range(2)]
    _attn_maps(qi, kj, maps, bias_ref, None, s_ref, p_ref, m_ref, acc_ref)

    @pl.when(kj == qi)
    def _():
        lam = lam_ref[0:1, 0:1]
        for h in range(DIFF_HEADS):
            a1 = acc_ref[2 * h]
            a2 = acc_ref[2 * h + 1]
            a = a1[0:dv] / a1[dv:dv + 1] - lam * (a2[0:dv] / a2[dv:dv + 1])
            y = a * lax.rsqrt(jnp.mean(a * a, axis=0, keepdims=True) + EPS) * gb_ref[...] * out_scale
            o_ref[:, h * dv:(h + 1) * dv] = y.T.astype(o_ref.dtype)


def diff_attention(h_bf16, vt_ext, bsz, seq, q_blk, k_blk, bias, lam_row, subln_g, out_scale, t=512):
    nq = seq // t
    qi, kj = tri_pairs(nq)
    w = DIFF_HEADS * DIFF_V_DIM
    nmap = 2 * DIFF_HEADS
    vrows = DIFF_HEADS * (DIFF_V_DIM + V_ONES)
    g_b = jnp.broadcast_to(subln_g.reshape(DIFF_V_DIM, 1), (DIFF_V_DIM, t))
    grid_spec = pltpu.PrefetchScalarGridSpec(
        num_scalar_prefetch=2,
        grid=(bsz, qi.shape[0]),
        in_specs=[pl.BlockSpec((t, w), lambda b, p, qi, kj: (b * nq + qi[p], q_blk)),
                  pl.BlockSpec((t, w), lambda b, p, qi, kj: (b * nq + kj[p], k_blk)),
                  pl.BlockSpec((vrows, t), lambda b, p, qi, kj: (b, kj[p])),
                  pl.BlockSpec((1, DIFF_HEADS, t, t),
                               lambda b, p, qi, kj: (jnp.minimum(qi[p] - kj[p], 2), 0, 0, 0)),
                  pl.BlockSpec((1, LANES), lambda b, p, qi, kj: (0, 0)),
                  pl.BlockSpec((DIFF_V_DIM, t), lambda b, p, qi, kj: (0, 0))],
        out_specs=pl.BlockSpec((t, w), lambda b, p, qi, kj: (b * nq + qi[p], 0)),
        scratch_shapes=[pltpu.VMEM((2, t, t), F32), pltpu.VMEM((2, t, t), BF16),
                        pltpu.VMEM((nmap, 1, t), F32), pltpu.VMEM((nmap, DIFF_V_DIM + V_ONES, t), F32)],
    )
    return pl.pallas_call(
        functools.partial(_diff_attn_kernel, out_scale=out_scale),
        grid_spec=grid_spec,
        out_shape=jax.ShapeDtypeStruct((bsz * seq, w), BF16),
        compiler_params=_params(("arbitrary", "arbitrary"), 48),
    )(qi, kj, h_bf16, h_bf16, vt_ext, bias, lam_row, g_b)


def _sort_key(x):
    b = lax.bitcast_convert_type(x, jnp.int32)
    return b ^ ((b >> 31) & 0x7FFFFFFF)


def _batcher_pairs(n):
    def merge(lo, hi, r):
        step = r * 2
        if step < hi - lo:
            yield from merge(lo, hi, step)
            yield from merge(lo + r, hi, step)
            yield from [(a, a + r) for a in range(lo + r, hi - r, step)]
        else:
            yield (lo, lo + r)

    def sort(lo, hi):
        if hi - lo >= 1:
            mid = lo + (hi - lo) // 2
            yield from sort(lo, mid)
            yield from sort(mid + 1, hi)
            yield from merge(lo, hi, 1)

    return tuple(sort(0, n - 1))


SORT_GROUP = 16


def _count_ge_sorted(v, cand):
    g = len(v)
    out, masks, level = [], [], 1
    while (1 << level) <= g:
        pivots = [v[(2 * k + 1) * g // (1 << level) - 1] for k in range(1 << (level - 1))]

        def pick(lo, count, bit):
            if count == 1:
                return pivots[lo]
            half = count // 2
            return jnp.where(masks[bit], pick(lo, half, bit + 1), pick(lo + half, half, bit + 1))

        masks.append(pick(0, len(pivots), 0) >= cand)
        out.append((masks[-1], float(g >> level)))
        level += 1
    out.append((v[g - 1] >= cand, 1.0))
    return out


def _dsa_select_kernel(iq_ref, w_ref, ikz_ref, o_ref, key_ref, srt_ref, a_ref, wb_ref, *, tq, tk, topk, seq):
    i = pl.program_id(1)
    last = (i * tq + tq - 1) // tk
    idx_scale = (IDX_DIM * IDX_HEADS) ** -0.5
    iw_row = 64
    grp = SORT_GROUP
    pairs = _batcher_pairs(grp)

    wt = w_ref[...].T
    for h in range(IDX_HEADS):
        wb_ref[h] = jnp.broadcast_to(wt[iw_row + h:iw_row + h + 1, :], (8, tq))

    def in_prefix(off, n):
        key_pos = off + lax.broadcasted_iota(jnp.int32, (n, tq), 0)
        q_pos = i * tq + lax.broadcasted_iota(jnp.int32, (n, tq), 1)
        return key_pos <= q_pos

    q_stack = [jnp.concatenate([iq_ref[:, (2 * pg) * LANES:(2 * pg + 1) * LANES],
                                iq_ref[:, (2 * pg + 1) * LANES:(2 * pg + 2) * LANES]], axis=0) for pg in range(2)]

    def head_dots(j, slot):
        off = pl.multiple_of(j * tk, tk)
        for half in range(2):
            kt = ikz_ref[pl.ds(off, tk), half * LANES:(half + 1) * LANES]
            for pg in range(2):
                a_ref[slot, 2 * half + pg] = lax.dot_general(kt, q_stack[pg], NT_DIMS, preferred_element_type=F32)

    def score_tile(j, slot, diagonal):
        off = pl.multiple_of(j * tk, tk)
        for c in range(tk // 8):
            r0 = c * 8
            sc = None
            for half in range(2):
                for pg in range(2):
                    for s in range(2):
                        head = 2 * (2 * pg + s) + half
                        term = wb_ref[head] * jnp.maximum(
                            a_ref[slot, 2 * half + pg, r0:r0 + 8, s * tq:(s + 1) * tq], 0.0)
                        sc = term if sc is None else sc + term
            sc = sc * idx_scale
            if diagonal:
                sc = jnp.where(in_prefix(off + r0, 8), sc, -jnp.inf)
            key_ref[pl.ds(pl.multiple_of(off + r0, 8), 8), :] = _sort_key(sc)
        for g in range(tk // (8 * grp)):
            base = off + g * grp * 8
            keys = [key_ref[pl.ds(pl.multiple_of(base + c * 8, 8), 8), :] for c in range(grp)]
            for lo, hi in pairs:
                keys[lo], keys[hi] = jnp.minimum(keys[lo], keys[hi]), jnp.maximum(keys[lo], keys[hi])
            for c in range(grp):
                srt_ref[pl.ds(pl.multiple_of(base + c * 8, 8), 8), :] = keys[c]

    def score_body(p, carry):
        head_dots(2 * p + 1, 1)
        score_tile(2 * p, 0, False)
        head_dots(2 * p + 2, 0)
        score_tile(2 * p + 1, 1, False)
        return carry

    head_dots(0, 0)
    lax.fori_loop(0, last // 2, score_body, 0)

    @pl.when(last % 2 == 0)
    def _():
        score_tile(last, 0, True)

    @pl.when(last % 2 == 1)
    def _():
        head_dots(last, 1)
        score_tile(last - 1, 0, False)
        score_tile(last, 1, True)

    n_pairs = (last + 2) // 2

    @pl.when(n_pairs * 2 > last + 1)
    def _():
        srt_ref[pl.ds(pl.multiple_of((last + 1) * tk, tk), tk), :] = jnp.full((tk, tq), INT_MIN, jnp.int32)

    n_terms = grp.bit_length()

    def bit_body(it, thr):
        cand = thr + lax.shift_left(jnp.int32(1), 31 - it)
        cand_b = jnp.broadcast_to(cand, (8, tq))

        def count_body(j, accs):
            blk = srt_ref[pl.ds(pl.multiple_of(j * 2 * tk, 2 * tk), 2 * tk), :]
            accs = list(accs)
            for g in range(2 * tk // (8 * grp)):
                run = [blk[(g * grp + c) * 8:(g * grp + c + 1) * 8, :] for c in range(grp)]
                for t, (mask, _) in enumerate(_count_ge_sorted(run, cand_b)):
                    accs[t] = accs[t] + jnp.where(mask, 1.0, 0.0)
            return tuple(accs)

        zero = jnp.zeros((8, tq), F32)
        accs = lax.fori_loop(0, n_pairs, count_body, (zero,) * n_terms)
        weights = [float(grp >> (t + 1)) for t in range(n_terms - 1)] + [1.0]
        total = accs[0] * weights[0]
        for t in range(1, n_terms):
            total = total + accs[t] * weights[t]
        cnt = jnp.sum(total, axis=0, keepdims=True)
        return jnp.where(cnt >= topk, cand, thr)

    thr = lax.fori_loop(0, 32, bit_body, jnp.full((1, tq), INT_MIN, jnp.int32))
    thr_b = jnp.broadcast_to(thr, (16, tq))

    def out_tile(j, diagonal):
        off = pl.multiple_of(j * tk, tk)
        for c in range(tk // 16):
            rows = pl.ds(pl.multiple_of(off + c * 16, 16), 16)
            keep = key_ref[rows, :] >= thr_b
            if diagonal:
                keep = keep & in_prefix(off + c * 16, 16)
            o_ref[0, rows, :] = jnp.where(keep, 0.0, NEG).astype(o_ref.dtype)

    def out_body(j, carry):
        out_tile(j, False)
        return carry

    lax.fori_loop(0, last, out_body, 0)
    out_tile(last, True)

    def fill_body(j, carry):
        o_ref[0, pl.ds(pl.multiple_of(j * tk, tk), tk), :] = jnp.full((tk, tq), NEG, o_ref.dtype)
        return carry

    lax.fori_loop(last + 1, seq // tk, fill_body, 0)


def dsa_select(h_bf16, misc, ikz, bsz, seq, iq_blk, tq=128, tk=512):
    nq = seq // tq
    topk = min(TOPK_MAX, seq // 4)
    w = IDX_HEADS * IDX_DIM
    return pl.pallas_call(
        functools.partial(_dsa_select_kernel, tq=tq, tk=tk, topk=topk, seq=seq),
        grid=(bsz, nq),
        in_specs=[pl.BlockSpec((tq, w), lambda b, i: (b * nq + i, iq_blk)),
                  pl.BlockSpec((tq, LANES), lambda b, i: (b * nq + i, 0)),
                  pl.BlockSpec((seq, 2 * LANES), lambda b, i: (b, 0), pipeline_mode=pl.Buffered(1))],
        out_specs=pl.BlockSpec((1, seq, tq), lambda b, i: (b, 0, i)),
        out_shape=jax.ShapeDtypeStruct((bsz, seq, seq), BF16),
        scratch_shapes=[pltpu.VMEM((seq, tq), jnp.int32), pltpu.VMEM((seq, tq), jnp.int32),
                        pltpu.VMEM((2, IDX_HEADS // 2, tk, 2 * tq), F32), pltpu.VMEM((IDX_HEADS, 8, tq), F32)],
        compiler_params=_params(("arbitrary", "arbitrary"), 48),
    )(h_bf16, misc, ikz)


def _dsa_attn_kernel(qi_ref, kj_ref, q_ref, k_ref, vt_ref, bias_ref, mask_ref, o_ref,
                     s_ref, p_ref, mb_ref, m_ref, acc_ref):
    p = pl.program_id(1)
    qi = qi_ref[p]
    kj = kj_ref[p]
    dh = HEAD_DIM
    vrows = dh + V_ONES

    @pl.when(kj == 0)
    def _():
        _init_softmax_state(m_ref, acc_ref)

    mb_ref[...] = mask_ref[0].astype(F32)
    maps = [(lambda h=h: k_ref[:, h * dh:(h + 1) * dh], lambda h=h: q_ref[:, h * dh:(h + 1) * dh],
             lambda h=h: vt_ref[h * vrows:(h + 1) * vrows, :], h) for h in range(DSA_HEADS)]
    _attn_maps(qi, kj, maps, bias_ref, mb_ref, s_ref, p_ref, m_ref, acc_ref)

    @pl.when(kj == qi)
    def _():
        for h in range(DSA_HEADS):
            a = acc_ref[h]
            o_ref[:, h * dh:(h + 1) * dh] = (a[0:dh] / a[dh:dh + 1]).T.astype(o_ref.dtype)


def dsa_attention(h_bf16, vt_ext, mask, bsz, seq, q_blk, k_blk, bias, t=512):
    nq = seq // t
    qi, kj = tri_pairs(nq)
    w = DSA_HEADS * HEAD_DIM
    vrows = DSA_HEADS * (HEAD_DIM + V_ONES)
    grid_spec = pltpu.PrefetchScalarGridSpec(
        num_scalar_prefetch=2,
        grid=(bsz, qi.shape[0]),
        in_specs=[pl.BlockSpec((t, w), lambda b, p, qi, kj: (b * nq + qi[p], q_blk)),
                  pl.BlockSpec((t, w), lambda b, p, qi, kj: (b * nq + kj[p], k_blk)),
                  pl.BlockSpec((vrows, t), lambda b, p, qi, kj: (b, kj[p])),
                  pl.BlockSpec((1, DSA_HEADS, t, t),
                               lambda b, p, qi, kj: (jnp.minimum(qi[p] - kj[p], 2), 0, 0, 0)),
                  pl.BlockSpec((1, t, t), lambda b, p, qi, kj: (b, kj[p], qi[p]))],
        out_specs=pl.BlockSpec((t, w), lambda b, p, qi, kj: (b * nq + qi[p], 0)),
        scratch_shapes=[pltpu.VMEM((2, t, t), F32), pltpu.VMEM((2, t, t), BF16), pltpu.VMEM((t, t), F32),
                        pltpu.VMEM((DSA_HEADS, 1, t), F32), pltpu.VMEM((DSA_HEADS, HEAD_DIM + V_ONES, t), F32)],
    )
    return pl.pallas_call(
        _dsa_attn_kernel,
        grid_spec=grid_spec,
        out_shape=jax.ShapeDtypeStruct((bsz * seq, w), BF16),
        compiler_params=_params(("arbitrary", "arbitrary"), 48),
    )(qi, kj, h_bf16, h_bf16, vt_ext, bias, mask)


_SPLITS = (512, 512, 512, 512, 64, 8, 512, 512, 512, 512, 1024, 512, 512, 4, 4)
_NAMES = ('a_q', 'a_k', 'a_v', 'a_iq', 'a_ik', 'a_iw', 'b_u', 'c_q', 'c_k', 'c_v', 'd_qk', 'd_v', 'd_o', 'd_i', 'd_f')


def _split_w_in(w_in):
    out, off = {}, 0
    for name, width in zip(_NAMES, _SPLITS):
        out[name] = w_in[:, off:off + width]
        off += width
    return out


def _group_w_in(w_in):
    c = _split_w_in(w_in)
    d = w_in.shape[0]
    z = lambda n: jnp.zeros((d, n), w_in.dtype)
    w_attn = jnp.concatenate([c['a_q'], c['a_k'], c['a_v'], c['a_iq'], c['c_q'], c['c_k'], c['c_v']], axis=1)
    w_scan = jnp.concatenate([c['d_qk'], c['b_u'], c['d_v'], c['d_o']], axis=1)
    w_misc = jnp.concatenate([c['a_ik'], z(64), z(64), c['a_ik'],
                              z(64), c['a_iw'], c['d_i'], c['d_f'], z(48)], axis=1)
    return w_attn.astype(BF16), w_scan.astype(BF16), w_misc.astype(BF16)


def _pad_ffn(w_in, w_out):
    d = w_in.shape[0]
    pad = D_FF_PAD - D_FF
    zi = jnp.zeros((d, pad), w_in.dtype)
    w_in_pad = jnp.concatenate([w_in[:, :D_FF], zi, w_in[:, D_FF:], zi], axis=1).astype(BF16)
    w_out_pad = jnp.concatenate([w_out, jnp.zeros((pad, d), w_out.dtype)], axis=0).astype(BF16)
    return w_in_pad, w_out_pad


def _branches(xn, l, bsz, seq, w_in, conv_w, i_bias, f_bias, s5, s5_d, s5_w_glu, diff_lambda,
              diff_subln_g, bias_a, bias_c):
    lambda_init = 0.8 - 0.6 * math.exp(-0.3 * l)
    w_attn, w_scan, w_misc = _group_w_in(w_in)
    ones = jnp.ones((512,), F32)
    attn_scale = jnp.concatenate([ones * (HEAD_DIM ** -0.5 * LOG2E), ones, ones, ones,
                                  ones * (DIFF_QK_DIM ** -0.5 * LOG2E), ones, ones]).reshape(1, -1)
    h_attn = matmul(xn, w_attn, attn_scale, BF16)
    h_scan = matmul(xn, w_scan, jnp.ones((1, w_scan.shape[1]), F32), F32)
    ikz, misc = misc_proj(xn, w_misc)

    mask = dsa_select(h_attn, misc, ikz, bsz, seq, iq_blk=3)
    vt_a = values_t_ext(h_attn, 2 * 512, bsz, seq, DSA_HEADS, HEAD_DIM)
    y_a = dsa_attention(h_attn, vt_a, mask, bsz, seq, 0, 1, bias_a)

    bbd, abar_re, abar_im, cbd = s5
    y_b = s5_branch(h_scan, 2, bsz, seq, bbd, abar_re, abar_im, cbd, s5_d.reshape(1, -1), s5_w_glu.astype(BF16))

    dl = diff_lambda.astype(F32)
    lam = jnp.exp(jnp.sum(dl[0] * dl[1])) - jnp.exp(jnp.sum(dl[2] * dl[3])) + lambda_init
    lam_row = jnp.full((1, LANES), lam, F32)
    vt_c = values_t_ext(h_attn, 6 * 512, bsz, seq, DIFF_HEADS, DIFF_V_DIM)
    y_c = diff_attention(h_attn, vt_c, bsz, seq, 4, 5, bias_c, lam_row, diff_subln_g, 1.0 - lambda_init)

    gate_bias = jnp.zeros((1, LANES), F32).at[0, 72:76].set(i_bias).at[0, 76:80].set(f_bias)
    y_d = mlstm_branch(h_scan, misc, gate_bias, conv_w, bsz, seq, qk_blk=0, v_blk=3, o_blk=4)
    return y_a, y_b, y_c, y_d


def _mixer(x, l, bsz, seq, norm_g, w_in, conv_w, i_bias, f_bias, s5, s5_d, s5_w_glu, diff_lambda,
           diff_subln_g, bias_a, bias_c, w_gate, b_gate, w_branch, w_out):
    xn = rmsnorm(x, norm_g, BF16)
    ys = _branches(xn, l, bsz, seq, w_in, conv_w, i_bias, f_bias, s5, s5_d, s5_w_glu, diff_lambda,
                   diff_subln_g, bias_a, bias_c)
    merged = gated_merge(xn, ys, w_gate.astype(BF16), b_gate, w_branch.astype(BF16))
    return matmul_residual(merged, w_out.astype(BF16), x)


def kernel(x, norm_g, w_ffn_in, w_ffn_out, w_in, conv_w, mlstm_i_bias, mlstm_f_bias, s5_a_re, s5_a_im,
           s5_log_dt, s5_b_re, s5_b_im, s5_c_re, s5_c_im, s5_d, s5_w_glu, diff_lambda, diff_subln_g,
           rel_table, w_gate, b_gate, w_branch, w_out, final_g):
    bsz, seq, d = x.shape
    t_attn = 512
    bias_a = bias_tiles(rel_table[:, :DSA_HEADS], t_attn)
    bias_c = bias_tiles(rel_table[:, DSA_HEADS:], t_attn)
    xf = x.reshape(bsz * seq, d)
    for l in range(DEPTH):
        xf = ffn_block(xf, norm_g[l, 0], *_pad_ffn(w_ffn_in[l, 0], w_ffn_out[l, 0]))
        s5 = s5_tables(s5_a_re[l], s5_a_im[l], s5_log_dt[l], s5_b_re[l], s5_b_im[l], s5_c_re[l], s5_c_im[l])
        xf = _mixer(xf, l, bsz, seq, norm_g[l, 1], w_in[l], conv_w[l], mlstm_i_bias[l], mlstm_f_bias[l],
                    s5, s5_d[l], s5_w_glu[l], diff_lambda[l], diff_subln_g[l], bias_a, bias_c,
                    w_gate[l], b_gate[l], w_branch[l], w_out[l])
        xf = ffn_block(xf, norm_g[l, 2], *_pad_ffn(w_ffn_in[l, 1], w_ffn_out[l, 1]))
    return rmsnorm(xf, final_g, F32).reshape(bsz, seq, d)
```

```python
import functools
import math

import jax
import jax.numpy as jnp
from jax import lax
from jax.experimental import pallas as pl
from jax.experimental.pallas import tpu as pltpu

F32 = jnp.float32
BF16 = jnp.bfloat16

D_MODEL = 2048
DEPTH = 4
N_BRANCH = 4
BRANCH_WIDTH = 512
HEAD_DIM = 128
DSA_HEADS = 4
IDX_HEADS = 8
IDX_DIM = 64
TOPK_MAX = 256
S5_WIDTH = 512
S5_GROUP = 16
S5_GROUPS = S5_WIDTH // S5_GROUP
S5_STATE = 64
DIFF_HEADS = 4
DIFF_QK_DIM = 64
DIFF_V_DIM = 128
MLSTM_HEADS = 4
MLSTM_DIM = 128
CONV_WIDTH = 4
D_FF = 5504
REL_BUCKETS = 32
REL_MAX_DIST = 128
EPS = 1e-6

LANES = 128
NEG = -1e30
INT_MIN = -(2 ** 31)
D_FF_PAD = 5632
MLSTM_L = 128

V_ONES = 16
LOG2E = math.log2(math.e)

NT_DIMS = (((1,), (1,)), ((), ()))


def _params(sem, vmem_mb):
    return pltpu.CompilerParams(dimension_semantics=sem, vmem_limit_bytes=vmem_mb * 1024 * 1024)


def _rms(x, g):
    return x * lax.rsqrt(jnp.mean(x * x, axis=-1, keepdims=True) + EPS) * g


def _rmsnorm_kernel(x_ref, g_ref, o_ref):
    o_ref[...] = _rms(x_ref[...], g_ref[...]).astype(o_ref.dtype)


def rmsnorm(x, g, out_dtype, tm=512):
    m, d = x.shape
    return pl.pallas_call(
        _rmsnorm_kernel,
        grid=(m // tm,),
        in_specs=[pl.BlockSpec((tm, d), lambda i: (i, 0)), pl.BlockSpec((1, d), lambda i: (0, 0))],
        out_specs=pl.BlockSpec((tm, d), lambda i: (i, 0)),
        out_shape=jax.ShapeDtypeStruct((m, d), out_dtype),
        compiler_params=_params(("parallel",), 40),
    )(x, g.reshape(1, d))


def _mm_kernel(x_ref, w_ref, cs_ref, o_ref):
    acc = jnp.dot(x_ref[...], w_ref[...], preferred_element_type=F32)
    o_ref[...] = (acc * cs_ref[...]).astype(o_ref.dtype)


def matmul(x, w, col_scale, out_dtype, tm=1024, tn=512):
    m, k = x.shape
    n = w.shape[1]
    return pl.pallas_call(
        _mm_kernel,
        grid=(m // tm, n // tn),
        in_specs=[pl.BlockSpec((tm, k), lambda i, j: (i, 0)), pl.BlockSpec((k, tn), lambda i, j: (0, j)),
                  pl.BlockSpec((1, tn), lambda i, j: (0, j))],
        out_specs=pl.BlockSpec((tm, tn), lambda i, j: (i, j)),
        out_shape=jax.ShapeDtypeStruct((m, n), out_dtype),
        compiler_params=_params(("parallel", "arbitrary"), 40),
    )(x, w, col_scale)


def _norm_mm_kernel(x_ref, g_ref, w_ref, cs_ref, xn_ref, o_ref):
    @pl.when(pl.program_id(1) == 0)
    def _():
        xn_ref[...] = _rms(x_ref[...], g_ref[...]).astype(xn_ref.dtype)

    acc = jnp.dot(xn_ref[...], w_ref[...], preferred_element_type=F32)
    o_ref[...] = (acc * cs_ref[...]).astype(o_ref.dtype)


def norm_matmul(x, g, w, col_scale, out_dtype, tm=1024, tn=512):
    m, k = x.shape
    n = w.shape[1]
    return pl.pallas_call(
        _norm_mm_kernel,
        grid=(m // tm, n // tn),
        in_specs=[pl.BlockSpec((tm, k), lambda i, j: (i, 0)), pl.BlockSpec((1, k), lambda i, j: (0, 0)),
                  pl.BlockSpec((k, tn), lambda i, j: (0, j)), pl.BlockSpec((1, tn), lambda i, j: (0, j))],
        out_specs=[pl.BlockSpec((tm, k), lambda i, j: (i, 0)), pl.BlockSpec((tm, tn), lambda i, j: (i, j))],
        out_shape=[jax.ShapeDtypeStruct((m, k), BF16), jax.ShapeDtypeStruct((m, n), out_dtype)],
        compiler_params=_params(("parallel", "arbitrary"), 48),
    )(x, g.reshape(1, k), w, col_scale)


def _mm_res_kernel(x_ref, w_ref, r_ref, o_ref):
    o_ref[...] = r_ref[...] + jnp.dot(x_ref[...], w_ref[...], preferred_element_type=F32)


def matmul_residual(x, w, r, tm=1024, tn=512):
    m, k = x.shape
    n = w.shape[1]
    return pl.pallas_call(
        _mm_res_kernel,
        grid=(m // tm, n // tn),
        in_specs=[pl.BlockSpec((tm, k), lambda i, j: (i, 0)), pl.BlockSpec((k, tn), lambda i, j: (0, j)),
                  pl.BlockSpec((tm, tn), lambda i, j: (i, j))],
        out_specs=pl.BlockSpec((tm, tn), lambda i, j: (i, j)),
        out_shape=jax.ShapeDtypeStruct((m, n), F32),
        compiler_params=_params(("parallel", "arbitrary"), 40),
    )(x, w, r)


def _misc_proj_kernel(x_ref, w_ref, ikz_ref, misc_ref):
    acc = jnp.dot(x_ref[...], w_ref[...], preferred_element_type=F32)
    ikz_ref[...] = acc[:, :2 * LANES].astype(BF16)
    misc_ref[...] = acc[:, 2 * LANES:]


def misc_proj(xn, w, tm=1024):
    m, k = xn.shape
    n = w.shape[1]
    return pl.pallas_call(
        _misc_proj_kernel,
        grid=(m // tm,),
        in_specs=[pl.BlockSpec((tm, k), lambda i: (i, 0)), pl.BlockSpec((k, n), lambda i: (0, 0))],
        out_specs=[pl.BlockSpec((tm, 2 * LANES), lambda i: (i, 0)), pl.BlockSpec((tm, LANES), lambda i: (i, 0))],
        out_shape=[jax.ShapeDtypeStruct((m, 2 * LANES), BF16), jax.ShapeDtypeStruct((m, LANES), F32)],
        compiler_params=_params(("parallel",), 40),
    )(xn, w)


def _ffn_kernel(x_ref, g_ref, wg_ref, wu_ref, wo_ref, o_ref, xn_ref):
    j = pl.program_id(1)

    @pl.when(j == 0)
    def _():
        x = x_ref[...]
        xn_ref[...] = _rms(x, g_ref[...]).astype(BF16)
        o_ref[...] = x

    xn = xn_ref[...]
    g = jnp.dot(xn, wg_ref[...], preferred_element_type=F32)
    u = jnp.dot(xn, wu_ref[...], preferred_element_type=F32)
    a = (g * jax.nn.sigmoid(g) * u * 0.5).astype(BF16)
    o_ref[...] += jnp.dot(a, wo_ref[...], preferred_element_type=F32)


def ffn_block(x, g, w_in_pad, w_out_pad, tm=1024, tf=512):
    m, d = x.shape
    nf = w_out_pad.shape[0] // tf
    return pl.pallas_call(
        _ffn_kernel,
        grid=(m // tm, nf),
        in_specs=[pl.BlockSpec((tm, d), lambda i, j: (i, 0)),
                  pl.BlockSpec((1, d), lambda i, j: (0, 0)),
                  pl.BlockSpec((d, tf), lambda i, j: (0, j)),
                  pl.BlockSpec((d, tf), lambda i, j: (0, j + nf)),
                  pl.BlockSpec((tf, d), lambda i, j: (j, 0))],
        out_specs=pl.BlockSpec((tm, d), lambda i, j: (i, 0)),
        out_shape=jax.ShapeDtypeStruct((m, d), F32),
        scratch_shapes=[pltpu.VMEM((tm, d), BF16)],
        compiler_params=_params(("parallel", "arbitrary"), 56),
    )(x, g.reshape(1, d), w_in_pad, w_in_pad, w_out_pad)


def _merge_kernel(xn_ref, ya_ref, yb_ref, yc_ref, yd_ref, wg_ref, bg_ref, wb_ref, o_ref):
    xn = xn_ref[...]
    acc = None
    for n, y_ref in enumerate((ya_ref, yb_ref, yc_ref, yd_ref)):
        gate = jnp.dot(xn, wg_ref[n], preferred_element_type=F32) + bg_ref[n]
        proj = jnp.dot(y_ref[...], wb_ref[n], preferred_element_type=F32)
        term = jax.nn.sigmoid(gate) * proj
        acc = term if acc is None else acc + term
    o_ref[...] = acc.astype(o_ref.dtype)


def gated_merge(xn, ys, w_gate, b_gate, w_branch, tm=1024, tn=256):
    m, d = xn.shape
    bw = ys[0].shape[1]
    y_spec = pl.BlockSpec((tm, bw), lambda i, j: (i, 0))
    return pl.pallas_call(
        _merge_kernel,
        grid=(m // tm, d // tn),
        in_specs=[pl.BlockSpec((tm, d), lambda i, j: (i, 0)), y_spec, y_spec, y_spec, y_spec,
                  pl.BlockSpec((N_BRANCH, d, tn), lambda i, j: (0, 0, j)),
                  pl.BlockSpec((N_BRANCH, 1, tn), lambda i, j: (0, 0, j)),
                  pl.BlockSpec((N_BRANCH, bw, tn), lambda i, j: (0, 0, j))],
        out_specs=pl.BlockSpec((tm, tn), lambda i, j: (i, j)),
        out_shape=jax.ShapeDtypeStruct((m, d), BF16),
        compiler_params=_params(("parallel", "arbitrary"), 48),
    )(xn, *ys, w_gate, b_gate.reshape(N_BRANCH, 1, d), w_branch)


def _s5_kernel(u_ref, perm_ref, permt_ref, bbd_ref, pwr_ref, pwi_ref, cbd_ref, d_ref, wglu_ref, o_ref,
               x_ref, st_ref, *, tt):
    n = S5_GROUPS * S5_STATE
    sl = tt // 8
    lc = 512

    @pl.when(pl.program_id(1) == 0)
    def _():
        st_ref[...] = jnp.zeros_like(st_ref)

    u = u_ref[...]
    u_perm = jnp.dot(perm_ref[...], u.astype(BF16), preferred_element_type=F32).astype(BF16)
    x_ref[...] = jnp.dot(u_perm, bbd_ref[...], preferred_element_type=F32)

    for c0 in range(0, n, lc):
        re = slice(c0, c0 + lc)
        im = slice(n + c0, n + c0 + lc)
        ar = pwr_ref[0:1, re]
        ai = pwi_ref[0:1, re]
        xr = jnp.zeros((8, lc), F32)
        xi = jnp.zeros((8, lc), F32)
        for j in range(sl):
            rows = slice(8 * j, 8 * j + 8)
            nr = ar * xr - ai * xi + x_ref[rows, re]
            ni = ar * xi + ai * xr + x_ref[rows, im]
            xr, xi = nr, ni
            x_ref[rows, re] = xr
            x_ref[rows, im] = xi

        alr = pwr_ref[sl - 1:sl, re]
        ali = pwi_ref[sl - 1:sl, re]
        cr = [st_ref[0:1, re]]
        ci = [st_ref[1:2, re]]
        for s in range(1, 9):
            pr, pi = cr[-1], ci[-1]
            cr.append(xr[s - 1:s] + alr * pr - ali * pi)
            ci.append(xi[s - 1:s] + alr * pi + ali * pr)
        st_ref[0:1, re] = cr[8]
        st_ref[1:2, re] = ci[8]
        car = jnp.concatenate(cr[:8], axis=0)
        cai = jnp.concatenate(ci[:8], axis=0)
        for j in range(sl):
            rows = slice(8 * j, 8 * j + 8)
            pr = pwr_ref[j:j + 1, re]
            pi = pwi_ref[j:j + 1, re]
            x_ref[rows, re] = x_ref[rows, re] + (pr * car - pi * cai)
            x_ref[rows, im] = x_ref[rows, im] + (pr * cai + pi * car)

    y_perm = jnp.dot(x_ref[...].astype(BF16), cbd_ref[...], preferred_element_type=F32)
    y = jnp.dot(permt_ref[...], y_perm, precision=lax.Precision.HIGHEST, preferred_element_type=F32) + d_ref[...] * u
    z = jax.nn.gelu(y)
    gate = jnp.dot(z.astype(BF16), wglu_ref[...], preferred_element_type=F32)
    o_ref[...] = (z * jax.nn.sigmoid(gate)).astype(o_ref.dtype)


S5_TILE = 256


def s5_branch(h_f32, col_block, bsz, seq, bbd, pw_re, pw_im, cbd, d_skip, w_glu, tt=S5_TILE):
    n = S5_GROUPS * S5_STATE
    nt = seq // tt
    sl = tt // 8
    row = jnp.arange(tt)
    perm = jax.nn.one_hot((row % 8) * sl + row // 8, tt, dtype=F32)
    const = lambda shape: pl.BlockSpec(shape, lambda b, t: (0, 0))
    return pl.pallas_call(
        functools.partial(_s5_kernel, tt=tt),
        grid=(bsz, nt),
        in_specs=[pl.BlockSpec((tt, S5_WIDTH), lambda b, t: (b * nt + t, col_block)),
                  const((tt, tt)), const((tt, tt)),
                  const((S5_WIDTH, 2 * n)), const((sl, n)), const((sl, n)), const((2 * n, S5_WIDTH)),
                  const((1, S5_WIDTH)), const((S5_WIDTH, S5_WIDTH))],
        out_specs=pl.BlockSpec((tt, S5_WIDTH), lambda b, t: (b * nt + t, 0)),
        out_shape=jax.ShapeDtypeStruct((bsz * seq, S5_WIDTH), BF16),
        scratch_shapes=[pltpu.VMEM((tt, 2 * n), F32), pltpu.VMEM((8, n), F32)],
        compiler_params=_params(("arbitrary", "arbitrary"), 48),
    )(h_f32, perm.astype(BF16), perm.T, bbd, pw_re, pw_im, cbd, d_skip, w_glu)


def s5_tables(a_re, a_im, log_dt, b_re, b_im, c_re, c_im, n_pow=S5_TILE // 8):
    g, p = a_re.shape
    dt = jnp.exp(log_dt)[:, None]
    mag = jnp.exp(a_re * dt)
    ab_re, ab_im = mag * jnp.cos(a_im * dt), mag * jnp.sin(a_im * dt)
    steps = jnp.arange(1, n_pow + 1, dtype=F32)[:, None, None]
    pw_mag = jnp.exp(a_re * dt * steps)
    pw_re = (pw_mag * jnp.cos(a_im * dt * steps)).reshape(n_pow, g * p)
    pw_im = (pw_mag * jnp.sin(a_im * dt * steps)).reshape(n_pow, g * p)
    den = a_re * a_re + a_im * a_im
    nr, ni = ab_re - 1.0, ab_im
    coef_re = (nr * a_re + ni * a_im) / den
    coef_im = (ni * a_re - nr * a_im) / den
    bb_re = coef_re[..., None] * b_re - coef_im[..., None] * b_im
    bb_im = coef_re[..., None] * b_im + coef_im[..., None] * b_re
    eye = jnp.eye(g, dtype=F32)
    to_bd_in = lambda t: jnp.einsum('gpc,gh->gchp', t, eye).reshape(g * S5_GROUP, g * p)
    to_bd_out = lambda t: jnp.einsum('gcp,gh->gphc', t, eye).reshape(g * p, g * S5_GROUP)
    bbd = jnp.concatenate([to_bd_in(bb_re), to_bd_in(bb_im)], axis=1).astype(BF16)
    cbd = jnp.concatenate([to_bd_out(c_re), to_bd_out(-c_im)], axis=0).astype(BF16)
    return bbd, pw_re, pw_im, cbd


def _log_sigmoid(x):
    return jnp.minimum(x, 0.0) - jnp.log1p(jnp.exp(-jnp.abs(x)))


def _mlstm_kernel(qk_ref, v_ref, og_ref, gate_ref, gbias_ref, convw_ref, o_ref,
                  xbuf_ref, c_ref, n_ref, m_ref, *, tt):
    L = MLSTM_L
    hd = MLSTM_DIM
    nh = MLSTM_HEADS
    i_col, f_col = 72, 76

    @pl.when(pl.program_id(1) == 0)
    def _():
        xbuf_ref[0:8, :] = jnp.zeros((8, 2 * nh * hd), F32)
        c_ref[...] = jnp.zeros_like(c_ref)
        n_ref[...] = jnp.zeros_like(n_ref)
        m_ref[...] = jnp.zeros_like(m_ref)

    xbuf_ref[8:8 + tt, :] = qk_ref[...]
    cw = convw_ref[...]
    conv = xbuf_ref[8:8 + tt, :] * cw[CONV_WIDTH - 1:CONV_WIDTH]
    for sh in range(1, CONV_WIDTH):
        conv = conv + xbuf_ref[8 - sh:8 - sh + tt, :] * cw[CONV_WIDTH - 1 - sh:CONV_WIDTH - sh]
    xbuf_ref[0:8, :] = xbuf_ref[tt:tt + 8, :]
    qk = conv * jax.nn.sigmoid(conv)

    gates = gate_ref[...] + gbias_ref[...]
    logf = _log_sigmoid(gates)
    row = lax.broadcasted_iota(jnp.int32, (L, L), 0)
    col = lax.broadcasted_iota(jnp.int32, (L, L), 1)
    causal = col <= row
    tri = jnp.where(causal, 1.0, 0.0).astype(F32)

    for c in range(tt // L):
        r0 = c * L
        gc = gates[r0:r0 + L]
        bcum = jnp.dot(tri, logf[r0:r0 + L], precision=lax.Precision.HIGHEST, preferred_element_type=F32)
        gct = gc.T
        bcumt = bcum.T
        for h in range(nh):
            ig_col = gc[:, i_col + h:i_col + h + 1]
            b_col = bcum[:, f_col + h:f_col + h + 1]
            ig_row = gct[i_col + h:i_col + h + 1, :]
            b_row = bcumt[f_col + h:f_col + h + 1, :]
            b_last = b_col[L - 1:L, :]
            q = qk[r0:r0 + L, h * hd:(h + 1) * hd]
            k = qk[r0:r0 + L, (nh + h) * hd:(nh + h + 1) * hd] * (hd ** -0.5)
            vb = v_ref[r0:r0 + L, h * hd:(h + 1) * hd].astype(BF16)
            qb = q.astype(BF16)
            kb = k.astype(BF16)
            c_prev = c_ref[h]
            n_prev = n_ref[h]
            m_prev = m_ref[h][:, 0:1]

            g = b_last - b_col + ig_col
            m_loc = jnp.max(g, axis=0, keepdims=True)
            wk = jnp.exp(g - m_loc) * k
            c_loc = jnp.dot(wk.T.astype(BF16), vb, preferred_element_type=F32)
            n_loc = jnp.sum(wk, axis=0, keepdims=True)

            m_inter = b_col + m_prev
            log_d = jnp.where(causal, b_col - b_row + ig_row, -jnp.inf)
            m_j = jnp.maximum(jnp.max(log_d, axis=1, keepdims=True), m_inter)
            dqk = jnp.exp(log_d - m_j) * lax.dot_general(qb, kb, NT_DIMS, preferred_element_type=F32)
            inter = jnp.exp(m_inter - m_j)
            num = (jnp.dot(dqk.astype(BF16), vb, preferred_element_type=F32)
                   + inter * jnp.dot(qb, c_prev.astype(BF16), preferred_element_type=F32))
            den = jnp.sum(dqk, axis=1, keepdims=True) + inter * jnp.sum(q * n_prev, axis=1, keepdims=True)
            hid = num / jnp.maximum(jnp.abs(den), jnp.exp(-m_j))
            og = jax.nn.sigmoid(og_ref[r0:r0 + L, h * hd:(h + 1) * hd])
            o_ref[r0:r0 + L, h * hd:(h + 1) * hd] = (og * hid).astype(o_ref.dtype)

            m_new = jnp.maximum(b_last + m_prev, m_loc)
            a = jnp.exp(b_last + m_prev - m_new)
            bcoef = jnp.exp(m_loc - m_new)
            c_ref[h] = a * c_prev + bcoef * c_loc
            n_ref[h] = a * n_prev + bcoef * n_loc
            m_ref[h] = jnp.broadcast_to(m_new, (1, LANES))


def mlstm_branch(h_f32, misc, gate_bias, conv_w, bsz, seq, qk_blk, v_blk, o_blk, tt=512):
    nh, hd = MLSTM_HEADS, MLSTM_DIM
    w = nh * hd
    nt = seq // tt
    return pl.pallas_call(
        functools.partial(_mlstm_kernel, tt=tt),
        grid=(bsz, nt),
        in_specs=[pl.BlockSpec((tt, 2 * w), lambda b, t: (b * nt + t, qk_blk)),
                  pl.BlockSpec((tt, w), lambda b, t: (b * nt + t, v_blk)),
                  pl.BlockSpec((tt, w), lambda b, t: (b * nt + t, o_blk)),
                  pl.BlockSpec((tt, LANES), lambda b, t: (b * nt + t, 0)),
                  pl.BlockSpec((1, LANES), lambda b, t: (0, 0)),
                  pl.BlockSpec((CONV_WIDTH, 2 * w), lambda b, t: (0, 0))],
        out_specs=pl.BlockSpec((tt, w), lambda b, t: (b * nt + t, 0)),
        out_shape=jax.ShapeDtypeStruct((bsz * seq, w), BF16),
        scratch_shapes=[pltpu.VMEM((tt + 8, 2 * w), F32), pltpu.VMEM((nh, hd, hd), F32),
                        pltpu.VMEM((nh, 1, hd), F32), pltpu.VMEM((nh, 1, LANES), F32)],
        compiler_params=_params(("arbitrary", "arbitrary"), 48),
    )(h_f32, h_f32, h_f32, misc, gate_bias, conv_w)


def rel_bucket(dist):
    max_exact = REL_BUCKETS // 2
    n = jnp.maximum(dist, 0)
    large = max_exact + (jnp.log(jnp.maximum(n, 1).astype(F32) / max_exact)
                         / math.log(REL_MAX_DIST / max_exact)
                         * (REL_BUCKETS - max_exact)).astype(jnp.int32)
    large = jnp.minimum(large, REL_BUCKETS - 1)
    return jnp.where(n < max_exact, n, large)


def bias_tiles(rel, t):
    assert t >= REL_MAX_DIST
    kk = jnp.arange(t)[:, None]
    qq = jnp.arange(t)[None, :]

    def lookup(dist):
        onehot = (rel_bucket(dist)[..., None] == jnp.arange(REL_BUCKETS)).astype(F32)
        return jnp.einsum('kqb,bh->hkq', onehot, rel.astype(F32), precision=lax.Precision.HIGHEST) * LOG2E

    t0 = jnp.where((qq >= kk)[None], lookup(qq - kk), NEG)
    t1 = lookup(t + qq - kk)
    t2 = lookup(jnp.full((t, t), 2 * t))
    return jnp.stack([t0, t1, t2])


def tri_pairs(nq):
    qi = [q for q in range(nq) for _ in range(q + 1)]
    kj = [k for q in range(nq) for k in range(q + 1)]
    return jnp.asarray(qi, jnp.int32), jnp.asarray(kj, jnp.int32)


def values_t_ext(h_bf16, col0, bsz, seq, heads, dv):
    v = h_bf16[:, col0:col0 + heads * dv].reshape(bsz, seq, heads, dv)
    vt = jnp.transpose(v, (0, 2, 3, 1))
    ones = jnp.ones((bsz, heads, V_ONES, seq), BF16)
    return jnp.concatenate([vt, ones], axis=2).reshape(bsz * heads * (dv + V_ONES), seq)


def _attn_maps(qi, kj, maps, bias_ref, mb_ref, s_ref, p_ref, m_ref, acc_ref):
    def run(const_bias):
        def qk(i):
            load_k, load_q, _, head = maps[i]
            s = lax.dot_general(load_k(), load_q(), NT_DIMS, preferred_element_type=F32)
            if not const_bias:
                s = s + bias_ref[0, head]
            if mb_ref is not None:
                s = s + mb_ref[...]
            s_ref[i % 2] = s

        qk(0)
        for i, (_, _, load_vt, head) in enumerate(maps):
            if i + 1 < len(maps):
                qk(i + 1)
            _softmax_pv(i, load_vt, bias_ref[0, head, 0:1, :] if const_bias else None, s_ref, p_ref, m_ref, acc_ref)

    far = qi - kj >= 2

    @pl.when(far)
    def _():
        run(True)

    @pl.when(jnp.logical_not(far))
    def _():
        run(False)


def _softmax_pv(idx, load_vt, bias_c, s_ref, p_ref, m_ref, acc_ref):
    tk, tq = s_ref.shape[1:]
    sb = s_ref.at[idx % 2]
    pb = p_ref.at[idx % 2]

    mx = [jnp.full((8, tq), -jnp.inf, F32) for _ in range(4)]
    for c in range(tk // 8):
        mx[c % 4] = jnp.maximum(mx[c % 4], sb[c * 8:(c + 1) * 8, :])
    m_tile = jnp.max(jnp.maximum(jnp.maximum(mx[0], mx[1]), jnp.maximum(mx[2], mx[3])), axis=0, keepdims=True)
    m_prev = m_ref[idx]
    if bias_c is not None:
        m_new = jnp.maximum(m_prev, m_tile + bias_c)
        m_b = jnp.broadcast_to(m_new - bias_c, (16, tq))
    else:
        m_new = jnp.maximum(m_prev, m_tile)
        m_b = jnp.broadcast_to(m_new, (16, tq))
    m_ref[idx] = m_new
    for c in range(tk // 16):
        pb[c * 16:(c + 1) * 16, :] = jnp.exp2(sb[c * 16:(c + 1) * 16, :] - m_b).astype(BF16)
    alpha = jnp.exp2(m_prev - m_new)
    acc_ref[idx] = alpha * acc_ref[idx] + jnp.dot(load_vt(), pb[...], preferred_element_type=F32)


def _init_softmax_state(m_ref, acc_ref):
    m_ref[...] = jnp.full(m_ref.shape, NEG, F32)
    acc_ref[...] = jnp.zeros_like(acc_ref)


def _diff_attn_kernel(qi_ref, kj_ref, q_ref, k_ref, vt_ref, bias_ref, lam_ref, gb_ref, o_ref,
                      s_ref, p_ref, m_ref, acc_ref, *, out_scale):
    p = pl.program_id(1)
    qi = qi_ref[p]
    kj = kj_ref[p]
    dq = DIFF_QK_DIM
    dv = DIFF_V_DIM
    vrows = dv + V_ONES

    @pl.when(kj == 0)
    def _():
        _init_softmax_state(m_ref, acc_ref)

    def load_q(h, mi):
        qh = q_ref[:, h * 2 * dq:(h + 1) * 2 * dq]
        lane = lax.broadcasted_iota(jnp.int32, qh.shape, 1)
        return jnp.where((lane < dq) if mi == 0 else (lane >= dq), qh, jnp.zeros_like(qh))

    maps = [(lambda h=h: k_ref[:, h * 2 * dq:(h + 1) * 2 * dq],
             functools.partial(load_q, h, mi),
             lambda h=h: vt_ref[h * vrows:(h + 1) * vrows, :], h)
            for h in range(DIFF_HEADS) for mi in range(2)]
    _attn_maps(qi, kj, maps, bias_ref, None, s_ref, p_ref, m_ref, acc_ref)

    @pl.when(kj == qi)
    def _():
        lam = lam_ref[0:1, 0:1]
        for h in range(DIFF_HEADS):
            a1 = acc_ref[2 * h]
            a2 = acc_ref[2 * h + 1]
            a = a1[0:dv] / a1[dv:dv + 1] - lam * (a2[0:dv] / a2[dv:dv + 1])
            y = a * lax.rsqrt(jnp.mean(a * a, axis=0, keepdims=True) + EPS) * gb_ref[...] * out_scale
            o_ref[:, h * dv:(h + 1) * dv] = y.T.astype(o_ref.dtype)


def diff_attention(h_bf16, vt_ext, bsz, seq, q_blk, k_blk, bias, lam_row, subln_g, out_scale, t=512):
    nq = seq // t
    qi, kj = tri_pairs(nq)
    w = DIFF_HEADS * DIFF_V_DIM
    nmap = 2 * DIFF_HEADS
    vrows = DIFF_HEADS * (DIFF_V_DIM + V_ONES)
    g_b = jnp.broadcast_to(subln_g.reshape(DIFF_V_DIM, 1), (DIFF_V_DIM, t))
    grid_spec = pltpu.PrefetchScalarGridSpec(
        num_scalar_prefetch=2,
        grid=(bsz, qi.shape[0]),
        in_specs=[pl.BlockSpec((t, w), lambda b, p, qi, kj: (b * nq + qi[p], q_blk)),
                  pl.BlockSpec((t, w), lambda b, p, qi, kj: (b * nq + kj[p], k_blk)),
                  pl.BlockSpec((vrows, t), lambda b, p, qi, kj: (b, kj[p])),
                  pl.BlockSpec((1, DIFF_HEADS, t, t),
                               lambda b, p, qi, kj: (jnp.minimum(qi[p] - kj[p], 2), 0, 0, 0)),
                  pl.BlockSpec((1, LANES), lambda b, p, qi, kj: (0, 0)),
                  pl.BlockSpec((DIFF_V_DIM, t), lambda b, p, qi, kj: (0, 0))],
        out_specs=pl.BlockSpec((t, w), lambda b, p, qi, kj: (b * nq + qi[p], 0)),
        scratch_shapes=[pltpu.VMEM((2, t, t), F32), pltpu.VMEM((2, t, t), BF16),
                        pltpu.VMEM((nmap, 1, t), F32), pltpu.VMEM((nmap, DIFF_V_DIM + V_ONES, t), F32)],
    )
    return pl.pallas_call(
        functools.partial(_diff_attn_kernel, out_scale=out_scale),
        grid_spec=grid_spec,
        out_shape=jax.ShapeDtypeStruct((bsz * seq, w), BF16),
        compiler_params=_params(("arbitrary", "arbitrary"), 48),
    )(qi, kj, h_bf16, h_bf16, vt_ext, bias, lam_row, g_b)


def _sort_key(x):
    b = lax.bitcast_convert_type(x, jnp.int32)
    return b ^ ((b >> 31) & 0x7FFFFFFF)


def _batcher_pairs(n):
    def merge(lo, hi, r):
        step = r * 2
        if step < hi - lo:
            yield from merge(lo, hi, step)
            yield from merge(lo + r, hi, step)
            yield from [(a, a + r) for a in range(lo + r, hi - r, step)]
        else:
            yield (lo, lo + r)

    def sort(lo, hi):
        if hi - lo >= 1:
            mid = lo + (hi - lo) // 2
            yield from sort(lo, mid)
            yield from sort(mid + 1, hi)
            yield from merge(lo, hi, 1)

    return tuple(sort(0, n - 1))


SORT_GROUP = 16


def _count_ge_sorted(v, cand):
    g = len(v)
    out, masks, level = [], [], 1
    while (1 << level) <= g:
        pivots = [v[(2 * k + 1) * g // (1 << level) - 1] for k in range(1 << (level - 1))]

        def pick(lo, count, bit):
            if count == 1:
                return pivots[lo]
            half = count // 2
            return jnp.where(masks[bit], pick(lo, half, bit + 1), pick(lo + half, half, bit + 1))

        masks.append(pick(0, len(pivots), 0) >= cand)
        out.append((masks[-1], float(g >> level)))
        level += 1
    out.append((v[g - 1] >= cand, 1.0))
    return out


def _dsa_select_kernel(iq_ref, w_ref, ikz_ref, o_ref, key_ref, srt_ref, a_ref, wb_ref, *, tq, tk, topk, seq):
    i = pl.program_id(1)
    last = (i * tq + tq - 1) // tk
    idx_scale = (IDX_DIM * IDX_HEADS) ** -0.5
    iw_row = 64
    grp = SORT_GROUP
    pairs = _batcher_pairs(grp)

    wt = w_ref[...].T
    for h in range(IDX_HEADS):
        wb_ref[h] = jnp.broadcast_to(wt[iw_row + h:iw_row + h + 1, :] * idx_scale, (8, tq))

    def in_prefix(off, n):
        key_pos = off + lax.broadcasted_iota(jnp.int32, (n, tq), 0)
        q_pos = i * tq + lax.broadcasted_iota(jnp.int32, (n, tq), 1)
        return key_pos <= q_pos

    q_stack = [jnp.concatenate([iq_ref[:, (2 * pg) * LANES:(2 * pg + 1) * LANES],
                                iq_ref[:, (2 * pg + 1) * LANES:(2 * pg + 2) * LANES]], axis=0) for pg in range(2)]

    def head_dots(j, slot):
        off = pl.multiple_of(j * tk, tk)
        for half in range(2):
            kt = ikz_ref[pl.ds(off, tk), half * LANES:(half + 1) * LANES]
            for pg in range(2):
                a_ref[slot, 2 * half + pg] = lax.dot_general(kt, q_stack[pg], NT_DIMS, preferred_element_type=F32)

    def score_tile(j, slot, diagonal):
        off = pl.multiple_of(j * tk, tk)
        for c in range(tk // 8):
            r0 = c * 8
            sc = None
            for half in range(2):
                for pg in range(2):
                    for s in range(2):
                        head = 2 * (2 * pg + s) + half
                        term = wb_ref[head] * jnp.maximum(
                            a_ref[slot, 2 * half + pg, r0:r0 + 8, s * tq:(s + 1) * tq], 0.0)
                        sc = term if sc is None else sc + term
            if diagonal:
                sc = jnp.where(in_prefix(off + r0, 8), sc, -jnp.inf)
            key_ref[pl.ds(pl.multiple_of(off + r0, 8), 8), :] = _sort_key(sc)
        for g in range(tk // (8 * grp)):
            base = off + g * grp * 8
            keys = [key_ref[pl.ds(pl.multiple_of(base + c * 8, 8), 8), :] for c in range(grp)]
            for lo, hi in pairs:
                keys[lo], keys[hi] = jnp.minimum(keys[lo], keys[hi]), jnp.maximum(keys[lo], keys[hi])
            for c in range(grp):
                srt_ref[pl.ds(pl.multiple_of(base + c * 8, 8), 8), :] = keys[c]

    def score_body(p, carry):
        head_dots(2 * p + 1, 1)
        score_tile(2 * p, 0, False)
        head_dots(2 * p + 2, 0)
        score_tile(2 * p + 1, 1, False)
        return carry

    head_dots(0, 0)
    lax.fori_loop(0, last // 2, score_body, 0)

    @pl.when(last % 2 == 0)
    def _():
        score_tile(last, 0, True)

    @pl.when(last % 2 == 1)
    def _():
        head_dots(last, 1)
        score_tile(last - 1, 0, False)
        score_tile(last, 1, True)

    n_pairs = (last + 2) // 2

    @pl.when(n_pairs * 2 > last + 1)
    def _():
        srt_ref[pl.ds(pl.multiple_of((last + 1) * tk, tk), tk), :] = jnp.full((tk, tq), INT_MIN, jnp.int32)

    n_terms = grp.bit_length()

    def bit_body(it, thr):
        cand = thr + lax.shift_left(jnp.int32(1), 31 - it)
        cand_b = jnp.broadcast_to(cand, (8, tq))

        def count_body(j, accs):
            blk = srt_ref[pl.ds(pl.multiple_of(j * 2 * tk, 2 * tk), 2 * tk), :]
            accs = list(accs)
            for g in range(2 * tk // (8 * grp)):
                run = [blk[(g * grp + c) * 8:(g * grp + c + 1) * 8, :] for c in range(grp)]
                for t, (mask, _) in enumerate(_count_ge_sorted(run, cand_b)):
                    accs[t] = accs[t] + jnp.where(mask, 1.0, 0.0)
            return tuple(accs)

        zero = jnp.zeros((8, tq), F32)
        accs = lax.fori_loop(0, n_pairs, count_body, (zero,) * n_terms)
        weights = [float(grp >> (t + 1)) for t in range(n_terms - 1)] + [1.0]
        total = accs[0] * weights[0]
        for t in range(1, n_terms):
            total = total + accs[t] * weights[t]
        cnt = jnp.sum(total, axis=0, keepdims=True)
        return jnp.where(cnt >= topk, cand, thr)

    thr = lax.fori_loop(0, 32, bit_body, jnp.full((1, tq), INT_MIN, jnp.int32))
    thr_b = jnp.broadcast_to(thr, (16, tq))

    def out_tile(j, diagonal):
        off = pl.multiple_of(j * tk, tk)
        for c in range(tk // 16):
            rows = pl.ds(pl.multiple_of(off + c * 16, 16), 16)
            keep = key_ref[rows, :] >= thr_b
            if diagonal:
                keep = keep & in_prefix(off + c * 16, 16)
            o_ref[0, rows, :] = jnp.where(keep, 0.0, NEG).astype(o_ref.dtype)

    def out_body(j, carry):
        out_tile(j, False)
        return carry

    lax.fori_loop(0, last, out_body, 0)
    out_tile(last, True)

    def fill_body(j, carry):
        o_ref[0, pl.ds(pl.multiple_of(j * tk, tk), tk), :] = jnp.full((tk, tq), NEG, o_ref.dtype)
        return carry

    lax.fori_loop(last + 1, seq // tk, fill_body, 0)


def dsa_select(h_bf16, misc, ikz, bsz, seq, iq_blk, tq=128, tk=512):
    nq = seq // tq
    topk = min(TOPK_MAX, seq // 4)
    w = IDX_HEADS * IDX_DIM
    return pl.pallas_call(
        functools.partial(_dsa_select_kernel, tq=tq, tk=tk, topk=topk, seq=seq),
        grid=(bsz, nq),
        in_specs=[pl.BlockSpec((tq, w), lambda b, i: (b * nq + i, iq_blk)),
                  pl.BlockSpec((tq, LANES), lambda b, i: (b * nq + i, 0)),
                  pl.BlockSpec((seq, 2 * LANES), lambda b, i: (b, 0), pipeline_mode=pl.Buffered(1))],
        out_specs=pl.BlockSpec((1, seq, tq), lambda b, i: (b, 0, i)),
        out_shape=jax.ShapeDtypeStruct((bsz, seq, seq), BF16),
        scratch_shapes=[pltpu.VMEM((seq, tq), jnp.int32), pltpu.VMEM((seq, tq), jnp.int32),
                        pltpu.VMEM((2, IDX_HEADS // 2, tk, 2 * tq), F32), pltpu.VMEM((IDX_HEADS, 8, tq), F32)],
        compiler_params=_params(("arbitrary", "arbitrary"), 48),
    )(h_bf16, misc, ikz)


def _dsa_attn_kernel(qi_ref, kj_ref, q_ref, k_ref, vt_ref, bias_ref, mask_ref, o_ref,
                     s_ref, p_ref, mb_ref, m_ref, acc_ref):
    p = pl.program_id(1)
    qi = qi_ref[p]
    kj = kj_ref[p]
    dh = HEAD_DIM
    vrows = dh + V_ONES

    @pl.when(kj == 0)
    def _():
        _init_softmax_state(m_ref, acc_ref)

    mb_ref[...] = mask_ref[0].astype(F32)
    maps = [(lambda h=h: k_ref[:, h * dh:(h + 1) * dh], lambda h=h: q_ref[:, h * dh:(h + 1) * dh],
             lambda h=h: vt_ref[h * vrows:(h + 1) * vrows, :], h) for h in range(DSA_HEADS)]
    _attn_maps(qi, kj, maps, bias_ref, mb_ref, s_ref, p_ref, m_ref, acc_ref)

    @pl.when(kj == qi)
    def _():
        for h in range(DSA_HEADS):
            a = acc_ref[h]
            o_ref[:, h * dh:(h + 1) * dh] = (a[0:dh] / a[dh:dh + 1]).T.astype(o_ref.dtype)


def dsa_attention(h_bf16, vt_ext, mask, bsz, seq, q_blk, k_blk, bias, t=512):
    nq = seq // t
    qi, kj = tri_pairs(nq)
    w = DSA_HEADS * HEAD_DIM
    vrows = DSA_HEADS * (HEAD_DIM + V_ONES)
    grid_spec = pltpu.PrefetchScalarGridSpec(
        num_scalar_prefetch=2,
        grid=(bsz, qi.shape[0]),
        in_specs=[pl.BlockSpec((t, w), lambda b, p, qi, kj: (b * nq + qi[p], q_blk)),
                  pl.BlockSpec((t, w), lambda b, p, qi, kj: (b * nq + kj[p], k_blk)),
                  pl.BlockSpec((vrows, t), lambda b, p, qi, kj: (b, kj[p])),
                  pl.BlockSpec((1, DSA_HEADS, t, t),
                               lambda b, p, qi, kj: (jnp.minimum(qi[p] - kj[p], 2), 0, 0, 0)),
                  pl.BlockSpec((1, t, t), lambda b, p, qi, kj: (b, kj[p], qi[p]))],
        out_specs=pl.BlockSpec((t, w), lambda b, p, qi, kj: (b * nq + qi[p], 0)),
        scratch_shapes=[pltpu.VMEM((2, t, t), F32), pltpu.VMEM((2, t, t), BF16), pltpu.VMEM((t, t), F32),
                        pltpu.VMEM((DSA_HEADS, 1, t), F32), pltpu.VMEM((DSA_HEADS, HEAD_DIM + V_ONES, t), F32)],
    )
    return pl.pallas_call(
        _dsa_attn_kernel,
        grid_spec=grid_spec,
        out_shape=jax.ShapeDtypeStruct((bsz * seq, w), BF16),
        compiler_params=_params(("arbitrary", "arbitrary"), 48),
    )(qi, kj, h_bf16, h_bf16, vt_ext, bias, mask)


_SPLITS = (512, 512, 512, 512, 64, 8, 512, 512, 512, 512, 1024, 512, 512, 4, 4)
_NAMES = ('a_q', 'a_k', 'a_v', 'a_iq', 'a_ik', 'a_iw', 'b_u', 'c_q', 'c_k', 'c_v', 'd_qk', 'd_v', 'd_o', 'd_i', 'd_f')


def _split_w_in(w_in):
    out, off = {}, 0
    for name, width in zip(_NAMES, _SPLITS):
        out[name] = w_in[:, off:off + width]
        off += width
    return out


def _group_w_in(w_in):
    c = _split_w_in(w_in)
    d = w_in.shape[0]
    z = lambda n: jnp.zeros((d, n), w_in.dtype)
    w_attn = jnp.concatenate([c['a_q'], c['a_k'], c['a_v'], c['a_iq'], c['c_q'], c['c_k'], c['c_v']], axis=1)
    w_scan = jnp.concatenate([c['d_qk'], c['b_u'], c['d_v'], c['d_o']], axis=1)
    w_misc = jnp.concatenate([c['a_ik'], z(64), z(64), c['a_ik'],
                              z(64), c['a_iw'], c['d_i'], c['d_f'], z(48)], axis=1)
    return w_attn.astype(BF16), w_scan.astype(BF16), w_misc.astype(BF16)


def _pad_ffn(w_in, w_out):
    d = w_in.shape[0]
    pad = D_FF_PAD - D_FF
    zi = jnp.zeros((d, pad), w_in.dtype)
    w_in_pad = jnp.concatenate([w_in[:, :D_FF], zi, w_in[:, D_FF:], zi], axis=1).astype(BF16)
    w_out_pad = jnp.concatenate([w_out, jnp.zeros((pad, d), w_out.dtype)], axis=0).astype(BF16)
    return w_in_pad, w_out_pad


def _branches(x, norm_g, l, bsz, seq, w_in, conv_w, i_bias, f_bias, s5, s5_d, s5_w_glu, diff_lambda,
              diff_subln_g, bias_a, bias_c):
    lambda_init = 0.8 - 0.6 * math.exp(-0.3 * l)
    w_attn, w_scan, w_misc = _group_w_in(w_in)
    ones = jnp.ones((512,), F32)
    attn_scale = jnp.concatenate([ones * (HEAD_DIM ** -0.5 * LOG2E), ones, ones, ones,
                                  ones * (DIFF_QK_DIM ** -0.5 * LOG2E), ones, ones]).reshape(1, -1)
    xn, h_attn = norm_matmul(x, norm_g, w_attn, attn_scale, BF16)
    h_scan = matmul(xn, w_scan, jnp.ones((1, w_scan.shape[1]), F32), F32)
    ikz, misc = misc_proj(xn, w_misc)

    mask = dsa_select(h_attn, misc, ikz, bsz, seq, iq_blk=3)
    vt_a = values_t_ext(h_attn, 2 * 512, bsz, seq, DSA_HEADS, HEAD_DIM)
    y_a = dsa_attention(h_attn, vt_a, mask, bsz, seq, 0, 1, bias_a)

    bbd, abar_re, abar_im, cbd = s5
    y_b = s5_branch(h_scan, 2, bsz, seq, bbd, abar_re, abar_im, cbd, s5_d.reshape(1, -1), s5_w_glu.astype(BF16))

    dl = diff_lambda.astype(F32)
    lam = jnp.exp(jnp.sum(dl[0] * dl[1])) - jnp.exp(jnp.sum(dl[2] * dl[3])) + lambda_init
    lam_row = jnp.full((1, LANES), lam, F32)
    vt_c = values_t_ext(h_attn, 6 * 512, bsz, seq, DIFF_HEADS, DIFF_V_DIM)
    y_c = diff_attention(h_attn, vt_c, bsz, seq, 4, 5, bias_c, lam_row, diff_subln_g, 1.0 - lambda_init)

    gate_bias = jnp.zeros((1, LANES), F32).at[0, 72:76].set(i_bias).at[0, 76:80].set(f_bias)
    y_d = mlstm_branch(h_scan, misc, gate_bias, conv_w, bsz, seq, qk_blk=0, v_blk=3, o_blk=4)
    return xn, (y_a, y_b, y_c, y_d)


def _mixer(x, l, bsz, seq, norm_g, w_in, conv_w, i_bias, f_bias, s5, s5_d, s5_w_glu, diff_lambda,
           diff_subln_g, bias_a, bias_c, w_gate, b_gate, w_branch, w_out):
    xn, ys = _branches(x, norm_g, l, bsz, seq, w_in, conv_w, i_bias, f_bias, s5, s5_d, s5_w_glu, diff_lambda,
                       diff_subln_g, bias_a, bias_c)
    merged = gated_merge(xn, ys, w_gate.astype(BF16), b_gate, w_branch.astype(BF16))
    return matmul_residual(merged, w_out.astype(BF16), x)


def kernel(x, norm_g, w_ffn_in, w_ffn_out, w_in, conv_w, mlstm_i_bias, mlstm_f_bias, s5_a_re, s5_a_im,
           s5_log_dt, s5_b_re, s5_b_im, s5_c_re, s5_c_im, s5_d, s5_w_glu, diff_lambda, diff_subln_g,
           rel_table, w_gate, b_gate, w_branch, w_out, final_g):
    bsz, seq, d = x.shape
    t_attn = 512
    bias_a = bias_tiles(rel_table[:, :DSA_HEADS], t_attn)
    bias_c = bias_tiles(rel_table[:, DSA_HEADS:], t_attn)
    xf = x.reshape(bsz * seq, d)
    for l in range(DEPTH):
        xf = ffn_block(xf, norm_g[l, 0], *_pad_ffn(w_ffn_in[l, 0], w_ffn_out[l, 0]))
        s5 = s5_tables(s5_a_re[l], s5_a_im[l], s5_log_dt[l], s5_b_re[l], s5_b_im[l], s5_c_re[l], s5_c_im[l])
        xf = _mixer(xf, l, bsz, seq, norm_g[l, 1], w_in[l], conv_w[l], mlstm_i_bias[l], mlstm_f_bias[l],
                    s5, s5_d[l], s5_w_glu[l], diff_lambda[l], diff_subln_g[l], bias_a, bias_c,
                    w_gate[l], b_gate[l], w_branch[l], w_out[l])
        xf = ffn_block(xf, norm_g[l, 2], *_pad_ffn(w_ffn_in[l, 1], w_ffn_out[l, 1]))
    return rmsnorm(xf, final_g, F32).reshape(bsz, seq, d)
```

```python
import functools
import math

import jax
import jax.numpy as jnp
from jax import lax
from jax.experimental import pallas as pl
from jax.experimental.pallas import tpu as pltpu

F32 = jnp.float32
BF16 = jnp.bfloat16

D_MODEL = 2048
DEPTH = 4
N_BRANCH = 4
BRANCH_WIDTH = 512
HEAD_DIM = 128
DSA_HEADS = 4
IDX_HEADS = 8
IDX_DIM = 64
TOPK_MAX = 256
S5_WIDTH = 512
S5_GROUP = 16
S5_GROUPS = S5_WIDTH // S5_GROUP
S5_STATE = 64
DIFF_HEADS = 4
DIFF_QK_DIM = 64
DIFF_V_DIM = 128
MLSTM_HEADS = 4
MLSTM_DIM = 128
CONV_WIDTH = 4
D_FF = 5504
REL_BUCKETS = 32
REL_MAX_DIST = 128
EPS = 1e-6

LANES = 128
NEG = -1e30
INT_MIN = -(2 ** 31)
D_FF_PAD = 5632
MLSTM_L = 128

V_ONES = 16
LOG2E = math.log2(math.e)

NT_DIMS = (((1,), (1,)), ((), ()))


def _params(sem, vmem_mb):
    return pltpu.CompilerParams(dimension_semantics=sem, vmem_limit_bytes=vmem_mb * 1024 * 1024)


def _rms(x, g):
    return x * lax.rsqrt(jnp.mean(x * x, axis=-1, keepdims=True) + EPS) * g


def _rmsnorm_kernel(x_ref, g_ref, o_ref):
    o_ref[...] = _rms(x_ref[...], g_ref[...]).astype(o_ref.dtype)


def rmsnorm(x, g, out_dtype, tm=512):
    m, d = x.shape
    return pl.pallas_call(
        _rmsnorm_kernel,
        grid=(m // tm,),
        in_specs=[pl.BlockSpec((tm, d), lambda i: (i, 0)), pl.BlockSpec((1, d), lambda i: (0, 0))],
        out_specs=pl.BlockSpec((tm, d), lambda i: (i, 0)),
        out_shape=jax.ShapeDtypeStruct((m, d), out_dtype),
        compiler_params=_params(("parallel",), 40),
    )(x, g.reshape(1, d))


def _mm_kernel(x_ref, w_ref, cs_ref, o_ref):
    acc = jnp.dot(x_ref[...], w_ref[...], preferred_element_type=F32)
    o_ref[...] = (acc * cs_ref[...]).astype(o_ref.dtype)


def matmul(x, w, col_scale, out_dtype, tm=1024, tn=512):
    m, k = x.shape
    n = w.shape[1]
    return pl.pallas_call(
        _mm_kernel,
        grid=(m // tm, n // tn),
        in_specs=[pl.BlockSpec((tm, k), lambda i, j: (i, 0)), pl.BlockSpec((k, tn), lambda i, j: (0, j)),
                  pl.BlockSpec((1, tn), lambda i, j: (0, j))],
        out_specs=pl.BlockSpec((tm, tn), lambda i, j: (i, j)),
        out_shape=jax.ShapeDtypeStruct((m, n), out_dtype),
        compiler_params=_params(("parallel", "arbitrary"), 40),
    )(x, w, col_scale)


def _norm_mm_kernel(x_ref, g_ref, w_ref, cs_ref, xn_ref, o_ref):
    @pl.when(pl.program_id(1) == 0)
    def _():
        xn_ref[...] = _rms(x_ref[...], g_ref[...]).astype(xn_ref.dtype)

    acc = jnp.dot(xn_ref[...], w_ref[...], preferred_element_type=F32)
    o_ref[...] = (acc * cs_ref[...]).astype(o_ref.dtype)


def norm_matmul(x, g, w, col_scale, out_dtype, tm=1024, tn=512):
    m, k = x.shape
    n = w.shape[1]
    return pl.pallas_call(
        _norm_mm_kernel,
        grid=(m // tm, n // tn),
        in_specs=[pl.BlockSpec((tm, k), lambda i, j: (i, 0)), pl.BlockSpec((1, k), lambda i, j: (0, 0)),
                  pl.BlockSpec((k, tn), lambda i, j: (0, j)), pl.BlockSpec((1, tn), lambda i, j: (0, j))],
        out_specs=[pl.BlockSpec((tm, k), lambda i, j: (i, 0)), pl.BlockSpec((tm, tn), lambda i, j: (i, j))],
        out_shape=[jax.ShapeDtypeStruct((m, k), BF16), jax.ShapeDtypeStruct((m, n), out_dtype)],
        compiler_params=_params(("parallel", "arbitrary"), 48),
    )(x, g.reshape(1, k), w, col_scale)


def _mm_res_kernel(x_ref, w_ref, r_ref, o_ref):
    o_ref[...] = r_ref[...] + jnp.dot(x_ref[...], w_ref[...], preferred_element_type=F32)


def matmul_residual(x, w, r, tm=1024, tn=512):
    m, k = x.shape
    n = w.shape[1]
    return pl.pallas_call(
        _mm_res_kernel,
        grid=(m // tm, n // tn),
        in_specs=[pl.BlockSpec((tm, k), lambda i, j: (i, 0)), pl.BlockSpec((k, tn), lambda i, j: (0, j)),
                  pl.BlockSpec((tm, tn), lambda i, j: (i, j))],
        out_specs=pl.BlockSpec((tm, tn), lambda i, j: (i, j)),
        out_shape=jax.ShapeDtypeStruct((m, n), F32),
        compiler_params=_params(("parallel", "arbitrary"), 40),
    )(x, w, r)


def _misc_proj_kernel(x_ref, w_ref, ikz_ref, misc_ref):
    acc = jnp.dot(x_ref[...], w_ref[...], preferred_element_type=F32)
    ikz_ref[...] = acc[:, :2 * LANES].astype(BF16)
    misc_ref[...] = acc[:, 2 * LANES:]


def misc_proj(xn, w, tm=1024):
    m, k = xn.shape
    n = w.shape[1]
    return pl.pallas_call(
        _misc_proj_kernel,
        grid=(m // tm,),
        in_specs=[pl.BlockSpec((tm, k), lambda i: (i, 0)), pl.BlockSpec((k, n), lambda i: (0, 0))],
        out_specs=[pl.BlockSpec((tm, 2 * LANES), lambda i: (i, 0)), pl.BlockSpec((tm, LANES), lambda i: (i, 0))],
        out_shape=[jax.ShapeDtypeStruct((m, 2 * LANES), BF16), jax.ShapeDtypeStruct((m, LANES), F32)],
        compiler_params=_params(("parallel",), 40),
    )(xn, w)


def _ffn_kernel(x_ref, g_ref, wg_ref, wu_ref, wo_ref, o_ref, xn_ref):
    j = pl.program_id(1)

    @pl.when(j == 0)
    def _():
        x = x_ref[...]
        xn_ref[...] = _rms(x, g_ref[...]).astype(BF16)
        o_ref[...] = x

    xn = xn_ref[...]
    g = jnp.dot(xn, wg_ref[...], preferred_element_type=F32)
    u = jnp.dot(xn, wu_ref[...], preferred_element_type=F32)
    a = (g * jax.nn.sigmoid(g) * u * 0.5).astype(BF16)
    o_ref[...] += jnp.dot(a, wo_ref[...], preferred_element_type=F32)


def ffn_block(x, g, w_in_pad, w_out_pad, tm=1024, tf=512):
    m, d = x.shape
    nf = w_out_pad.shape[0] // tf
    return pl.pallas_call(
        _ffn_kernel,
        grid=(m // tm, nf),
        in_specs=[pl.BlockSpec((tm, d), lambda i, j: (i, 0)),
                  pl.BlockSpec((1, d), lambda i, j: (0, 0)),
                  pl.BlockSpec((d, tf), lambda i, j: (0, j)),
                  pl.BlockSpec((d, tf), lambda i, j: (0, j + nf)),
                  pl.BlockSpec((tf, d), lambda i, j: (j, 0))],
        out_specs=pl.BlockSpec((tm, d), lambda i, j: (i, 0)),
        out_shape=jax.ShapeDtypeStruct((m, d), F32),
        scratch_shapes=[pltpu.VMEM((tm, d), BF16)],
        compiler_params=_params(("parallel", "arbitrary"), 56),
    )(x, g.reshape(1, d), w_in_pad, w_in_pad, w_out_pad)


def _merge_kernel(xn_ref, ya_ref, yb_ref, yc_ref, yd_ref, wg_ref, bg_ref, wb_ref, o_ref):
    xn = xn_ref[...]
    acc = None
    for n, y_ref in enumerate((ya_ref, yb_ref, yc_ref, yd_ref)):
        gate = jnp.dot(xn, wg_ref[n], preferred_element_type=F32) + bg_ref[n]
        proj = jnp.dot(y_ref[...], wb_ref[n], preferred_element_type=F32)
        term = jax.nn.sigmoid(gate) * proj
        acc = term if acc is None else acc + term
    o_ref[...] = acc.astype(o_ref.dtype)


def gated_merge(xn, ys, w_gate, b_gate, w_branch, tm=1024, tn=256):
    m, d = xn.shape
    bw = ys[0].shape[1]
    y_spec = pl.BlockSpec((tm, bw), lambda i, j: (i, 0))
    return pl.pallas_call(
        _merge_kernel,
        grid=(m // tm, d // tn),
        in_specs=[pl.BlockSpec((tm, d), lambda i, j: (i, 0)), y_spec, y_spec, y_spec, y_spec,
                  pl.BlockSpec((N_BRANCH, d, tn), lambda i, j: (0, 0, j)),
                  pl.BlockSpec((N_BRANCH, 1, tn), lambda i, j: (0, 0, j)),
                  pl.BlockSpec((N_BRANCH, bw, tn), lambda i, j: (0, 0, j))],
        out_specs=pl.BlockSpec((tm, tn), lambda i, j: (i, j)),
        out_shape=jax.ShapeDtypeStruct((m, d), BF16),
        compiler_params=_params(("parallel", "arbitrary"), 48),
    )(xn, *ys, w_gate, b_gate.reshape(N_BRANCH, 1, d), w_branch)


def _s5_kernel(u_ref, perm_ref, permt_ref, bbd_ref, pwr_ref, pwi_ref, cbd_ref, d_ref, wglu_ref, o_ref,
               x_ref, st_ref, *, tt):
    n = S5_GROUPS * S5_STATE
    sl = tt // 8
    lc = 512

    @pl.when(pl.program_id(1) == 0)
    def _():
        st_ref[...] = jnp.zeros_like(st_ref)

    u = u_ref[...]
    u_perm = jnp.dot(perm_ref[...], u.astype(BF16), preferred_element_type=F32).astype(BF16)
    x_ref[...] = jnp.dot(u_perm, bbd_ref[...], preferred_element_type=F32)

    for c0 in range(0, n, lc):
        re = slice(c0, c0 + lc)
        im = slice(n + c0, n + c0 + lc)
        ar = pwr_ref[0:1, re]
        ai = pwi_ref[0:1, re]
        xr = jnp.zeros((8, lc), F32)
        xi = jnp.zeros((8, lc), F32)
        for j in range(sl):
            rows = slice(8 * j, 8 * j + 8)
            nr = ar * xr - ai * xi + x_ref[rows, re]
            ni = ar * xi + ai * xr + x_ref[rows, im]
            xr, xi = nr, ni
            x_ref[rows, re] = xr
            x_ref[rows, im] = xi

        alr = pwr_ref[sl - 1:sl, re]
        ali = pwi_ref[sl - 1:sl, re]
        cr = [st_ref[0:1, re]]
        ci = [st_ref[1:2, re]]
        for s in range(1, 9):
            pr, pi = cr[-1], ci[-1]
            cr.append(xr[s - 1:s] + alr * pr - ali * pi)
            ci.append(xi[s - 1:s] + alr * pi + ali * pr)
        st_ref[0:1, re] = cr[8]
        st_ref[1:2, re] = ci[8]
        car = jnp.concatenate(cr[:8], axis=0)
        cai = jnp.concatenate(ci[:8], axis=0)
        for j in range(sl):
            rows = slice(8 * j, 8 * j + 8)
            pr = pwr_ref[j:j + 1, re]
            pi = pwi_ref[j:j + 1, re]
            x_ref[rows, re] = x_ref[rows, re] + (pr * car - pi * cai)
            x_ref[rows, im] = x_ref[rows, im] + (pr * cai + pi * car)

    y_perm = jnp.dot(x_ref[...].astype(BF16), cbd_ref[...], preferred_element_type=F32)
    y = jnp.dot(permt_ref[...], y_perm, precision=lax.Precision.HIGHEST, preferred_element_type=F32) + d_ref[...] * u
    z = jax.nn.gelu(y)
    gate = jnp.dot(z.astype(BF16), wglu_ref[...], preferred_element_type=F32)
    o_ref[...] = (z * jax.nn.sigmoid(gate)).astype(o_ref.dtype)


S5_TILE = 256


def s5_branch(h_f32, col_block, bsz, seq, bbd, pw_re, pw_im, cbd, d_skip, w_glu, tt=S5_TILE):
    n = S5_GROUPS * S5_STATE
    nt = seq // tt
    sl = tt // 8
    row = jnp.arange(tt)
    perm = jax.nn.one_hot((row % 8) * sl + row // 8, tt, dtype=F32)
    const = lambda shape: pl.BlockSpec(shape, lambda b, t: (0, 0))
    return pl.pallas_call(
        functools.partial(_s5_kernel, tt=tt),
        grid=(bsz, nt),
        in_specs=[pl.BlockSpec((tt, S5_WIDTH), lambda b, t: (b * nt + t, col_block)),
                  const((tt, tt)), const((tt, tt)),
                  const((S5_WIDTH, 2 * n)), const((sl, n)), const((sl, n)), const((2 * n, S5_WIDTH)),
                  const((1, S5_WIDTH)), const((S5_WIDTH, S5_WIDTH))],
        out_specs=pl.BlockSpec((tt, S5_WIDTH), lambda b, t: (b * nt + t, 0)),
        out_shape=jax.ShapeDtypeStruct((bsz * seq, S5_WIDTH), BF16),
        scratch_shapes=[pltpu.VMEM((tt, 2 * n), F32), pltpu.VMEM((8, n), F32)],
        compiler_params=_params(("arbitrary", "arbitrary"), 48),
    )(h_f32, perm.astype(BF16), perm.T, bbd, pw_re, pw_im, cbd, d_skip, w_glu)


def s5_tables(a_re, a_im, log_dt, b_re, b_im, c_re, c_im, n_pow=S5_TILE // 8):
    g, p = a_re.shape
    dt = jnp.exp(log_dt)[:, None]
    mag = jnp.exp(a_re * dt)
    ab_re, ab_im = mag * jnp.cos(a_im * dt), mag * jnp.sin(a_im * dt)
    steps = jnp.arange(1, n_pow + 1, dtype=F32)[:, None, None]
    pw_mag = jnp.exp(a_re * dt * steps)
    pw_re = (pw_mag * jnp.cos(a_im * dt * steps)).reshape(n_pow, g * p)
    pw_im = (pw_mag * jnp.sin(a_im * dt * steps)).reshape(n_pow, g * p)
    den = a_re * a_re + a_im * a_im
    nr, ni = ab_re - 1.0, ab_im
    coef_re = (nr * a_re + ni * a_im) / den
    coef_im = (ni * a_re - nr * a_im) / den
    bb_re = coef_re[..., None] * b_re - coef_im[..., None] * b_im
    bb_im = coef_re[..., None] * b_im + coef_im[..., None] * b_re
    eye = jnp.eye(g, dtype=F32)
    to_bd_in = lambda t: jnp.einsum('gpc,gh->gchp', t, eye).reshape(g * S5_GROUP, g * p)
    to_bd_out = lambda t: jnp.einsum('gcp,gh->gphc', t, eye).reshape(g * p, g * S5_GROUP)
    bbd = jnp.concatenate([to_bd_in(bb_re), to_bd_in(bb_im)], axis=1).astype(BF16)
    cbd = jnp.concatenate([to_bd_out(c_re), to_bd_out(-c_im)], axis=0).astype(BF16)
    return bbd, pw_re, pw_im, cbd


def _log_sigmoid(x):
    return jnp.minimum(x, 0.0) - jnp.log1p(jnp.exp(-jnp.abs(x)))


def _mlstm_kernel(qk_ref, v_ref, og_ref, gate_ref, gbias_ref, convw_ref, o_ref,
                  xbuf_ref, c_ref, n_ref, m_ref, *, tt):
    L = MLSTM_L
    hd = MLSTM_DIM
    nh = MLSTM_HEADS
    i_col, f_col = 72, 76

    @pl.when(pl.program_id(1) == 0)
    def _():
        xbuf_ref[0:8, :] = jnp.zeros((8, 2 * nh * hd), F32)
        c_ref[...] = jnp.zeros_like(c_ref)
        n_ref[...] = jnp.zeros_like(n_ref)
        m_ref[...] = jnp.zeros_like(m_ref)

    xbuf_ref[8:8 + tt, :] = qk_ref[...]
    cw = convw_ref[...]
    conv = xbuf_ref[8:8 + tt, :] * cw[CONV_WIDTH - 1:CONV_WIDTH]
    for sh in range(1, CONV_WIDTH):
        conv = conv + xbuf_ref[8 - sh:8 - sh + tt, :] * cw[CONV_WIDTH - 1 - sh:CONV_WIDTH - sh]
    xbuf_ref[0:8, :] = xbuf_ref[tt:tt + 8, :]
    qk = conv * jax.nn.sigmoid(conv)

    gates = gate_ref[...] + gbias_ref[...]
    logf = _log_sigmoid(gates)
    row = lax.broadcasted_iota(jnp.int32, (L, L), 0)
    col = lax.broadcasted_iota(jnp.int32, (L, L), 1)
    causal = col <= row
    tri = jnp.where(causal, 1.0, 0.0).astype(F32)

    for c in range(tt // L):
        r0 = c * L
        gc = gates[r0:r0 + L]
        bcum = jnp.dot(tri, logf[r0:r0 + L], precision=lax.Precision.HIGHEST, preferred_element_type=F32)
        gct = gc.T
        bcumt = bcum.T
        for h in range(nh):
            ig_col = gc[:, i_col + h:i_col + h + 1]
            b_col = bcum[:, f_col + h:f_col + h + 1]
            ig_row = gct[i_col + h:i_col + h + 1, :]
            b_row = bcumt[f_col + h:f_col + h + 1, :]
            b_last = b_col[L - 1:L, :]
            q = qk[r0:r0 + L, h * hd:(h + 1) * hd]
            k = qk[r0:r0 + L, (nh + h) * hd:(nh + h + 1) * hd] * (hd ** -0.5)
            vb = v_ref[r0:r0 + L, h * hd:(h + 1) * hd].astype(BF16)
            qb = q.astype(BF16)
            kb = k.astype(BF16)
            c_prev = c_ref[h]
            n_prev = n_ref[h]
            m_prev = m_ref[h][:, 0:1]

            g = b_last - b_col + ig_col
            m_loc = jnp.max(g, axis=0, keepdims=True)
            wk = jnp.exp(g - m_loc) * k
            c_loc = jnp.dot(wk.T.astype(BF16), vb, preferred_element_type=F32)
            n_loc = jnp.sum(wk, axis=0, keepdims=True)

            m_inter = b_col + m_prev
            log_d = jnp.where(causal, b_col - b_row + ig_row, -jnp.inf)
            m_j = jnp.maximum(jnp.max(log_d, axis=1, keepdims=True), m_inter)
            dqk = jnp.exp(log_d - m_j) * lax.dot_general(qb, kb, NT_DIMS, preferred_element_type=F32)
            inter = jnp.exp(m_inter - m_j)
            num = (jnp.dot(dqk.astype(BF16), vb, preferred_element_type=F32)
                   + inter * jnp.dot(qb, c_prev.astype(BF16), preferred_element_type=F32))
            den = jnp.sum(dqk, axis=1, keepdims=True) + inter * jnp.sum(q * n_prev, axis=1, keepdims=True)
            hid = num / jnp.maximum(jnp.abs(den), jnp.exp(-m_j))
            og = jax.nn.sigmoid(og_ref[r0:r0 + L, h * hd:(h + 1) * hd])
            o_ref[r0:r0 + L, h * hd:(h + 1) * hd] = (og * hid).astype(o_ref.dtype)

            m_new = jnp.maximum(b_last + m_prev, m_loc)
            a = jnp.exp(b_last + m_prev - m_new)
            bcoef = jnp.exp(m_loc - m_new)
            c_ref[h] = a * c_prev + bcoef * c_loc
            n_ref[h] = a * n_prev + bcoef * n_loc
            m_ref[h] = jnp.broadcast_to(m_new, (1, LANES))


def mlstm_branch(h_f32, misc, gate_bias, conv_w, bsz, seq, qk_blk, v_blk, o_blk, tt=512):
    nh, hd = MLSTM_HEADS, MLSTM_DIM
    w = nh * hd
    nt = seq // tt
    return pl.pallas_call(
        functools.partial(_mlstm_kernel, tt=tt),
        grid=(bsz, nt),
        in_specs=[pl.BlockSpec((tt, 2 * w), lambda b, t: (b * nt + t, qk_blk)),
                  pl.BlockSpec((tt, w), lambda b, t: (b * nt + t, v_blk)),
                  pl.BlockSpec((tt, w), lambda b, t: (b * nt + t, o_blk)),
                  pl.BlockSpec((tt, LANES), lambda b, t: (b * nt + t, 0)),
                  pl.BlockSpec((1, LANES), lambda b, t: (0, 0)),
                  pl.BlockSpec((CONV_WIDTH, 2 * w), lambda b, t: (0, 0))],
        out_specs=pl.BlockSpec((tt, w), lambda b, t: (b * nt + t, 0)),
        out_shape=jax.ShapeDtypeStruct((bsz * seq, w), BF16),
        scratch_shapes=[pltpu.VMEM((tt + 8, 2 * w), F32), pltpu.VMEM((nh, hd, hd), F32),
                        pltpu.VMEM((nh, 1, hd), F32), pltpu.VMEM((nh, 1, LANES), F32)],
        compiler_params=_params(("arbitrary", "arbitrary"), 48),
    )(h_f32, h_f32, h_f32, misc, gate_bias, conv_w)


def rel_bucket(dist):
    max_exact = REL_BUCKETS // 2
    n = jnp.maximum(dist, 0)
    large = max_exact + (jnp.log(jnp.maximum(n, 1).astype(F32) / max_exact)
                         / math.log(REL_MAX_DIST / max_exact)
                         * (REL_BUCKETS - max_exact)).astype(jnp.int32)
    large = jnp.minimum(large, REL_BUCKETS - 1)
    return jnp.where(n < max_exact, n, large)


def bias_tiles(rel, t):
    assert t >= REL_MAX_DIST
    kk = jnp.arange(t)[:, None]
    qq = jnp.arange(t)[None, :]

    def lookup(dist):
        onehot = (rel_bucket(dist)[..., None] == jnp.arange(REL_BUCKETS)).astype(F32)
        return jnp.einsum('kqb,bh->hkq', onehot, rel.astype(F32), precision=lax.Precision.HIGHEST) * LOG2E

    t0 = jnp.where((qq >= kk)[None], lookup(qq - kk), NEG)
    t1 = lookup(t + qq - kk)
    t2 = lookup(jnp.full((t, t), 2 * t))
    return jnp.stack([t0, t1, t2])


def tri_pairs(nq):
    qi = [q for q in range(nq) for _ in range(q + 1)]
    kj = [k for q in range(nq) for k in range(q + 1)]
    return jnp.asarray(qi, jnp.int32), jnp.asarray(kj, jnp.int32)


def values_t_ext(h_bf16, col0, bsz, seq, heads, dv):
    v = h_bf16[:, col0:col0 + heads * dv].reshape(bsz, seq, heads, dv)
    vt = jnp.transpose(v, (0, 2, 3, 1))
    ones = jnp.ones((bsz, heads, V_ONES, seq), BF16)
    return jnp.concatenate([vt, ones], axis=2).reshape(bsz * heads * (dv + V_ONES), seq)


def _attn_maps(qi, kj, maps, bias_ref, mb_ref, s_ref, p_ref, m_ref, acc_ref):
    def run(const_bias):
        def qk(i):
            load_k, load_q, _, head = maps[i]
            s = lax.dot_general(load_k(), load_q(), NT_DIMS, preferred_element_type=F32)
            if not const_bias:
                s = s + bias_ref[0, head]
            if mb_ref is not None:
                s = s + mb_ref[...]
            s_ref[i % 2] = s

        qk(0)
        for i, (_, _, load_vt, head) in enumerate(maps):
            if i + 1 < len(maps):
                qk(i + 1)
            _softmax_pv(i, load_vt, bias_ref[0, head, 0:1, :] if const_bias else None, s_ref, p_ref, m_ref, acc_ref)

    far = qi - kj >= 2

    @pl.when(far)
    def _():
        run(True)

    @pl.when(jnp.logical_not(far))
    def _():
        run(False)


def _softmax_pv(idx, load_vt, bias_c, s_ref, p_ref, m_ref, acc_ref):
    tk, tq = s_ref.shape[1:]
    sb = s_ref.at[idx % 2]
    pb = p_ref.at[idx % 2]

    mx = [jnp.full((8, tq), -jnp.inf, F32) for _ in range(4)]
    for c in range(tk // 8):
        mx[c % 4] = jnp.maximum(mx[c % 4], sb[c * 8:(c + 1) * 8, :])
    m_tile = jnp.max(jnp.maximum(jnp.maximum(mx[0], mx[1]), jnp.maximum(mx[2], mx[3])), axis=0, keepdims=True)
    m_prev = m_ref[idx]
    if bias_c is not None:
        m_new = jnp.maximum(m_prev, m_tile + bias_c)
        m_b = jnp.broadcast_to(m_new - bias_c, (16, tq))
    else:
        m_new = jnp.maximum(m_prev, m_tile)
        m_b = jnp.broadcast_to(m_new, (16, tq))
    m_ref[idx] = m_new
    for c in range(tk // 16):
        pb[c * 16:(c + 1) * 16, :] = jnp.exp2(sb[c * 16:(c + 1) * 16, :] - m_b).astype(BF16)
    alpha = jnp.exp2(m_prev - m_new)
    acc_ref[idx] = alpha * acc_ref[idx] + jnp.dot(load_vt(), pb[...], preferred_element_type=F32)


def _init_softmax_state(m_ref, acc_ref):
    m_ref[...] = jnp.full(m_ref.shape, NEG, F32)
    acc_ref[...] = jnp.zeros_like(acc_ref)


def _diff_attn_kernel(qi_ref, kj_ref, q_ref, k_ref, vt_ref, bias_ref, lam_ref, gb_ref, o_ref,
                      s_ref, p_ref, m_ref, acc_ref, *, out_scale):
    p = pl.program_id(1)
    qi = qi_ref[p]
    kj = kj_ref[p]
    dq = DIFF_QK_DIM
    dv = DIFF_V_DIM
    vrows = dv + V_ONES

    @pl.when(kj == 0)
    def _():
        _init_softmax_state(m_ref, acc_ref)

    def load_q(h, mi):
        qh = q_ref[:, h * 2 * dq:(h + 1) * 2 * dq]
        lane = lax.broadcasted_iota(jnp.int32, qh.shape, 1)
        return jnp.where((lane < dq) if mi == 0 else (lane >= dq), qh, jnp.zeros_like(qh))

    maps = [(lambda h=h: k_ref[:, h * 2 * dq:(h + 1) * 2 * dq],
             functools.partial(load_q, h, mi),
             lambda h=h: vt_ref[h * vrows:(h + 1) * vrows, :], h)
            for h in range(DIFF_HEADS) for mi in range(2)]
    _attn_maps(qi, kj, maps, bias_ref, None, s_ref, p_ref, m_ref, acc_ref)

    @pl.when(kj == qi)
    def _():
        lam = lam_ref[0:1, 0:1]
        for h in range(DIFF_HEADS):
            a1 = acc_ref[2 * h]
            a2 = acc_ref[2 * h + 1]
            a = a1[0:dv] / a1[dv:dv + 1] - lam * (a2[0:dv] / a2[dv:dv + 1])
            y = a * lax.rsqrt(jnp.mean(a * a, axis=0, keepdims=True) + EPS) * gb_ref[...] * out_scale
            o_ref[:, h * dv:(h + 1) * dv] = y.T.astype(o_ref.dtype)


def diff_attention(h_bf16, vt_ext, bsz, seq, q_blk, k_blk, bias, lam_row, subln_g, out_scale, t=512):
    nq = seq // t
    qi, kj = tri_pairs(nq)
    w = DIFF_HEADS * DIFF_V_DIM
    nmap = 2 * DIFF_HEADS
    vrows = DIFF_HEADS * (DIFF_V_DIM + V_ONES)
    g_b = jnp.broadcast_to(subln_g.reshape(DIFF_V_DIM, 1), (DIFF_V_DIM, t))
    grid_spec = pltpu.PrefetchScalarGridSpec(
        num_scalar_prefetch=2,
        grid=(bsz, qi.shape[0]),
        in_specs=[pl.BlockSpec((t, w), lambda b, p, qi, kj: (b * nq + qi[p], q_blk)),
                  pl.BlockSpec((t, w), lambda b, p, qi, kj: (b * nq + kj[p], k_blk)),
                  pl.BlockSpec((vrows, t), lambda b, p, qi, kj: (b, kj[p])),
                  pl.BlockSpec((1, DIFF_HEADS, t, t),
                               lambda b, p, qi, kj: (jnp.minimum(qi[p] - kj[p], 2), 0, 0, 0)),
                  pl.BlockSpec((1, LANES), lambda b, p, qi, kj: (0, 0)),
                  pl.BlockSpec((DIFF_V_DIM, t), lambda b, p, qi, kj: (0, 0))],
        out_specs=pl.BlockSpec((t, w), lambda b, p, qi, kj: (b * nq + qi[p], 0)),
        scratch_shapes=[pltpu.VMEM((2, t, t), F32), pltpu.VMEM((2, t, t), BF16),
                        pltpu.VMEM((nmap, 1, t), F32), pltpu.VMEM((nmap, DIFF_V_DIM + V_ONES, t), F32)],
    )
    return pl.pallas_call(
        functools.partial(_diff_attn_kernel, out_scale=out_scale),
        grid_spec=grid_spec,
        out_shape=jax.ShapeDtypeStruct((bsz * seq, w), BF16),
        compiler_params=_params(("arbitrary", "arbitrary"), 48),
    )(qi, kj, h_bf16, h_bf16, vt_ext, bias, lam_row, g_b)


def _sort_key(x):
    b = lax.bitcast_convert_type(x, jnp.int32)
    return b ^ ((b >> 31) & 0x7FFFFFFF)


def _batcher_pairs(n):
    def merge(lo, hi, r):
        step = r * 2
        if step < hi - lo:
            yield from merge(lo, hi, step)
            yield from merge(lo + r, hi, step)
            yield from [(a, a + r) for a in range(lo + r, hi - r, step)]
        else:
            yield (lo, lo + r)

    def sort(lo, hi):
        if hi - lo >= 1:
            mid = lo + (hi - lo) // 2
            yield from sort(lo, mid)
            yield from sort(mid + 1, hi)
            yield from merge(lo, hi, 1)

    return tuple(sort(0, n - 1))


SORT_GROUP = 16


def _count_ge_sorted(v, cand):
    g = len(v)
    out, masks, level = [], [], 1
    while (1 << level) <= g:
        pivots = [v[(2 * k + 1) * g // (1 << level) - 1] for k in range(1 << (level - 1))]

        def pick(lo, count, bit):
            if count == 1:
                return pivots[lo]
            half = count // 2
            return jnp.where(masks[bit], pick(lo, half, bit + 1), pick(lo + half, half, bit + 1))

        masks.append(pick(0, len(pivots), 0) >= cand)
        out.append((masks[-1], float(g >> level)))
        level += 1
    out.append((v[g - 1] >= cand, 1.0))
    return out


def _dsa_select_kernel(iq_ref, w_ref, ikz_ref, o_ref, key_ref, srt_ref, a_ref, wb_ref, *, tq, tk, topk, seq):
    i = pl.program_id(1)
    last = (i * tq + tq - 1) // tk
    idx_scale = (IDX_DIM * IDX_HEADS) ** -0.5
    iw_row = 64
    grp = SORT_GROUP
    pairs = _batcher_pairs(grp)

    wt = w_ref[...].T
    for h in range(IDX_HEADS):
        wb_ref[h] = jnp.broadcast_to(wt[iw_row + h:iw_row + h + 1, :], (8, tq))

    def in_prefix(off, n):
        key_pos = off + lax.broadcasted_iota(jnp.int32, (n, tq), 0)
        q_pos = i * tq + lax.broadcasted_iota(jnp.int32, (n, tq), 1)
        return key_pos <= q_pos

    q_stack = [jnp.concatenate([iq_ref[:, (2 * pg) * LANES:(2 * pg + 1) * LANES],
                                iq_ref[:, (2 * pg + 1) * LANES:(2 * pg + 2) * LANES]], axis=0) for pg in range(2)]

    def head_dots(j, slot):
        off = pl.multiple_of(j * tk, tk)
        for half in range(2):
            kt = ikz_ref[pl.ds(off, tk), half * LANES:(half + 1) * LANES]
            for pg in range(2):
                a_ref[slot, 2 * half + pg] = lax.dot_general(kt, q_stack[pg], NT_DIMS, preferred_element_type=F32)

    def score_tile(j, slot, diagonal):
        off = pl.multiple_of(j * tk, tk)
        for c in range(tk // 8):
            r0 = c * 8
            sc = None
            for half in range(2):
                for pg in range(2):
                    for s in range(2):
                        head = 2 * (2 * pg + s) + half
                        term = wb_ref[head] * jnp.maximum(
                            a_ref[slot, 2 * half + pg, r0:r0 + 8, s * tq:(s + 1) * tq], 0.0)
                        sc = term if sc is None else sc + term
            sc = sc * idx_scale
            if diagonal:
                sc = jnp.where(in_prefix(off + r0, 8), sc, -jnp.inf)
            key_ref[pl.ds(pl.multiple_of(off + r0, 8), 8), :] = _sort_key(sc)
        for g in range(tk // (8 * grp)):
            base = off + g * grp * 8
            keys = [key_ref[pl.ds(pl.multiple_of(base + c * 8, 8), 8), :] for c in range(grp)]
            for lo, hi in pairs:
                keys[lo], keys[hi] = jnp.minimum(keys[lo], keys[hi]), jnp.maximum(keys[lo], keys[hi])
            for c in range(grp):
                srt_ref[pl.ds(pl.multiple_of(base + c * 8, 8), 8), :] = keys[c]

    def score_body(p, carry):
        head_dots(2 * p + 1, 1)
        score_tile(2 * p, 0, False)
        head_dots(2 * p + 2, 0)
        score_tile(2 * p + 1, 1, False)
        return carry

    head_dots(0, 0)
    lax.fori_loop(0, last // 2, score_body, 0)

    @pl.when(last % 2 == 0)
    def _():
        score_tile(last, 0, True)

    @pl.when(last % 2 == 1)
    def _():
        head_dots(last, 1)
        score_tile(last - 1, 0, False)
        score_tile(last, 1, True)

    n_pairs = (last + 2) // 2

    @pl.when(n_pairs * 2 > last + 1)
    def _():
        srt_ref[pl.ds(pl.multiple_of((last + 1) * tk, tk), tk), :] = jnp.full((tk, tq), INT_MIN, jnp.int32)

    n_terms = grp.bit_length()

    def count_ge(cand):
        cand_b = jnp.broadcast_to(cand, (8, tq))

        def count_body(j, accs):
            blk = srt_ref[pl.ds(pl.multiple_of(j * 2 * tk, 2 * tk), 2 * tk), :]
            accs = list(accs)
            for g in range(2 * tk // (8 * grp)):
                run = [blk[(g * grp + c) * 8:(g * grp + c + 1) * 8, :] for c in range(grp)]
                for t, (mask, _) in enumerate(_count_ge_sorted(run, cand_b)):
                    accs[t] = accs[t] + jnp.where(mask, 1.0, 0.0)
            return tuple(accs)

        zero = jnp.zeros((8, tq), F32)
        accs = lax.fori_loop(0, n_pairs, count_body, (zero,) * n_terms)
        weights = [float(grp >> (t + 1)) for t in range(n_terms - 1)] + [1.0]
        total = accs[0] * weights[0]
        for t in range(1, n_terms):
            total = total + accs[t] * weights[t]
        return jnp.sum(total, axis=0, keepdims=True)

    def bit_body(it, carry):
        thr, cnt_thr = carry
        cand = thr + lax.shift_left(jnp.int32(1), 31 - it)
        cnt = count_ge(cand)
        ok = cnt >= topk
        return jnp.where(ok, cand, thr), jnp.where(ok, cnt, cnt_thr)

    thr, cnt_thr = lax.fori_loop(0, 32, bit_body, (jnp.full((1, tq), INT_MIN, jnp.int32),
                                                   jnp.full((1, tq), 2.0 * tk, F32) * n_pairs.astype(F32)))
    thr_b = jnp.broadcast_to(thr, (16, tq))

    q_pos = i * tq + lax.broadcasted_iota(jnp.int32, (1, tq), 1)
    excess = jnp.where(q_pos + 1 > topk, cnt_thr - topk, 0.0)
    any_excess = jnp.max(excess) > 0.5

    def out_tile(j, diagonal):
        off = pl.multiple_of(j * tk, tk)
        for c in range(tk // 16):
            rows = pl.ds(pl.multiple_of(off + c * 16, 16), 16)
            keep = key_ref[rows, :] >= thr_b
            if diagonal:
                keep = keep & in_prefix(off + c * 16, 16)
            o_ref[0, rows, :] = jnp.where(keep, 0.0, NEG).astype(o_ref.dtype)

    def out_body(j, carry):
        out_tile(j, False)
        return carry

    @pl.when(jnp.logical_not(any_excess))
    def _():
        lax.fori_loop(0, last, out_body, 0)
        out_tile(last, True)

    @pl.when(any_excess)
    def _():
        ties_kept = topk - count_ge(thr + 1)
        sub = lax.broadcasted_iota(jnp.int32, (8, tq), 0)
        thr8 = jnp.broadcast_to(thr, (8, tq))

        def prefix8(x):
            for sh in (1, 2, 4):
                x = x + jnp.where(sub >= sh, pltpu.roll(x, sh, 0), 0.0)
            return x

        def out_tile_ties(j, diagonal, seen):
            off = pl.multiple_of(j * tk, tk)
            for c in range(tk // 16):
                halves = []
                for hh in range(2):
                    r0 = off + c * 16 + hh * 8
                    key = key_ref[pl.ds(pl.multiple_of(r0, 8), 8), :]
                    tie = key == thr8
                    rank = prefix8(jnp.where(tie, 1.0, 0.0)) + seen
                    keep = (key > thr8) | (tie & (rank <= ties_kept))
                    if diagonal:
                        keep = keep & in_prefix(r0, 8)
                    halves.append(jnp.where(keep, 0.0, NEG))
                    seen = rank[7:8, :]
                o_ref[0, pl.ds(pl.multiple_of(off + c * 16, 16), 16), :] = jnp.concatenate(
                    halves, axis=0).astype(o_ref.dtype)
            return seen

        seen = lax.fori_loop(0, last, lambda j, s: out_tile_ties(j, False, s), jnp.zeros((1, tq), F32))
        out_tile_ties(last, True, seen)

    def fill_body(j, carry):
        o_ref[0, pl.ds(pl.multiple_of(j * tk, tk), tk), :] = jnp.full((tk, tq), NEG, o_ref.dtype)
        return carry

    lax.fori_loop(last + 1, seq // tk, fill_body, 0)


def dsa_select(h_bf16, misc, ikz, bsz, seq, iq_blk, tq=128, tk=512):
    nq = seq // tq
    topk = min(TOPK_MAX, seq // 4)
    w = IDX_HEADS * IDX_DIM
    return pl.pallas_call(
        functools.partial(_dsa_select_kernel, tq=tq, tk=tk, topk=topk, seq=seq),
        grid=(bsz, nq),
        in_specs=[pl.BlockSpec((tq, w), lambda b, i: (b * nq + i, iq_blk)),
                  pl.BlockSpec((tq, LANES), lambda b, i: (b * nq + i, 0)),
                  pl.BlockSpec((seq, 2 * LANES), lambda b, i: (b, 0), pipeline_mode=pl.Buffered(1))],
        out_specs=pl.BlockSpec((1, seq, tq), lambda b, i: (b, 0, i)),
        out_shape=jax.ShapeDtypeStruct((bsz, seq, seq), BF16),
        scratch_shapes=[pltpu.VMEM((seq, tq), jnp.int32), pltpu.VMEM((seq, tq), jnp.int32),
                        pltpu.VMEM((2, IDX_HEADS // 2, tk, 2 * tq), F32), pltpu.VMEM((IDX_HEADS, 8, tq), F32)],
        compiler_params=_params(("arbitrary", "arbitrary"), 48),
    )(h_bf16, misc, ikz)


def _dsa_attn_kernel(qi_ref, kj_ref, q_ref, k_ref, vt_ref, bias_ref, mask_ref, o_ref,
                     s_ref, p_ref, mb_ref, m_ref, acc_ref):
    p = pl.program_id(1)
    qi = qi_ref[p]
    kj = kj_ref[p]
    dh = HEAD_DIM
    vrows = dh + V_ONES

    @pl.when(kj == 0)
    def _():
        _init_softmax_state(m_ref, acc_ref)

    mb_ref[...] = mask_ref[0].astype(F32)
    maps = [(lambda h=h: k_ref[:, h * dh:(h + 1) * dh], lambda h=h: q_ref[:, h * dh:(h + 1) * dh],
             lambda h=h: vt_ref[h * vrows:(h + 1) * vrows, :], h) for h in range(DSA_HEADS)]
    _attn_maps(qi, kj, maps, bias_ref, mb_ref, s_ref, p_ref, m_ref, acc_ref)

    @pl.when(kj == qi)
    def _():
        for h in range(DSA_HEADS):
            a = acc_ref[h]
            o_ref[:, h * dh:(h + 1) * dh] = (a[0:dh] / a[dh:dh + 1]).T.astype(o_ref.dtype)


def dsa_attention(h_bf16, vt_ext, mask, bsz, seq, q_blk, k_blk, bias, t=512):
    nq = seq // t
    qi, kj = tri_pairs(nq)
    w = DSA_HEADS * HEAD_DIM
    vrows = DSA_HEADS * (HEAD_DIM + V_ONES)
    grid_spec = pltpu.PrefetchScalarGridSpec(
        num_scalar_prefetch=2,
        grid=(bsz, qi.shape[0]),
        in_specs=[pl.BlockSpec((t, w), lambda b, p, qi, kj: (b * nq + qi[p], q_blk)),
                  pl.BlockSpec((t, w), lambda b, p, qi, kj: (b * nq + kj[p], k_blk)),
                  pl.BlockSpec((vrows, t), lambda b, p, qi, kj: (b, kj[p])),
                  pl.BlockSpec((1, DSA_HEADS, t, t),
                               lambda b, p, qi, kj: (jnp.minimum(qi[p] - kj[p], 2), 0, 0, 0)),
                  pl.BlockSpec((1, t, t), lambda b, p, qi, kj: (b, kj[p], qi[p]))],
        out_specs=pl.BlockSpec((t, w), lambda b, p, qi, kj: (b * nq + qi[p], 0)),
        scratch_shapes=[pltpu.VMEM((2, t, t), F32), pltpu.VMEM((2, t, t), BF16), pltpu.VMEM((t, t), F32),
                        pltpu.VMEM((DSA_HEADS, 1, t), F32), pltpu.VMEM((DSA_HEADS, HEAD_DIM + V_ONES, t), F32)],
    )
    return pl.pallas_call(
        _dsa_attn_kernel,
        grid_spec=grid_spec,
        out_shape=jax.ShapeDtypeStruct((bsz * seq, w), BF16),
        compiler_params=_params(("arbitrary", "arbitrary"), 48),
    )(qi, kj, h_bf16, h_bf16, vt_ext, bias, mask)


_SPLITS = (512, 512, 512, 512, 64, 8, 512, 512, 512, 512, 1024, 512, 512, 4, 4)
_NAMES = ('a_q', 'a_k', 'a_v', 'a_iq', 'a_ik', 'a_iw', 'b_u', 'c_q', 'c_k', 'c_v', 'd_qk', 'd_v', 'd_o', 'd_i', 'd_f')


def _split_w_in(w_in):
    out, off = {}, 0
    for name, width in zip(_NAMES, _SPLITS):
        out[name] = w_in[:, off:off + width]
        off += width
    return out


def _group_w_in(w_in):
    c = _split_w_in(w_in)
    d = w_in.shape[0]
    z = lambda n: jnp.zeros((d, n), w_in.dtype)
    w_attn = jnp.concatenate([c['a_q'], c['a_k'], c['a_v'], c['a_iq'], c['c_q'], c['c_k'], c['c_v']], axis=1)
    w_scan = jnp.concatenate([c['d_qk'], c['b_u'], c['d_v'], c['d_o']], axis=1)
    w_misc = jnp.concatenate([c['a_ik'], z(64), z(64), c['a_ik'],
                              z(64), c['a_iw'], c['d_i'], c['d_f'], z(48)], axis=1)
    return w_attn.astype(BF16), w_scan.astype(BF16), w_misc.astype(BF16)


def _pad_ffn(w_in, w_out):
    d = w_in.shape[0]
    pad = D_FF_PAD - D_FF
    zi = jnp.zeros((d, pad), w_in.dtype)
    w_in_pad = jnp.concatenate([w_in[:, :D_FF], zi, w_in[:, D_FF:], zi], axis=1).astype(BF16)
    w_out_pad = jnp.concatenate([w_out, jnp.zeros((pad, d), w_out.dtype)], axis=0).astype(BF16)
    return w_in_pad, w_out_pad


def _branches(x, norm_g, l, bsz, seq, w_in, conv_w, i_bias, f_bias, s5, s5_d, s5_w_glu, diff_lambda,
              diff_subln_g, bias_a, bias_c):
    lambda_init = 0.8 - 0.6 * math.exp(-0.3 * l)
    w_attn, w_scan, w_misc = _group_w_in(w_in)
    ones = jnp.ones((512,), F32)
    attn_scale = jnp.concatenate([ones * (HEAD_DIM ** -0.5 * LOG2E), ones, ones, ones,
                                  ones * (DIFF_QK_DIM ** -0.5 * LOG2E), ones, ones]).reshape(1, -1)
    xn, h_attn = norm_matmul(x, norm_g, w_attn, attn_scale, BF16)
    h_scan = matmul(xn, w_scan, jnp.ones((1, w_scan.shape[1]), F32), F32)
    ikz, misc = misc_proj(xn, w_misc)

    mask = dsa_select(h_attn, misc, ikz, bsz, seq, iq_blk=3)
    vt_a = values_t_ext(h_attn, 2 * 512, bsz, seq, DSA_HEADS, HEAD_DIM)
    y_a = dsa_attention(h_attn, vt_a, mask, bsz, seq, 0, 1, bias_a)

    bbd, abar_re, abar_im, cbd = s5
    y_b = s5_branch(h_scan, 2, bsz, seq, bbd, abar_re, abar_im, cbd, s5_d.reshape(1, -1), s5_w_glu.astype(BF16))

    dl = diff_lambda.astype(F32)
    lam = jnp.exp(jnp.sum(dl[0] * dl[1])) - jnp.exp(jnp.sum(dl[2] * dl[3])) + lambda_init
    lam_row = jnp.full((1, LANES), lam, F32)
    vt_c = values_t_ext(h_attn, 6 * 512, bsz, seq, DIFF_HEADS, DIFF_V_DIM)
    y_c = diff_attention(h_attn, vt_c, bsz, seq, 4, 5, bias_c, lam_row, diff_subln_g, 1.0 - lambda_init)

    gate_bias = jnp.zeros((1, LANES), F32).at[0, 72:76].set(i_bias).at[0, 76:80].set(f_bias)
    y_d = mlstm_branch(h_scan, misc, gate_bias, conv_w, bsz, seq, qk_blk=0, v_blk=3, o_blk=4)
    return xn, (y_a, y_b, y_c, y_d)


def _mixer(x, l, bsz, seq, norm_g, w_in, conv_w, i_bias, f_bias, s5, s5_d, s5_w_glu, diff_lambda,
           diff_subln_g, bias_a, bias_c, w_gate, b_gate, w_branch, w_out):
    xn, ys = _branches(x, norm_g, l, bsz, seq, w_in, conv_w, i_bias, f_bias, s5, s5_d, s5_w_glu, diff_lambda,
                       diff_subln_g, bias_a, bias_c)
    merged = gated_merge(xn, ys, w_gate.astype(BF16), b_gate, w_branch.astype(BF16))
    return matmul_residual(merged, w_out.astype(BF16), x)


def kernel(x, norm_g, w_ffn_in, w_ffn_out, w_in, conv_w, mlstm_i_bias, mlstm_f_bias, s5_a_re, s5_a_im,
           s5_log_dt, s5_b_re, s5_b_im, s5_c_re, s5_c_im, s5_d, s5_w_glu, diff_lambda, diff_subln_g,
           rel_table, w_gate, b_gate, w_branch, w_out, final_g):
    bsz, seq, d = x.shape
    t_attn = 512
    bias_a = bias_tiles(rel_table[:, :DSA_HEADS], t_attn)
    bias_c = bias_tiles(rel_table[:, DSA_HEADS:], t_attn)
    xf = x.reshape(bsz * seq, d)
    for l in range(DEPTH):
        xf = ffn_block(xf, norm_g[l, 0], *_pad_ffn(w_ffn_in[l, 0], w_ffn_out[l, 0]))
        s5 = s5_tables(s5_a_re[l], s5_a_im[l], s5_log_dt[l], s5_b_re[l], s5_b_im[l], s5_c_re[l], s5_c_im[l])
        xf = _mixer(xf, l, bsz, seq, norm_g[l, 1], w_in[l], conv_w[l], mlstm_i_bias[l], mlstm_f_bias[l],
                    s5, s5_d[l], s5_w_glu[l], diff_lambda[l], diff_subln_g[l], bias_a, bias_c,
                    w_gate[l], b_gate[l], w_branch[l], w_out[l])
        xf = ffn_block(xf, norm_g[l, 2], *_pad_ffn(w_ffn_in[l, 1], w_ffn_out[l, 1]))
    return rmsnorm(xf, final_g, F32).reshape(bsz, seq, d)
```

```python
import functools
import math

import jax
import jax.numpy as jnp
from jax import lax
from jax.experimental import pallas as pl
from jax.experimental.pallas import tpu as pltpu

F32 = jnp.float32
BF16 = jnp.bfloat16

D_MODEL = 2048
DEPTH = 4
N_BRANCH = 4
BRANCH_WIDTH = 512
HEAD_DIM = 128
DSA_HEADS = 4
IDX_HEADS = 8
IDX_DIM = 64
TOPK_MAX = 256
S5_WIDTH = 512
S5_GROUP = 16
S5_GROUPS = S5_WIDTH // S5_GROUP
S5_STATE = 64
DIFF_HEADS = 4
DIFF_QK_DIM = 64
DIFF_V_DIM = 128
MLSTM_HEADS = 4
MLSTM_DIM = 128
CONV_WIDTH = 4
D_FF = 5504
REL_BUCKETS = 32
REL_MAX_DIST = 128
EPS = 1e-6

LANES = 128
NEG = -1e30
INT_MIN = -(2 ** 31)
D_FF_PAD = 5632
MLSTM_L = 128

MISC_IW_COL = 64
MISC_I_COL = MISC_IW_COL + IDX_HEADS
MISC_F_COL = MISC_I_COL + MLSTM_HEADS

V_ONES = 16
LOG2E = math.log2(math.e)

NT_DIMS = (((1,), (1,)), ((), ()))


def _params(sem, vmem_mb):
    return pltpu.CompilerParams(dimension_semantics=sem, vmem_limit_bytes=vmem_mb * 1024 * 1024)


def _rms(x, g):
    return x * lax.rsqrt(jnp.mean(x * x, axis=-1, keepdims=True) + EPS) * g


def _rmsnorm_kernel(x_ref, g_ref, o_ref):
    o_ref[...] = _rms(x_ref[...], g_ref[...]).astype(o_ref.dtype)


def rmsnorm(x, g, out_dtype, tm=512):
    m, d = x.shape
    return pl.pallas_call(
        _rmsnorm_kernel,
        grid=(m // tm,),
        in_specs=[pl.BlockSpec((tm, d), lambda i: (i, 0)), pl.BlockSpec((1, d), lambda i: (0, 0))],
        out_specs=pl.BlockSpec((tm, d), lambda i: (i, 0)),
        out_shape=jax.ShapeDtypeStruct((m, d), out_dtype),
        compiler_params=_params(("parallel",), 40),
    )(x, g.reshape(1, d))


def _mm_kernel(x_ref, w_ref, cs_ref, o_ref):
    acc = jnp.dot(x_ref[...], w_ref[...], preferred_element_type=F32)
    o_ref[...] = (acc * cs_ref[...]).astype(o_ref.dtype)


def matmul(x, w, col_scale, out_dtype, tm=1024, tn=512):
    m, k = x.shape
    n = w.shape[1]
    return pl.pallas_call(
        _mm_kernel,
        grid=(m // tm, n // tn),
        in_specs=[pl.BlockSpec((tm, k), lambda i, j: (i, 0)), pl.BlockSpec((k, tn), lambda i, j: (0, j)),
                  pl.BlockSpec((1, tn), lambda i, j: (0, j))],
        out_specs=pl.BlockSpec((tm, tn), lambda i, j: (i, j)),
        out_shape=jax.ShapeDtypeStruct((m, n), out_dtype),
        compiler_params=_params(("parallel", "arbitrary"), 40),
    )(x, w, col_scale)


def _norm_mm_kernel(x_ref, g_ref, w_ref, cs_ref, xn_ref, o_ref):
    @pl.when(pl.program_id(1) == 0)
    def _():
        xn_ref[...] = _rms(x_ref[...], g_ref[...]).astype(xn_ref.dtype)

    acc = jnp.dot(xn_ref[...], w_ref[...], preferred_element_type=F32)
    o_ref[...] = (acc * cs_ref[...]).astype(o_ref.dtype)


def norm_matmul(x, g, w, col_scale, out_dtype, tm=1024, tn=512):
    m, k = x.shape
    n = w.shape[1]
    return pl.pallas_call(
        _norm_mm_kernel,
        grid=(m // tm, n // tn),
        in_specs=[pl.BlockSpec((tm, k), lambda i, j: (i, 0)), pl.BlockSpec((1, k), lambda i, j: (0, 0)),
                  pl.BlockSpec((k, tn), lambda i, j: (0, j)), pl.BlockSpec((1, tn), lambda i, j: (0, j))],
        out_specs=[pl.BlockSpec((tm, k), lambda i, j: (i, 0)), pl.BlockSpec((tm, tn), lambda i, j: (i, j))],
        out_shape=[jax.ShapeDtypeStruct((m, k), BF16), jax.ShapeDtypeStruct((m, n), out_dtype)],
        compiler_params=_params(("parallel", "arbitrary"), 48),
    )(x, g.reshape(1, k), w, col_scale)


def _mm_res_kernel(x_ref, w_ref, r_ref, o_ref):
    o_ref[...] = r_ref[...] + jnp.dot(x_ref[...], w_ref[...], preferred_element_type=F32)


def matmul_residual(x, w, r, tm=1024, tn=512):
    m, k = x.shape
    n = w.shape[1]
    return pl.pallas_call(
        _mm_res_kernel,
        grid=(m // tm, n // tn),
        in_specs=[pl.BlockSpec((tm, k), lambda i, j: (i, 0)), pl.BlockSpec((k, tn), lambda i, j: (0, j)),
                  pl.BlockSpec((tm, tn), lambda i, j: (i, j))],
        out_specs=pl.BlockSpec((tm, tn), lambda i, j: (i, j)),
        out_shape=jax.ShapeDtypeStruct((m, n), F32),
        compiler_params=_params(("parallel", "arbitrary"), 40),
    )(x, w, r)


def _misc_proj_kernel(x_ref, w_ref, ikz_ref, misc_ref):
    acc = jnp.dot(x_ref[...], w_ref[...], preferred_element_type=F32)
    ikz_ref[...] = acc[:, :2 * LANES].astype(BF16)
    misc_ref[...] = acc[:, 2 * LANES:]


def misc_proj(xn, w, tm=1024):
    m, k = xn.shape
    n = w.shape[1]
    return pl.pallas_call(
        _misc_proj_kernel,
        grid=(m // tm,),
        in_specs=[pl.BlockSpec((tm, k), lambda i: (i, 0)), pl.BlockSpec((k, n), lambda i: (0, 0))],
        out_specs=[pl.BlockSpec((tm, 2 * LANES), lambda i: (i, 0)), pl.BlockSpec((tm, LANES), lambda i: (i, 0))],
        out_shape=[jax.ShapeDtypeStruct((m, 2 * LANES), BF16), jax.ShapeDtypeStruct((m, LANES), F32)],
        compiler_params=_params(("parallel",), 40),
    )(xn, w)


def _ffn_kernel(x_ref, g_ref, wg_ref, wu_ref, wo_ref, o_ref, xn_ref):
    j = pl.program_id(1)

    @pl.when(j == 0)
    def _():
        x = x_ref[...]
        xn_ref[...] = _rms(x, g_ref[...]).astype(BF16)
        o_ref[...] = x

    xn = xn_ref[...]
    g = jnp.dot(xn, wg_ref[...], preferred_element_type=F32)
    u = jnp.dot(xn, wu_ref[...], preferred_element_type=F32)
    a = (g * jax.nn.sigmoid(g) * u * 0.5).astype(BF16)
    o_ref[...] += jnp.dot(a, wo_ref[...], preferred_element_type=F32)


def ffn_block(x, g, w_in_pad, w_out_pad, tm=1024, tf=512):
    m, d = x.shape
    nf = w_out_pad.shape[0] // tf
    return pl.pallas_call(
        _ffn_kernel,
        grid=(m // tm, nf),
        in_specs=[pl.BlockSpec((tm, d), lambda i, j: (i, 0)),
                  pl.BlockSpec((1, d), lambda i, j: (0, 0)),
                  pl.BlockSpec((d, tf), lambda i, j: (0, j)),
                  pl.BlockSpec((d, tf), lambda i, j: (0, j + nf)),
                  pl.BlockSpec((tf, d), lambda i, j: (j, 0))],
        out_specs=pl.BlockSpec((tm, d), lambda i, j: (i, 0)),
        out_shape=jax.ShapeDtypeStruct((m, d), F32),
        scratch_shapes=[pltpu.VMEM((tm, d), BF16)],
        compiler_params=_params(("parallel", "arbitrary"), 56),
    )(x, g.reshape(1, d), w_in_pad, w_in_pad, w_out_pad)


def _merge_kernel(xn_ref, ya_ref, yb_ref, yc_ref, yd_ref, wg_ref, bg_ref, wb_ref, o_ref):
    xn = xn_ref[...]
    acc = None
    for n, y_ref in enumerate((ya_ref, yb_ref, yc_ref, yd_ref)):
        gate = jnp.dot(xn, wg_ref[n], preferred_element_type=F32) + bg_ref[n]
        proj = jnp.dot(y_ref[...], wb_ref[n], preferred_element_type=F32)
        term = jax.nn.sigmoid(gate) * proj
        acc = term if acc is None else acc + term
    o_ref[...] = acc.astype(o_ref.dtype)


def gated_merge(xn, ys, w_gate, b_gate, w_branch, tm=1024, tn=256):
    m, d = xn.shape
    bw = ys[0].shape[1]
    y_spec = pl.BlockSpec((tm, bw), lambda i, j: (i, 0))
    return pl.pallas_call(
        _merge_kernel,
        grid=(m // tm, d // tn),
        in_specs=[pl.BlockSpec((tm, d), lambda i, j: (i, 0)), y_spec, y_spec, y_spec, y_spec,
                  pl.BlockSpec((N_BRANCH, d, tn), lambda i, j: (0, 0, j)),
                  pl.BlockSpec((N_BRANCH, 1, tn), lambda i, j: (0, 0, j)),
                  pl.BlockSpec((N_BRANCH, bw, tn), lambda i, j: (0, 0, j))],
        out_specs=pl.BlockSpec((tm, tn), lambda i, j: (i, j)),
        out_shape=jax.ShapeDtypeStruct((m, d), BF16),
        compiler_params=_params(("parallel", "arbitrary"), 48),
    )(xn, *ys, w_gate, b_gate.reshape(N_BRANCH, 1, d), w_branch)


def _s5_kernel(u_ref, perm_ref, permt_ref, bbd_ref, pwr_ref, pwi_ref, cbd_ref, d_ref, wglu_ref, o_ref,
               x_ref, st_ref, *, tt):
    n = S5_GROUPS * S5_STATE
    sl = tt // 8
    lc = 512

    @pl.when(pl.program_id(1) == 0)
    def _():
        st_ref[...] = jnp.zeros_like(st_ref)

    u = u_ref[...]
    u_perm = jnp.dot(perm_ref[...], u.astype(BF16), preferred_element_type=F32).astype(BF16)
    x_ref[...] = jnp.dot(u_perm, bbd_ref[...], preferred_element_type=F32)

    for c0 in range(0, n, lc):
        re = slice(c0, c0 + lc)
        im = slice(n + c0, n + c0 + lc)
        ar = pwr_ref[0:1, re]
        ai = pwi_ref[0:1, re]
        xr = jnp.zeros((8, lc), F32)
        xi = jnp.zeros((8, lc), F32)
        for j in range(sl):
            rows = slice(8 * j, 8 * j + 8)
            nr = ar * xr - ai * xi + x_ref[rows, re]
            ni = ar * xi + ai * xr + x_ref[rows, im]
            xr, xi = nr, ni
            x_ref[rows, re] = xr
            x_ref[rows, im] = xi

        alr = pwr_ref[sl - 1:sl, re]
        ali = pwi_ref[sl - 1:sl, re]
        cr = [st_ref[0:1, re]]
        ci = [st_ref[1:2, re]]
        for s in range(1, 9):
            pr, pi = cr[-1], ci[-1]
            cr.append(xr[s - 1:s] + alr * pr - ali * pi)
            ci.append(xi[s - 1:s] + alr * pi + ali * pr)
        st_ref[0:1, re] = cr[8]
        st_ref[1:2, re] = ci[8]
        car = jnp.concatenate(cr[:8], axis=0)
        cai = jnp.concatenate(ci[:8], axis=0)
        for j in range(sl):
            rows = slice(8 * j, 8 * j + 8)
            pr = pwr_ref[j:j + 1, re]
            pi = pwi_ref[j:j + 1, re]
            x_ref[rows, re] = x_ref[rows, re] + (pr * car - pi * cai)
            x_ref[rows, im] = x_ref[rows, im] + (pr * cai + pi * car)

    y_perm = jnp.dot(x_ref[...].astype(BF16), cbd_ref[...], preferred_element_type=F32)
    y = jnp.dot(permt_ref[...], y_perm, precision=lax.Precision.HIGHEST, preferred_element_type=F32) + d_ref[...] * u
    z = jax.nn.gelu(y)
    gate = jnp.dot(z.astype(BF16), wglu_ref[...], preferred_element_type=F32)
    o_ref[...] = (z * jax.nn.sigmoid(gate)).astype(o_ref.dtype)


S5_TILE = 256


def s5_branch(h_f32, col_block, bsz, seq, bbd, pw_re, pw_im, cbd, d_skip, w_glu, tt=S5_TILE):
    n = S5_GROUPS * S5_STATE
    nt = seq // tt
    sl = tt // 8
    row = jnp.arange(tt)
    perm = jax.nn.one_hot((row % 8) * sl + row // 8, tt, dtype=F32)
    const = lambda shape: pl.BlockSpec(shape, lambda b, t: (0, 0))
    return pl.pallas_call(
        functools.partial(_s5_kernel, tt=tt),
        grid=(bsz, nt),
        in_specs=[pl.BlockSpec((tt, S5_WIDTH), lambda b, t: (b * nt + t, col_block)),
                  const((tt, tt)), const((tt, tt)),
                  const((S5_WIDTH, 2 * n)), const((sl, n)), const((sl, n)), const((2 * n, S5_WIDTH)),
                  const((1, S5_WIDTH)), const((S5_WIDTH, S5_WIDTH))],
        out_specs=pl.BlockSpec((tt, S5_WIDTH), lambda b, t: (b * nt + t, 0)),
        out_shape=jax.ShapeDtypeStruct((bsz * seq, S5_WIDTH), BF16),
        scratch_shapes=[pltpu.VMEM((tt, 2 * n), F32), pltpu.VMEM((8, n), F32)],
        compiler_params=_params(("arbitrary", "arbitrary"), 48),
    )(h_f32, perm.astype(BF16), perm.T, bbd, pw_re, pw_im, cbd, d_skip, w_glu)


def s5_tables(a_re, a_im, log_dt, b_re, b_im, c_re, c_im, n_pow=S5_TILE // 8):
    g, p = a_re.shape
    dt = jnp.exp(log_dt)[:, None]
    mag = jnp.exp(a_re * dt)
    ab_re, ab_im = mag * jnp.cos(a_im * dt), mag * jnp.sin(a_im * dt)
    steps = jnp.arange(1, n_pow + 1, dtype=F32)[:, None, None]
    pw_mag = jnp.exp(a_re * dt * steps)
    pw_re = (pw_mag * jnp.cos(a_im * dt * steps)).reshape(n_pow, g * p)
    pw_im = (pw_mag * jnp.sin(a_im * dt * steps)).reshape(n_pow, g * p)
    den = a_re * a_re + a_im * a_im
    nr, ni = ab_re - 1.0, ab_im
    coef_re = (nr * a_re + ni * a_im) / den
    coef_im = (ni * a_re - nr * a_im) / den
    bb_re = coef_re[..., None] * b_re - coef_im[..., None] * b_im
    bb_im = coef_re[..., None] * b_im + coef_im[..., None] * b_re
    eye = jnp.eye(g, dtype=F32)
    to_bd_in = lambda t: jnp.einsum('gpc,gh->gchp', t, eye).reshape(g * S5_GROUP, g * p)
    to_bd_out = lambda t: jnp.einsum('gcp,gh->gphc', t, eye).reshape(g * p, g * S5_GROUP)
    bbd = jnp.concatenate([to_bd_in(bb_re), to_bd_in(bb_im)], axis=1).astype(BF16)
    cbd = jnp.concatenate([to_bd_out(c_re), to_bd_out(-c_im)], axis=0).astype(BF16)
    return bbd, pw_re, pw_im, cbd


def _log_sigmoid(x):
    return jnp.minimum(x, 0.0) - jnp.log1p(jnp.exp(-jnp.abs(x)))


def _mlstm_kernel(qk_ref, v_ref, og_ref, gate_ref, gbias_ref, convw_ref, o_ref,
                  xbuf_ref, c_ref, n_ref, m_ref, *, tt):
    L = MLSTM_L
    hd = MLSTM_DIM
    nh = MLSTM_HEADS
    i_col, f_col = MISC_I_COL, MISC_F_COL

    @pl.when(pl.program_id(1) == 0)
    def _():
        xbuf_ref[0:8, :] = jnp.zeros((8, 2 * nh * hd), F32)
        c_ref[...] = jnp.zeros_like(c_ref)
        n_ref[...] = jnp.zeros_like(n_ref)
        m_ref[...] = jnp.zeros_like(m_ref)

    xbuf_ref[8:8 + tt, :] = qk_ref[...]
    cw = convw_ref[...]
    conv = xbuf_ref[8:8 + tt, :] * cw[CONV_WIDTH - 1:CONV_WIDTH]
    for sh in range(1, CONV_WIDTH):
        conv = conv + xbuf_ref[8 - sh:8 - sh + tt, :] * cw[CONV_WIDTH - 1 - sh:CONV_WIDTH - sh]
    xbuf_ref[0:8, :] = xbuf_ref[tt:tt + 8, :]
    qk = conv * jax.nn.sigmoid(conv)

    gates = gate_ref[...] + gbias_ref[...]
    logf = _log_sigmoid(gates)
    row = lax.broadcasted_iota(jnp.int32, (L, L), 0)
    col = lax.broadcasted_iota(jnp.int32, (L, L), 1)
    causal = col <= row
    tri = jnp.where(causal, 1.0, 0.0).astype(F32)

    for c in range(tt // L):
        r0 = c * L
        gc = gates[r0:r0 + L]
        bcum = jnp.dot(tri, logf[r0:r0 + L], precision=lax.Precision.HIGHEST, preferred_element_type=F32)
        gct = gc.T
        bcumt = bcum.T
        for h in range(nh):
            ig_col = gc[:, i_col + h:i_col + h + 1]
            b_col = bcum[:, f_col + h:f_col + h + 1]
            ig_row = gct[i_col + h:i_col + h + 1, :]
            b_row = bcumt[f_col + h:f_col + h + 1, :]
            b_last = b_col[L - 1:L, :]
            q = qk[r0:r0 + L, h * hd:(h + 1) * hd]
            k = qk[r0:r0 + L, (nh + h) * hd:(nh + h + 1) * hd] * (hd ** -0.5)
            vb = v_ref[r0:r0 + L, h * hd:(h + 1) * hd].astype(BF16)
            qb = q.astype(BF16)
            kb = k.astype(BF16)
            c_prev = c_ref[h]
            n_prev = n_ref[h]
            m_prev = m_ref[h][:, 0:1]

            g = b_last - b_col + ig_col
            m_loc = jnp.max(g, axis=0, keepdims=True)
            wk = jnp.exp(g - m_loc) * k
            c_loc = jnp.dot(wk.T.astype(BF16), vb, preferred_element_type=F32)
            n_loc = jnp.sum(wk, axis=0, keepdims=True)

            m_inter = b_col + m_prev
            log_d = jnp.where(causal, b_col - b_row + ig_row, -jnp.inf)
            m_j = jnp.maximum(jnp.max(log_d, axis=1, keepdims=True), m_inter)
            dqk = jnp.exp(log_d - m_j) * lax.dot_general(qb, kb, NT_DIMS, preferred_element_type=F32)
            inter = jnp.exp(m_inter - m_j)
            num = (jnp.dot(dqk.astype(BF16), vb, preferred_element_type=F32)
                   + inter * jnp.dot(qb, c_prev.astype(BF16), preferred_element_type=F32))
            den = jnp.sum(dqk, axis=1, keepdims=True) + inter * jnp.sum(q * n_prev, axis=1, keepdims=True)
            hid = num / jnp.maximum(jnp.abs(den), jnp.exp(-m_j))
            og = jax.nn.sigmoid(og_ref[r0:r0 + L, h * hd:(h + 1) * hd])
            o_ref[r0:r0 + L, h * hd:(h + 1) * hd] = (og * hid).astype(o_ref.dtype)

            m_new = jnp.maximum(b_last + m_prev, m_loc)
            a = jnp.exp(b_last + m_prev - m_new)
            bcoef = jnp.exp(m_loc - m_new)
            c_ref[h] = a * c_prev + bcoef * c_loc
            n_ref[h] = a * n_prev + bcoef * n_loc
            m_ref[h] = jnp.broadcast_to(m_new, (1, LANES))


def mlstm_branch(h_f32, misc, gate_bias, conv_w, bsz, seq, qk_blk, v_blk, o_blk, tt=512):
    nh, hd = MLSTM_HEADS, MLSTM_DIM
    w = nh * hd
    nt = seq // tt
    return pl.pallas_call(
        functools.partial(_mlstm_kernel, tt=tt),
        grid=(bsz, nt),
        in_specs=[pl.BlockSpec((tt, 2 * w), lambda b, t: (b * nt + t, qk_blk)),
                  pl.BlockSpec((tt, w), lambda b, t: (b * nt + t, v_blk)),
                  pl.BlockSpec((tt, w), lambda b, t: (b * nt + t, o_blk)),
                  pl.BlockSpec((tt, LANES), lambda b, t: (b * nt + t, 0)),
                  pl.BlockSpec((1, LANES), lambda b, t: (0, 0)),
                  pl.BlockSpec((CONV_WIDTH, 2 * w), lambda b, t: (0, 0))],
        out_specs=pl.BlockSpec((tt, w), lambda b, t: (b * nt + t, 0)),
        out_shape=jax.ShapeDtypeStruct((bsz * seq, w), BF16),
        scratch_shapes=[pltpu.VMEM((tt + 8, 2 * w), F32), pltpu.VMEM((nh, hd, hd), F32),
                        pltpu.VMEM((nh, 1, hd), F32), pltpu.VMEM((nh, 1, LANES), F32)],
        compiler_params=_params(("arbitrary", "arbitrary"), 48),
    )(h_f32, h_f32, h_f32, misc, gate_bias, conv_w)


def rel_bucket(dist):
    max_exact = REL_BUCKETS // 2
    n = jnp.maximum(dist, 0)
    large = max_exact + (jnp.log(jnp.maximum(n, 1).astype(F32) / max_exact)
                         / math.log(REL_MAX_DIST / max_exact)
                         * (REL_BUCKETS - max_exact)).astype(jnp.int32)
    large = jnp.minimum(large, REL_BUCKETS - 1)
    return jnp.where(n < max_exact, n, large)


def bias_tiles(rel, t):
    assert t >= REL_MAX_DIST
    kk = jnp.arange(t)[:, None]
    qq = jnp.arange(t)[None, :]

    def lookup(dist):
        onehot = (rel_bucket(dist)[..., None] == jnp.arange(REL_BUCKETS)).astype(F32)
        return jnp.einsum('kqb,bh->hkq', onehot, rel.astype(F32), precision=lax.Precision.HIGHEST) * LOG2E

    t0 = jnp.where((qq >= kk)[None], lookup(qq - kk), NEG)
    t1 = lookup(t + qq - kk)
    t2 = lookup(jnp.full((t, t), 2 * t))
    return jnp.stack([t0, t1, t2])


def tri_pairs(nq):
    qi = [q for q in range(nq) for _ in range(q + 1)]
    kj = [k for q in range(nq) for k in range(q + 1)]
    return jnp.asarray(qi, jnp.int32), jnp.asarray(kj, jnp.int32)


def values_t_ext(h_bf16, col0, bsz, seq, heads, dv):
    v = h_bf16[:, col0:col0 + heads * dv].reshape(bsz, seq, heads, dv)
    vt = jnp.transpose(v, (0, 2, 3, 1))
    ones = jnp.ones((bsz, heads, V_ONES, seq), BF16)
    return jnp.concatenate([vt, ones], axis=2).reshape(bsz * heads * (dv + V_ONES), seq)


def _attn_maps(qi, kj, maps, bias_ref, mb_ref, s_ref, p_ref, m_ref, acc_ref):
    def run(const_bias):
        def qk(i):
            load_k, load_q, _, head = maps[i]
            s = lax.dot_general(load_k(), load_q(), NT_DIMS, preferred_element_type=F32)
            if not const_bias:
                s = s + bias_ref[0, head]
            if mb_ref is not None:
                s = s + mb_ref[...]
            s_ref[i % 2] = s

        qk(0)
        for i, (_, _, load_vt, head) in enumerate(maps):
            if i + 1 < len(maps):
                qk(i + 1)
            _softmax_pv(i, load_vt, bias_ref[0, head, 0:1, :] if const_bias else None, s_ref, p_ref, m_ref, acc_ref)

    far = qi - kj >= 2

    @pl.when(far)
    def _():
        run(True)

    @pl.when(jnp.logical_not(far))
    def _():
        run(False)


def _softmax_pv(idx, load_vt, bias_c, s_ref, p_ref, m_ref, acc_ref):
    tk, tq = s_ref.shape[1:]
    sb = s_ref.at[idx % 2]
    pb = p_ref.at[idx % 2]

    mx = [jnp.full((8, tq), -jnp.inf, F32) for _ in range(4)]
    for c in range(tk // 8):
        mx[c % 4] = jnp.maximum(mx[c % 4], sb[c * 8:(c + 1) * 8, :])
    m_tile = jnp.max(jnp.maximum(jnp.maximum(mx[0], mx[1]), jnp.maximum(mx[2], mx[3])), axis=0, keepdims=True)
    m_prev = m_ref[idx]
    if bias_c is not None:
        m_new = jnp.maximum(m_prev, m_tile + bias_c)
        m_b = jnp.broadcast_to(m_new - bias_c, (16, tq))
    else:
        m_new = jnp.maximum(m_prev, m_tile)
        m_b = jnp.broadcast_to(m_new, (16, tq))
    m_ref[idx] = m_new
    for c in range(tk // 16):
        pb[c * 16:(c + 1) * 16, :] = jnp.exp2(sb[c * 16:(c + 1) * 16, :] - m_b).astype(BF16)
    alpha = jnp.exp2(m_prev - m_new)
    acc_ref[idx] = alpha * acc_ref[idx] + jnp.dot(load_vt(), pb[...], preferred_element_type=F32)


def _init_softmax_state(m_ref, acc_ref):
    m_ref[...] = jnp.full(m_ref.shape, NEG, F32)
    acc_ref[...] = jnp.zeros_like(acc_ref)


def _diff_attn_kernel(qi_ref, kj_ref, q_ref, k_ref, vt_ref, bias_ref, lam_ref, gb_ref, o_ref,
                      s_ref, p_ref, m_ref, acc_ref, *, out_scale):
    p = pl.program_id(1)
    qi = qi_ref[p]
    kj = kj_ref[p]
    dq = DIFF_QK_DIM
    dv = DIFF_V_DIM
    vrows = dv + V_ONES

    @pl.when(kj == 0)
    def _():
        _init_softmax_state(m_ref, acc_ref)

    def load_q(h, mi):
        qh = q_ref[:, h * 2 * dq:(h + 1) * 2 * dq]
        lane = lax.broadcasted_iota(jnp.int32, qh.shape, 1)
        return jnp.where((lane < dq) if mi == 0 else (lane >= dq), qh, jnp.zeros_like(qh))

    maps = [(lambda h=h: k_ref[:, h * 2 * dq:(h + 1) * 2 * dq],
             functools.partial(load_q, h, mi),
             lambda h=h: vt_ref[h * vrows:(h + 1) * vrows, :], h)
            for h in range(DIFF_HEADS) for mi in range(2)]
    _attn_maps(qi, kj, maps, bias_ref, None, s_ref, p_ref, m_ref, acc_ref)

    @pl.when(kj == qi)
    def _():
        lam = lam_ref[0:1, 0:1]
        for h in range(DIFF_HEADS):
            a1 = acc_ref[2 * h]
            a2 = acc_ref[2 * h + 1]
            a = a1[0:dv] / a1[dv:dv + 1] - lam * (a2[0:dv] / a2[dv:dv + 1])
            y = a * lax.rsqrt(jnp.mean(a * a, axis=0, keepdims=True) + EPS) * gb_ref[...] * out_scale
            o_ref[:, h * dv:(h + 1) * dv] = y.T.astype(o_ref.dtype)


def diff_attention(h_bf16, vt_ext, bsz, seq, q_blk, k_blk, bias, lam_row, subln_g, out_scale, t=512):
    nq = seq // t
    qi, kj = tri_pairs(nq)
    w = DIFF_HEADS * DIFF_V_DIM
    nmap = 2 * DIFF_HEADS
    vrows = DIFF_HEADS * (DIFF_V_DIM + V_ONES)
    g_b = jnp.broadcast_to(subln_g.reshape(DIFF_V_DIM, 1), (DIFF_V_DIM, t))
    grid_spec = pltpu.PrefetchScalarGridSpec(
        num_scalar_prefetch=2,
        grid=(bsz, qi.shape[0]),
        in_specs=[pl.BlockSpec((t, w), lambda b, p, qi, kj: (b * nq + qi[p], q_blk)),
                  pl.BlockSpec((t, w), lambda b, p, qi, kj: (b * nq + kj[p], k_blk)),
                  pl.BlockSpec((vrows, t), lambda b, p, qi, kj: (b, kj[p])),
                  pl.BlockSpec((1, DIFF_HEADS, t, t),
                               lambda b, p, qi, kj: (jnp.minimum(qi[p] - kj[p], 2), 0, 0, 0)),
                  pl.BlockSpec((1, LANES), lambda b, p, qi, kj: (0, 0)),
                  pl.BlockSpec((DIFF_V_DIM, t), lambda b, p, qi, kj: (0, 0))],
        out_specs=pl.BlockSpec((t, w), lambda b, p, qi, kj: (b * nq + qi[p], 0)),
        scratch_shapes=[pltpu.VMEM((2, t, t), F32), pltpu.VMEM((2, t, t), BF16),
                        pltpu.VMEM((nmap, 1, t), F32), pltpu.VMEM((nmap, DIFF_V_DIM + V_ONES, t), F32)],
    )
    return pl.pallas_call(
        functools.partial(_diff_attn_kernel, out_scale=out_scale),
        grid_spec=grid_spec,
        out_shape=jax.ShapeDtypeStruct((bsz * seq, w), BF16),
        compiler_params=_params(("arbitrary", "arbitrary"), 48),
    )(qi, kj, h_bf16, h_bf16, vt_ext, bias, lam_row, g_b)


def _sort_key(x):
    b = lax.bitcast_convert_type(x, jnp.int32)
    return b ^ ((b >> 31) & 0x7FFFFFFF)


def _batcher_pairs(n):
    def merge(lo, hi, r):
        step = r * 2
        if step < hi - lo:
            yield from merge(lo, hi, step)
            yield from merge(lo + r, hi, step)
            yield from [(a, a + r) for a in range(lo + r, hi - r, step)]
        else:
            yield (lo, lo + r)

    def sort(lo, hi):
        if hi - lo >= 1:
            mid = lo + (hi - lo) // 2
            yield from sort(lo, mid)
            yield from sort(mid + 1, hi)
            yield from merge(lo, hi, 1)

    return tuple(sort(0, n - 1))


SORT_GROUP = 16


def _count_ge_sorted(v, cand):
    g = len(v)
    out, masks, level = [], [], 1
    while (1 << level) <= g:
        pivots = [v[(2 * k + 1) * g // (1 << level) - 1] for k in range(1 << (level - 1))]

        def pick(lo, count, bit):
            if count == 1:
                return pivots[lo]
            half = count // 2
            return jnp.where(masks[bit], pick(lo, half, bit + 1), pick(lo + half, half, bit + 1))

        masks.append(pick(0, len(pivots), 0) >= cand)
        out.append((masks[-1], float(g >> level)))
        level += 1
    out.append((v[g - 1] >= cand, 1.0))
    return out


def _dsa_select_kernel(iq_ref, w_ref, ikz_ref, o_ref, key_ref, srt_ref, a_ref, wb_ref, *, tq, tk, topk, seq):
    i = pl.program_id(1)
    last = (i * tq + tq - 1) // tk
    idx_scale = (IDX_DIM * IDX_HEADS) ** -0.5
    iw_row = MISC_IW_COL
    grp = SORT_GROUP
    pairs = _batcher_pairs(grp)

    wt = w_ref[...].T
    for h in range(IDX_HEADS):
        wb_ref[h] = jnp.broadcast_to(wt[iw_row + h:iw_row + h + 1, :], (8, tq))

    def in_prefix(off, n):
        key_pos = off + lax.broadcasted_iota(jnp.int32, (n, tq), 0)
        q_pos = i * tq + lax.broadcasted_iota(jnp.int32, (n, tq), 1)
        return key_pos <= q_pos

    q_stack = [jnp.concatenate([iq_ref[:, (2 * pg) * LANES:(2 * pg + 1) * LANES],
                                iq_ref[:, (2 * pg + 1) * LANES:(2 * pg + 2) * LANES]], axis=0) for pg in range(2)]

    def head_dots(j, slot):
        off = pl.multiple_of(j * tk, tk)
        for half in range(2):
            kt = ikz_ref[pl.ds(off, tk), half * LANES:(half + 1) * LANES]
            for pg in range(2):
                a_ref[slot, 2 * half + pg] = lax.dot_general(kt, q_stack[pg], NT_DIMS, preferred_element_type=F32)

    def score_tile(j, slot, diagonal):
        off = pl.multiple_of(j * tk, tk)
        for c in range(tk // 8):
            r0 = c * 8
            sc = None
            for half in range(2):
                for pg in range(2):
                    for s in range(2):
                        head = 2 * (2 * pg + s) + half
                        term = wb_ref[head] * jnp.maximum(
                            a_ref[slot, 2 * half + pg, r0:r0 + 8, s * tq:(s + 1) * tq], 0.0)
                        sc = term if sc is None else sc + term
            sc = sc * idx_scale
            if diagonal:
                sc = jnp.where(in_prefix(off + r0, 8), sc, -jnp.inf)
            key_ref[pl.ds(pl.multiple_of(off + r0, 8), 8), :] = _sort_key(sc)
        for g in range(tk // (8 * grp)):
            base = off + g * grp * 8
            keys = [key_ref[pl.ds(pl.multiple_of(base + c * 8, 8), 8), :] for c in range(grp)]
            for lo, hi in pairs:
                keys[lo], keys[hi] = jnp.minimum(keys[lo], keys[hi]), jnp.maximum(keys[lo], keys[hi])
            for c in range(grp):
                srt_ref[pl.ds(pl.multiple_of(base + c * 8, 8), 8), :] = keys[c]

    def score_body(p, carry):
        head_dots(2 * p + 1, 1)
        score_tile(2 * p, 0, False)
        head_dots(2 * p + 2, 0)
        score_tile(2 * p + 1, 1, False)
        return carry

    head_dots(0, 0)
    lax.fori_loop(0, last // 2, score_body, 0)

    @pl.when(last % 2 == 0)
    def _():
        score_tile(last, 0, True)

    @pl.when(last % 2 == 1)
    def _():
        head_dots(last, 1)
        score_tile(last - 1, 0, False)
        score_tile(last, 1, True)

    n_pairs = (last + 2) // 2

    @pl.when(n_pairs * 2 > last + 1)
    def _():
        srt_ref[pl.ds(pl.multiple_of((last + 1) * tk, tk), tk), :] = jnp.full((tk, tq), INT_MIN, jnp.int32)

    n_terms = grp.bit_length()

    def count_ge(cand):
        cand_b = jnp.broadcast_to(cand, (8, tq))

        def count_body(j, accs):
            blk = srt_ref[pl.ds(pl.multiple_of(j * 2 * tk, 2 * tk), 2 * tk), :]
            accs = list(accs)
            for g in range(2 * tk // (8 * grp)):
                run = [blk[(g * grp + c) * 8:(g * grp + c + 1) * 8, :] for c in range(grp)]
                for t, (mask, _) in enumerate(_count_ge_sorted(run, cand_b)):
                    accs[t] = accs[t] + jnp.where(mask, 1.0, 0.0)
            return tuple(accs)

        zero = jnp.zeros((8, tq), F32)
        accs = lax.fori_loop(0, n_pairs, count_body, (zero,) * n_terms)
        weights = [float(grp >> (t + 1)) for t in range(n_terms - 1)] + [1.0]
        total = accs[0] * weights[0]
        for t in range(1, n_terms):
            total = total + accs[t] * weights[t]
        return jnp.sum(total, axis=0, keepdims=True)

    def bit_body(it, carry):
        thr, cnt_thr = carry
        cand = thr + lax.shift_left(jnp.int32(1), 31 - it)
        cnt = count_ge(cand)
        ok = cnt >= topk
        return jnp.where(ok, cand, thr), jnp.where(ok, cnt, cnt_thr)

    thr, cnt_thr = lax.fori_loop(0, 32, bit_body, (jnp.full((1, tq), INT_MIN, jnp.int32),
                                                   jnp.full((1, tq), 2.0 * tk, F32) * n_pairs.astype(F32)))
    thr_b = jnp.broadcast_to(thr, (16, tq))

    q_pos = i * tq + lax.broadcasted_iota(jnp.int32, (1, tq), 1)
    excess = jnp.where(q_pos + 1 > topk, cnt_thr - topk, 0.0)
    any_excess = jnp.max(excess) > 0.5

    def out_tile(j, diagonal):
        off = pl.multiple_of(j * tk, tk)
        for c in range(tk // 16):
            rows = pl.ds(pl.multiple_of(off + c * 16, 16), 16)
            keep = key_ref[rows, :] >= thr_b
            if diagonal:
                keep = keep & in_prefix(off + c * 16, 16)
            o_ref[0, rows, :] = jnp.where(keep, 0.0, NEG).astype(o_ref.dtype)

    def out_body(j, carry):
        out_tile(j, False)
        return carry

    @pl.when(jnp.logical_not(any_excess))
    def _():
        lax.fori_loop(0, last, out_body, 0)
        out_tile(last, True)

    @pl.when(any_excess)
    def _():
        ties_kept = topk - count_ge(thr + 1)
        sub = lax.broadcasted_iota(jnp.int32, (8, tq), 0)
        thr8 = jnp.broadcast_to(thr, (8, tq))

        def prefix8(x):
            for sh in (1, 2, 4):
                x = x + jnp.where(sub >= sh, pltpu.roll(x, sh, 0), 0.0)
            return x

        def out_tile_ties(j, diagonal, seen):
            off = pl.multiple_of(j * tk, tk)
            for c in range(tk // 16):
                halves = []
                for hh in range(2):
                    r0 = off + c * 16 + hh * 8
                    key = key_ref[pl.ds(pl.multiple_of(r0, 8), 8), :]
                    tie = key == thr8
                    rank = prefix8(jnp.where(tie, 1.0, 0.0)) + seen
                    keep = (key > thr8) | (tie & (rank <= ties_kept))
                    if diagonal:
                        keep = keep & in_prefix(r0, 8)
                    halves.append(jnp.where(keep, 0.0, NEG))
                    seen = rank[7:8, :]
                o_ref[0, pl.ds(pl.multiple_of(off + c * 16, 16), 16), :] = jnp.concatenate(
                    halves, axis=0).astype(o_ref.dtype)
            return seen

        seen = lax.fori_loop(0, last, lambda j, s: out_tile_ties(j, False, s), jnp.zeros((1, tq), F32))
        out_tile_ties(last, True, seen)

    def fill_body(j, carry):
        o_ref[0, pl.ds(pl.multiple_of(j * tk, tk), tk), :] = jnp.full((tk, tq), NEG, o_ref.dtype)
        return carry

    lax.fori_loop(last + 1, seq // tk, fill_body, 0)


def dsa_select(h_bf16, misc, ikz, bsz, seq, iq_blk, tq=128, tk=512):
    nq = seq // tq
    topk = min(TOPK_MAX, seq // 4)
    w = IDX_HEADS * IDX_DIM
    return pl.pallas_call(
        functools.partial(_dsa_select_kernel, tq=tq, tk=tk, topk=topk, seq=seq),
        grid=(bsz, nq),
        in_specs=[pl.BlockSpec((tq, w), lambda b, i: (b * nq + i, iq_blk)),
                  pl.BlockSpec((tq, LANES), lambda b, i: (b * nq + i, 0)),
                  pl.BlockSpec((seq, 2 * LANES), lambda b, i: (b, 0), pipeline_mode=pl.Buffered(1))],
        out_specs=pl.BlockSpec((1, seq, tq), lambda b, i: (b, 0, i)),
        out_shape=jax.ShapeDtypeStruct((bsz, seq, seq), BF16),
        scratch_shapes=[pltpu.VMEM((seq, tq), jnp.int32), pltpu.VMEM((seq, tq), jnp.int32),
                        pltpu.VMEM((2, IDX_HEADS // 2, tk, 2 * tq), F32), pltpu.VMEM((IDX_HEADS, 8, tq), F32)],
        compiler_params=_params(("arbitrary", "arbitrary"), 48),
    )(h_bf16, misc, ikz)


def _dsa_attn_kernel(qi_ref, kj_ref, q_ref, k_ref, vt_ref, bias_ref, mask_ref, o_ref,
                     s_ref, p_ref, mb_ref, m_ref, acc_ref):
    p = pl.program_id(1)
    qi = qi_ref[p]
    kj = kj_ref[p]
    dh = HEAD_DIM
    vrows = dh + V_ONES

    @pl.when(kj == 0)
    def _():
        _init_softmax_state(m_ref, acc_ref)

    mb_ref[...] = mask_ref[0].astype(F32)
    maps = [(lambda h=h: k_ref[:, h * dh:(h + 1) * dh], lambda h=h: q_ref[:, h * dh:(h + 1) * dh],
             lambda h=h: vt_ref[h * vrows:(h + 1) * vrows, :], h) for h in range(DSA_HEADS)]
    _attn_maps(qi, kj, maps, bias_ref, mb_ref, s_ref, p_ref, m_ref, acc_ref)

    @pl.when(kj == qi)
    def _():
        for h in range(DSA_HEADS):
            a = acc_ref[h]
            o_ref[:, h * dh:(h + 1) * dh] = (a[0:dh] / a[dh:dh + 1]).T.astype(o_ref.dtype)


def dsa_attention(h_bf16, vt_ext, mask, bsz, seq, q_blk, k_blk, bias, t=512):
    nq = seq // t
    qi, kj = tri_pairs(nq)
    w = DSA_HEADS * HEAD_DIM
    vrows = DSA_HEADS * (HEAD_DIM + V_ONES)
    grid_spec = pltpu.PrefetchScalarGridSpec(
        num_scalar_prefetch=2,
        grid=(bsz, qi.shape[0]),
        in_specs=[pl.BlockSpec((t, w), lambda b, p, qi, kj: (b * nq + qi[p], q_blk)),
                  pl.BlockSpec((t, w), lambda b, p, qi, kj: (b * nq + kj[p], k_blk)),
                  pl.BlockSpec((vrows, t), lambda b, p, qi, kj: (b, kj[p])),
                  pl.BlockSpec((1, DSA_HEADS, t, t),
                               lambda b, p, qi, kj: (jnp.minimum(qi[p] - kj[p], 2), 0, 0, 0)),
                  pl.BlockSpec((1, t, t), lambda b, p, qi, kj: (b, kj[p], qi[p]))],
        out_specs=pl.BlockSpec((t, w), lambda b, p, qi, kj: (b * nq + qi[p], 0)),
        scratch_shapes=[pltpu.VMEM((2, t, t), F32), pltpu.VMEM((2, t, t), BF16), pltpu.VMEM((t, t), F32),
                        pltpu.VMEM((DSA_HEADS, 1, t), F32), pltpu.VMEM((DSA_HEADS, HEAD_DIM + V_ONES, t), F32)],
    )
    return pl.pallas_call(
        _dsa_attn_kernel,
        grid_spec=grid_spec,
        out_shape=jax.ShapeDtypeStruct((bsz * seq, w), BF16),
        compiler_params=_params(("arbitrary", "arbitrary"), 48),
    )(qi, kj, h_bf16, h_bf16, vt_ext, bias, mask)


_SPLITS = (512, 512, 512, 512, 64, 8, 512, 512, 512, 512, 1024, 512, 512, 4, 4)
_NAMES = ('a_q', 'a_k', 'a_v', 'a_iq', 'a_ik', 'a_iw', 'b_u', 'c_q', 'c_k', 'c_v', 'd_qk', 'd_v', 'd_o', 'd_i', 'd_f')


def _split_w_in(w_in):
    out, off = {}, 0
    for name, width in zip(_NAMES, _SPLITS):
        out[name] = w_in[:, off:off + width]
        off += width
    return out


def _group_w_in(w_in):
    c = _split_w_in(w_in)
    d = w_in.shape[0]
    z = lambda n: jnp.zeros((d, n), w_in.dtype)
    w_attn = jnp.concatenate([c['a_q'], c['a_k'], c['a_v'], c['a_iq'], c['c_q'], c['c_k'], c['c_v']], axis=1)
    w_scan = jnp.concatenate([c['d_qk'], c['b_u'], c['d_v'], c['d_o']], axis=1)
    w_misc = jnp.concatenate([c['a_ik'], z(IDX_DIM), z(IDX_DIM), c['a_ik'],
                              z(MISC_IW_COL), c['a_iw'], c['d_i'], c['d_f'],
                              z(LANES - MISC_F_COL - MLSTM_HEADS)], axis=1)
    return w_attn.astype(BF16), w_scan.astype(BF16), w_misc.astype(BF16)


def _pad_ffn(w_in, w_out):
    d = w_in.shape[0]
    pad = D_FF_PAD - D_FF
    zi = jnp.zeros((d, pad), w_in.dtype)
    w_in_pad = jnp.concatenate([w_in[:, :D_FF], zi, w_in[:, D_FF:], zi], axis=1).astype(BF16)
    w_out_pad = jnp.concatenate([w_out, jnp.zeros((pad, d), w_out.dtype)], axis=0).astype(BF16)
    return w_in_pad, w_out_pad


def _branches(x, norm_g, l, bsz, seq, w_in, conv_w, i_bias, f_bias, s5, s5_d, s5_w_glu, diff_lambda,
              diff_subln_g, bias_a, bias_c):
    lambda_init = 0.8 - 0.6 * math.exp(-0.3 * l)
    w_attn, w_scan, w_misc = _group_w_in(w_in)
    ones = jnp.ones((512,), F32)
    attn_scale = jnp.concatenate([ones * (HEAD_DIM ** -0.5 * LOG2E), ones, ones, ones,
                                  ones * (DIFF_QK_DIM ** -0.5 * LOG2E), ones, ones]).reshape(1, -1)
    xn, h_attn = norm_matmul(x, norm_g, w_attn, attn_scale, BF16)
    h_scan = matmul(xn, w_scan, jnp.ones((1, w_scan.shape[1]), F32), F32)
    ikz, misc = misc_proj(xn, w_misc)

    mask = dsa_select(h_attn, misc, ikz, bsz, seq, iq_blk=3)
    vt_a = values_t_ext(h_attn, 2 * 512, bsz, seq, DSA_HEADS, HEAD_DIM)
    y_a = dsa_attention(h_attn, vt_a, mask, bsz, seq, 0, 1, bias_a)

    bbd, abar_re, abar_im, cbd = s5
    y_b = s5_branch(h_scan, 2, bsz, seq, bbd, abar_re, abar_im, cbd, s5_d.reshape(1, -1), s5_w_glu.astype(BF16))

    dl = diff_lambda.astype(F32)
    lam = jnp.exp(jnp.sum(dl[0] * dl[1])) - jnp.exp(jnp.sum(dl[2] * dl[3])) + lambda_init
    lam_row = jnp.full((1, LANES), lam, F32)
    vt_c = values_t_ext(h_attn, 6 * 512, bsz, seq, DIFF_HEADS, DIFF_V_DIM)
    y_c = diff_attention(h_attn, vt_c, bsz, seq, 4, 5, bias_c, lam_row, diff_subln_g, 1.0 - lambda_init)

    gate_bias = (jnp.zeros((1, LANES), F32).at[0, MISC_I_COL:MISC_I_COL + MLSTM_HEADS].set(i_bias)
                 .at[0, MISC_F_COL:MISC_F_COL + MLSTM_HEADS].set(f_bias))
    y_d = mlstm_branch(h_scan, misc, gate_bias, conv_w, bsz, seq, qk_blk=0, v_blk=3, o_blk=4)
    return xn, (y_a, y_b, y_c, y_d)


def _mixer(x, l, bsz, seq, norm_g, w_in, conv_w, i_bias, f_bias, s5, s5_d, s5_w_glu, diff_lambda,
           diff_subln_g, bias_a, bias_c, w_gate, b_gate, w_branch, w_out):
    xn, ys = _branches(x, norm_g, l, bsz, seq, w_in, conv_w, i_bias, f_bias, s5, s5_d, s5_w_glu, diff_lambda,
                       diff_subln_g, bias_a, bias_c)
    merged = gated_merge(xn, ys, w_gate.astype(BF16), b_gate, w_branch.astype(BF16))
    return matmul_residual(merged, w_out.astype(BF16), x)


def kernel(x, norm_g, w_ffn_in, w_ffn_out, w_in, conv_w, mlstm_i_bias, mlstm_f_bias, s5_a_re, s5_a_im,
           s5_log_dt, s5_b_re, s5_b_im, s5_c_re, s5_c_im, s5_d, s5_w_glu, diff_lambda, diff_subln_g,
           rel_table, w_gate, b_gate, w_branch, w_out, final_g):
    bsz, seq, d = x.shape
    t_attn = 512
    bias_a = bias_tiles(rel_table[:, :DSA_HEADS], t_attn)
    bias_c = bias_tiles(rel_table[:, DSA_HEADS:], t_attn)
    xf = x.reshape(bsz * seq, d)
    for l in range(DEPTH):
        xf = ffn_block(xf, norm_g[l, 0], *_pad_ffn(w_ffn_in[l, 0], w_ffn_out[l, 0]))
        s5 = s5_tables(s5_a_re[l], s5_a_im[l], s5_log_dt[l], s5_b_re[l], s5_b_im[l], s5_c_re[l], s5_c_im[l])
        xf = _mixer(xf, l, bsz, seq, norm_g[l, 1], w_in[l], conv_w[l], mlstm_i_bias[l], mlstm_f_bias[l],
                    s5, s5_d[l], s5_w_glu[l], diff_lambda[l], diff_subln_g[l], bias_a, bias_c,
                    w_gate[l], b_gate[l], w_branch[l], w_out[l])
        xf = ffn_block(xf, norm_g[l, 2], *_pad_ffn(w_ffn_in[l, 1], w_ffn_out[l, 1]))
    return rmsnorm(xf, final_g, F32).reshape(bsz, seq, d)
```

```python
import functools
import math

import jax
import jax.numpy as jnp
from jax import lax
from jax.experimental import pallas as pl
from jax.experimental.pallas import tpu as pltpu

F32 = jnp.float32
BF16 = jnp.bfloat16

D_MODEL = 2048
DEPTH = 4
N_BRANCH = 4
BRANCH_WIDTH = 512
HEAD_DIM = 128
DSA_HEADS = 4
IDX_HEADS = 8
IDX_DIM = 64
TOPK_MAX = 256
S5_WIDTH = 512
S5_GROUP = 16
S5_GROUPS = S5_WIDTH // S5_GROUP
S5_STATE = 64
DIFF_HEADS = 4
DIFF_QK_DIM = 64
DIFF_V_DIM = 128
MLSTM_HEADS = 4
MLSTM_DIM = 128
CONV_WIDTH = 4
D_FF = 5504
REL_BUCKETS = 32
REL_MAX_DIST = 128
EPS = 1e-6

LANES = 128
NEG = -1e30
INT_MIN = -(2 ** 31)
D_FF_PAD = 5632
MLSTM_L = 128

MISC_IW_COL = 64
MISC_I_COL = MISC_IW_COL + IDX_HEADS
MISC_F_COL = MISC_I_COL + MLSTM_HEADS

V_ONES = 16
LOG2E = math.log2(math.e)

NT_DIMS = (((1,), (1,)), ((), ()))


def _params(sem, vmem_mb):
    return pltpu.CompilerParams(dimension_semantics=sem, vmem_limit_bytes=vmem_mb * 1024 * 1024)


def _rms(x, g):
    return x * lax.rsqrt(jnp.mean(x * x, axis=-1, keepdims=True) + EPS) * g


def _rmsnorm_kernel(x_ref, g_ref, o_ref):
    o_ref[...] = _rms(x_ref[...], g_ref[...]).astype(o_ref.dtype)


def rmsnorm(x, g, out_dtype, tm=512):
    m, d = x.shape
    return pl.pallas_call(
        _rmsnorm_kernel,
        grid=(m // tm,),
        in_specs=[pl.BlockSpec((tm, d), lambda i: (i, 0)), pl.BlockSpec((1, d), lambda i: (0, 0))],
        out_specs=pl.BlockSpec((tm, d), lambda i: (i, 0)),
        out_shape=jax.ShapeDtypeStruct((m, d), out_dtype),
        compiler_params=_params(("parallel",), 40),
    )(x, g.reshape(1, d))


def _mm_kernel(x_ref, w_ref, cs_ref, o_ref):
    acc = jnp.dot(x_ref[...], w_ref[...], preferred_element_type=F32)
    o_ref[...] = (acc * cs_ref[...]).astype(o_ref.dtype)


def matmul(x, w, col_scale, out_dtype, tm=1024, tn=512):
    m, k = x.shape
    n = w.shape[1]
    return pl.pallas_call(
        _mm_kernel,
        grid=(m // tm, n // tn),
        in_specs=[pl.BlockSpec((tm, k), lambda i, j: (i, 0)), pl.BlockSpec((k, tn), lambda i, j: (0, j)),
                  pl.BlockSpec((1, tn), lambda i, j: (0, j))],
        out_specs=pl.BlockSpec((tm, tn), lambda i, j: (i, j)),
        out_shape=jax.ShapeDtypeStruct((m, n), out_dtype),
        compiler_params=_params(("parallel", "arbitrary"), 40),
    )(x, w, col_scale)


def _norm_mm_kernel(x_ref, g_ref, w_ref, cs_ref, xn_ref, o_ref):
    @pl.when(pl.program_id(1) == 0)
    def _():
        xn_ref[...] = _rms(x_ref[...], g_ref[...]).astype(xn_ref.dtype)

    acc = jnp.dot(xn_ref[...], w_ref[...], preferred_element_type=F32)
    o_ref[...] = (acc * cs_ref[...]).astype(o_ref.dtype)


def norm_matmul(x, g, w, col_scale, out_dtype, tm=1024, tn=512):
    m, k = x.shape
    n = w.shape[1]
    return pl.pallas_call(
        _norm_mm_kernel,
        grid=(m // tm, n // tn),
        in_specs=[pl.BlockSpec((tm, k), lambda i, j: (i, 0)), pl.BlockSpec((1, k), lambda i, j: (0, 0)),
                  pl.BlockSpec((k, tn), lambda i, j: (0, j)), pl.BlockSpec((1, tn), lambda i, j: (0, j))],
        out_specs=[pl.BlockSpec((tm, k), lambda i, j: (i, 0)), pl.BlockSpec((tm, tn), lambda i, j: (i, j))],
        out_shape=[jax.ShapeDtypeStruct((m, k), BF16), jax.ShapeDtypeStruct((m, n), out_dtype)],
        compiler_params=_params(("parallel", "arbitrary"), 48),
    )(x, g.reshape(1, k), w, col_scale)


def _mm_res_kernel(x_ref, w_ref, r_ref, o_ref):
    o_ref[...] = r_ref[...] + jnp.dot(x_ref[...], w_ref[...], preferred_element_type=F32)


def matmul_residual(x, w, r, tm=1024, tn=512):
    m, k = x.shape
    n = w.shape[1]
    return pl.pallas_call(
        _mm_res_kernel,
        grid=(m // tm, n // tn),
        in_specs=[pl.BlockSpec((tm, k), lambda i, j: (i, 0)), pl.BlockSpec((k, tn), lambda i, j: (0, j)),
                  pl.BlockSpec((tm, tn), lambda i, j: (i, j))],
        out_specs=pl.BlockSpec((tm, tn), lambda i, j: (i, j)),
        out_shape=jax.ShapeDtypeStruct((m, n), F32),
        compiler_params=_params(("parallel", "arbitrary"), 40),
    )(x, w, r)


def _misc_proj_kernel(x_ref, w_ref, ikz_ref, misc_ref):
    acc = jnp.dot(x_ref[...], w_ref[...], preferred_element_type=F32)
    ikz_ref[...] = acc[:, :2 * LANES].astype(BF16)
    misc_ref[...] = acc[:, 2 * LANES:]


def misc_proj(xn, w, tm=1024):
    m, k = xn.shape
    n = w.shape[1]
    return pl.pallas_call(
        _misc_proj_kernel,
        grid=(m // tm,),
        in_specs=[pl.BlockSpec((tm, k), lambda i: (i, 0)), pl.BlockSpec((k, n), lambda i: (0, 0))],
        out_specs=[pl.BlockSpec((tm, 2 * LANES), lambda i: (i, 0)), pl.BlockSpec((tm, LANES), lambda i: (i, 0))],
        out_shape=[jax.ShapeDtypeStruct((m, 2 * LANES), BF16), jax.ShapeDtypeStruct((m, LANES), F32)],
        compiler_params=_params(("parallel",), 40),
    )(xn, w)


def _ffn_kernel(x_ref, g_ref, wg_ref, wu_ref, wo_ref, o_ref, xn_ref):
    j = pl.program_id(1)

    @pl.when(j == 0)
    def _():
        x = x_ref[...]
        xn_ref[...] = _rms(x, g_ref[...]).astype(BF16)
        o_ref[...] = x

    xn = xn_ref[...]
    g = jnp.dot(xn, wg_ref[...], preferred_element_type=F32)
    u = jnp.dot(xn, wu_ref[...], preferred_element_type=F32)
    a = (g * jax.nn.sigmoid(g) * u * 0.5).astype(BF16)
    o_ref[...] += jnp.dot(a, wo_ref[...], preferred_element_type=F32)


def ffn_block(x, g, w_in_pad, w_out_pad, tm=1024, tf=512):
    m, d = x.shape
    nf = w_out_pad.shape[0] // tf
    return pl.pallas_call(
        _ffn_kernel,
        grid=(m // tm, nf),
        in_specs=[pl.BlockSpec((tm, d), lambda i, j: (i, 0)),
                  pl.BlockSpec((1, d), lambda i, j: (0, 0)),
                  pl.BlockSpec((d, tf), lambda i, j: (0, j)),
                  pl.BlockSpec((d, tf), lambda i, j: (0, j + nf)),
                  pl.BlockSpec((tf, d), lambda i, j: (j, 0))],
        out_specs=pl.BlockSpec((tm, d), lambda i, j: (i, 0)),
        out_shape=jax.ShapeDtypeStruct((m, d), F32),
        scratch_shapes=[pltpu.VMEM((tm, d), BF16)],
        compiler_params=_params(("parallel", "arbitrary"), 56),
    )(x, g.reshape(1, d), w_in_pad, w_in_pad, w_out_pad)


def _merge_kernel(xn_ref, ya_ref, yb_ref, yc_ref, yd_ref, wg_ref, bg_ref, wb_ref, o_ref):
    xn = xn_ref[...]
    acc = None
    for n, y_ref in enumerate((ya_ref, yb_ref, yc_ref, yd_ref)):
        gate = jnp.dot(xn, wg_ref[n], preferred_element_type=F32) + bg_ref[n]
        proj = jnp.dot(y_ref[...], wb_ref[n], preferred_element_type=F32)
        term = jax.nn.sigmoid(gate) * proj
        acc = term if acc is None else acc + term
    o_ref[...] = acc.astype(o_ref.dtype)


def gated_merge(xn, ys, w_gate, b_gate, w_branch, tm=1024, tn=256):
    m, d = xn.shape
    bw = ys[0].shape[1]
    y_spec = pl.BlockSpec((tm, bw), lambda i, j: (i, 0))
    return pl.pallas_call(
        _merge_kernel,
        grid=(m // tm, d // tn),
        in_specs=[pl.BlockSpec((tm, d), lambda i, j: (i, 0)), y_spec, y_spec, y_spec, y_spec,
                  pl.BlockSpec((N_BRANCH, d, tn), lambda i, j: (0, 0, j)),
                  pl.BlockSpec((N_BRANCH, 1, tn), lambda i, j: (0, 0, j)),
                  pl.BlockSpec((N_BRANCH, bw, tn), lambda i, j: (0, 0, j))],
        out_specs=pl.BlockSpec((tm, tn), lambda i, j: (i, j)),
        out_shape=jax.ShapeDtypeStruct((m, d), BF16),
        compiler_params=_params(("parallel", "arbitrary"), 48),
    )(xn, *ys, w_gate, b_gate.reshape(N_BRANCH, 1, d), w_branch)


def _s5_kernel(u_ref, perm_ref, permt_ref, bbd_ref, pwr_ref, pwi_ref, cbd_ref, d_ref, wglu_ref, o_ref,
               x_ref, st_ref, *, tt):
    n = S5_GROUPS * S5_STATE
    sl = tt // 8
    lc = 512

    @pl.when(pl.program_id(1) == 0)
    def _():
        st_ref[...] = jnp.zeros_like(st_ref)

    u = u_ref[...]
    u_perm = jnp.dot(perm_ref[...], u.astype(BF16), preferred_element_type=F32).astype(BF16)
    x_ref[...] = jnp.dot(u_perm, bbd_ref[...], preferred_element_type=F32)

    for c0 in range(0, n, lc):
        re = slice(c0, c0 + lc)
        im = slice(n + c0, n + c0 + lc)
        ar = pwr_ref[0:1, re]
        ai = pwi_ref[0:1, re]
        xr = jnp.zeros((8, lc), F32)
        xi = jnp.zeros((8, lc), F32)
        for j in range(sl):
            rows = slice(8 * j, 8 * j + 8)
            nr = ar * xr - ai * xi + x_ref[rows, re]
            ni = ar * xi + ai * xr + x_ref[rows, im]
            xr, xi = nr, ni
            x_ref[rows, re] = xr
            x_ref[rows, im] = xi

        alr = pwr_ref[sl - 1:sl, re]
        ali = pwi_ref[sl - 1:sl, re]
        cr = [st_ref[0:1, re]]
        ci = [st_ref[1:2, re]]
        for s in range(1, 9):
            pr, pi = cr[-1], ci[-1]
            cr.append(xr[s - 1:s] + alr * pr - ali * pi)
            ci.append(xi[s - 1:s] + alr * pi + ali * pr)
        st_ref[0:1, re] = cr[8]
        st_ref[1:2, re] = ci[8]
        car = jnp.concatenate(cr[:8], axis=0)
        cai = jnp.concatenate(ci[:8], axis=0)
        for j in range(sl):
            rows = slice(8 * j, 8 * j + 8)
            pr = pwr_ref[j:j + 1, re]
            pi = pwi_ref[j:j + 1, re]
            x_ref[rows, re] = x_ref[rows, re] + (pr * car - pi * cai)
            x_ref[rows, im] = x_ref[rows, im] + (pr * cai + pi * car)

    y_perm = jnp.dot(x_ref[...].astype(BF16), cbd_ref[...], preferred_element_type=F32)
    y = jnp.dot(permt_ref[...], y_perm, precision=lax.Precision.HIGHEST, preferred_element_type=F32) + d_ref[...] * u
    z = jax.nn.gelu(y)
    gate = jnp.dot(z.astype(BF16), wglu_ref[...], preferred_element_type=F32)
    o_ref[...] = (z * jax.nn.sigmoid(gate)).astype(o_ref.dtype)


S5_TILE = 256


def s5_branch(h_f32, col_block, bsz, seq, bbd, pw_re, pw_im, cbd, d_skip, w_glu, tt=S5_TILE):
    n = S5_GROUPS * S5_STATE
    nt = seq // tt
    sl = tt // 8
    row = jnp.arange(tt)
    perm = jax.nn.one_hot((row % 8) * sl + row // 8, tt, dtype=F32)
    const = lambda shape: pl.BlockSpec(shape, lambda b, t: (0, 0))
    return pl.pallas_call(
        functools.partial(_s5_kernel, tt=tt),
        grid=(bsz, nt),
        in_specs=[pl.BlockSpec((tt, S5_WIDTH), lambda b, t: (b * nt + t, col_block)),
                  const((tt, tt)), const((tt, tt)),
                  const((S5_WIDTH, 2 * n)), const((sl, n)), const((sl, n)), const((2 * n, S5_WIDTH)),
                  const((1, S5_WIDTH)), const((S5_WIDTH, S5_WIDTH))],
        out_specs=pl.BlockSpec((tt, S5_WIDTH), lambda b, t: (b * nt + t, 0)),
        out_shape=jax.ShapeDtypeStruct((bsz * seq, S5_WIDTH), BF16),
        scratch_shapes=[pltpu.VMEM((tt, 2 * n), F32), pltpu.VMEM((8, n), F32)],
        compiler_params=_params(("arbitrary", "arbitrary"), 48),
    )(h_f32, perm.astype(BF16), perm.T, bbd, pw_re, pw_im, cbd, d_skip, w_glu)


def s5_tables(a_re, a_im, log_dt, b_re, b_im, c_re, c_im, n_pow=S5_TILE // 8):
    g, p = a_re.shape
    dt = jnp.exp(log_dt)[:, None]
    mag = jnp.exp(a_re * dt)
    ab_re, ab_im = mag * jnp.cos(a_im * dt), mag * jnp.sin(a_im * dt)
    steps = jnp.arange(1, n_pow + 1, dtype=F32)[:, None, None]
    pw_mag = jnp.exp(a_re * dt * steps)
    pw_re = (pw_mag * jnp.cos(a_im * dt * steps)).reshape(n_pow, g * p)
    pw_im = (pw_mag * jnp.sin(a_im * dt * steps)).reshape(n_pow, g * p)
    den = a_re * a_re + a_im * a_im
    nr, ni = ab_re - 1.0, ab_im
    coef_re = (nr * a_re + ni * a_im) / den
    coef_im = (ni * a_re - nr * a_im) / den
    bb_re = coef_re[..., None] * b_re - coef_im[..., None] * b_im
    bb_im = coef_re[..., None] * b_im + coef_im[..., None] * b_re
    eye = jnp.eye(g, dtype=F32)
    to_bd_in = lambda t: jnp.einsum('gpc,gh->gchp', t, eye).reshape(g * S5_GROUP, g * p)
    to_bd_out = lambda t: jnp.einsum('gcp,gh->gphc', t, eye).reshape(g * p, g * S5_GROUP)
    bbd = jnp.concatenate([to_bd_in(bb_re), to_bd_in(bb_im)], axis=1).astype(BF16)
    cbd = jnp.concatenate([to_bd_out(c_re), to_bd_out(-c_im)], axis=0).astype(BF16)
    return bbd, pw_re, pw_im, cbd


def _log_sigmoid(x):
    return jnp.minimum(x, 0.0) - jnp.log1p(jnp.exp(-jnp.abs(x)))


def _mlstm_kernel(qk_ref, v_ref, og_ref, gate_ref, gbias_ref, convw_ref, o_ref,
                  xbuf_ref, c_ref, n_ref, m_ref, *, tt):
    L = MLSTM_L
    hd = MLSTM_DIM
    nh = MLSTM_HEADS
    i_col, f_col = MISC_I_COL, MISC_F_COL

    @pl.when(pl.program_id(1) == 0)
    def _():
        xbuf_ref[0:8, :] = jnp.zeros((8, 2 * nh * hd), F32)
        c_ref[...] = jnp.zeros_like(c_ref)
        n_ref[...] = jnp.zeros_like(n_ref)
        m_ref[...] = jnp.zeros_like(m_ref)

    xbuf_ref[8:8 + tt, :] = qk_ref[...]
    cw = convw_ref[...]
    conv = xbuf_ref[8:8 + tt, :] * cw[CONV_WIDTH - 1:CONV_WIDTH]
    for sh in range(1, CONV_WIDTH):
        conv = conv + xbuf_ref[8 - sh:8 - sh + tt, :] * cw[CONV_WIDTH - 1 - sh:CONV_WIDTH - sh]
    xbuf_ref[0:8, :] = xbuf_ref[tt:tt + 8, :]
    qk = conv * jax.nn.sigmoid(conv)

    gates = gate_ref[...] + gbias_ref[...]
    logf = _log_sigmoid(gates)
    row = lax.broadcasted_iota(jnp.int32, (L, L), 0)
    col = lax.broadcasted_iota(jnp.int32, (L, L), 1)
    causal = col <= row
    tri = jnp.where(causal, 1.0, 0.0).astype(F32)

    for c in range(tt // L):
        r0 = c * L
        gc = gates[r0:r0 + L]
        bcum = jnp.dot(tri, logf[r0:r0 + L], precision=lax.Precision.HIGHEST, preferred_element_type=F32)
        gct = gc.T
        bcumt = bcum.T
        for h in range(nh):
            ig_col = gc[:, i_col + h:i_col + h + 1]
            b_col = bcum[:, f_col + h:f_col + h + 1]
            ig_row = gct[i_col + h:i_col + h + 1, :]
            b_row = bcumt[f_col + h:f_col + h + 1, :]
            b_last = b_col[L - 1:L, :]
            q = qk[r0:r0 + L, h * hd:(h + 1) * hd]
            k = qk[r0:r0 + L, (nh + h) * hd:(nh + h + 1) * hd] * (hd ** -0.5)
            vb = v_ref[r0:r0 + L, h * hd:(h + 1) * hd].astype(BF16)
            qb = q.astype(BF16)
            kb = k.astype(BF16)
            c_prev = c_ref[h]
            n_prev = n_ref[h]
            m_prev = m_ref[h][:, 0:1]

            g = b_last - b_col + ig_col
            m_loc = jnp.max(g, axis=0, keepdims=True)
            wk = jnp.exp(g - m_loc) * k
            c_loc = jnp.dot(wk.T.astype(BF16), vb, preferred_element_type=F32)
            n_loc = jnp.sum(wk, axis=0, keepdims=True)

            m_inter = b_col + m_prev
            log_d = jnp.where(causal, b_col - b_row + ig_row, -jnp.inf)
            m_j = jnp.maximum(jnp.max(log_d, axis=1, keepdims=True), m_inter)
            dqk = jnp.exp(log_d - m_j) * lax.dot_general(qb, kb, NT_DIMS, preferred_element_type=F32)
            inter = jnp.exp(m_inter - m_j)
            num = (jnp.dot(dqk.astype(BF16), vb, preferred_element_type=F32)
                   + inter * jnp.dot(qb, c_prev.astype(BF16), preferred_element_type=F32))
            den = jnp.sum(dqk, axis=1, keepdims=True) + inter * jnp.sum(q * n_prev, axis=1, keepdims=True)
            hid = num / jnp.maximum(jnp.abs(den), jnp.exp(-m_j))
            og = jax.nn.sigmoid(og_ref[r0:r0 + L, h * hd:(h + 1) * hd])
            o_ref[r0:r0 + L, h * hd:(h + 1) * hd] = (og * hid).astype(o_ref.dtype)

            m_new = jnp.maximum(b_last + m_prev, m_loc)
            a = jnp.exp(b_last + m_prev - m_new)
            bcoef = jnp.exp(m_loc - m_new)
            c_ref[h] = a * c_prev + bcoef * c_loc
            n_ref[h] = a * n_prev + bcoef * n_loc
            m_ref[h] = jnp.broadcast_to(m_new, (1, LANES))


def mlstm_branch(h_f32, misc, gate_bias, conv_w, bsz, seq, qk_blk, v_blk, o_blk, tt=512):
    nh, hd = MLSTM_HEADS, MLSTM_DIM
    w = nh * hd
    nt = seq // tt
    return pl.pallas_call(
        functools.partial(_mlstm_kernel, tt=tt),
        grid=(bsz, nt),
        in_specs=[pl.BlockSpec((tt, 2 * w), lambda b, t: (b * nt + t, qk_blk)),
                  pl.BlockSpec((tt, w), lambda b, t: (b * nt + t, v_blk)),
                  pl.BlockSpec((tt, w), lambda b, t: (b * nt + t, o_blk)),
                  pl.BlockSpec((tt, LANES), lambda b, t: (b * nt + t, 0)),
                  pl.BlockSpec((1, LANES), lambda b, t: (0, 0)),
                  pl.BlockSpec((CONV_WIDTH, 2 * w), lambda b, t: (0, 0))],
        out_specs=pl.BlockSpec((tt, w), lambda b, t: (b * nt + t, 0)),
        out_shape=jax.ShapeDtypeStruct((bsz * seq, w), BF16),
        scratch_shapes=[pltpu.VMEM((tt + 8, 2 * w), F32), pltpu.VMEM((nh, hd, hd), F32),
                        pltpu.VMEM((nh, 1, hd), F32), pltpu.VMEM((nh, 1, LANES), F32)],
        compiler_params=_params(("arbitrary", "arbitrary"), 48),
    )(h_f32, h_f32, h_f32, misc, gate_bias, conv_w)


def rel_bucket(dist):
    max_exact = REL_BUCKETS // 2
    n = jnp.maximum(dist, 0)
    large = max_exact + (jnp.log(jnp.maximum(n, 1).astype(F32) / max_exact)
                         / math.log(REL_MAX_DIST / max_exact)
                         * (REL_BUCKETS - max_exact)).astype(jnp.int32)
    large = jnp.minimum(large, REL_BUCKETS - 1)
    return jnp.where(n < max_exact, n, large)


def bias_tiles(rel, t):
    assert t >= REL_MAX_DIST
    kk = jnp.arange(t)[:, None]
    qq = jnp.arange(t)[None, :]

    def lookup(dist):
        onehot = (rel_bucket(dist)[..., None] == jnp.arange(REL_BUCKETS)).astype(F32)
        return jnp.einsum('kqb,bh->hkq', onehot, rel.astype(F32), precision=lax.Precision.HIGHEST) * LOG2E

    t0 = jnp.where((qq >= kk)[None], lookup(qq - kk), NEG)
    t1 = lookup(t + qq - kk)
    t2 = lookup(jnp.full((t, t), 2 * t))
    return jnp.stack([t0, t1, t2])


def tri_pairs(nq):
    qi = [q for q in range(nq) for _ in range(q + 1)]
    kj = [k for q in range(nq) for k in range(q + 1)]
    return jnp.asarray(qi, jnp.int32), jnp.asarray(kj, jnp.int32)


def values_t_ext(h_bf16, col0, bsz, seq, heads, dv):
    v = h_bf16[:, col0:col0 + heads * dv].reshape(bsz, seq, heads, dv)
    vt = jnp.transpose(v, (0, 2, 3, 1))
    ones = jnp.ones((bsz, heads, V_ONES, seq), BF16)
    return jnp.concatenate([vt, ones], axis=2).reshape(bsz * heads * (dv + V_ONES), seq)


def _attn_maps(qi, kj, maps, bias_ref, mb_ref, s_ref, p_ref, m_ref, acc_ref):
    def run(const_bias):
        def qk(i):
            load_k, load_q, _, head = maps[i]
            s = lax.dot_general(load_k(), load_q(), NT_DIMS, preferred_element_type=F32)
            if not const_bias:
                s = s + bias_ref[0, head]
            if mb_ref is not None:
                s = s + mb_ref[...]
            s_ref[i % 2] = s
            mx = [s[c * 8:(c + 1) * 8, :] for c in range(4)]
            for c in range(4, s.shape[0] // 8):
                mx[c % 4] = jnp.maximum(mx[c % 4], s[c * 8:(c + 1) * 8, :])
            return jnp.max(jnp.maximum(jnp.maximum(mx[0], mx[1]), jnp.maximum(mx[2], mx[3])), axis=0, keepdims=True)

        m_tile = qk(0)
        for i, (_, _, load_vt, head) in enumerate(maps):
            m_next = qk(i + 1) if i + 1 < len(maps) else None
            _softmax_pv(i, load_vt, m_tile, bias_ref[0, head, 0:1, :] if const_bias else None,
                        s_ref, p_ref, m_ref, acc_ref)
            m_tile = m_next

    far = qi - kj >= 2

    @pl.when(far)
    def _():
        run(True)

    @pl.when(jnp.logical_not(far))
    def _():
        run(False)


def _softmax_pv(idx, load_vt, m_tile, bias_c, s_ref, p_ref, m_ref, acc_ref):
    tk, tq = s_ref.shape[1:]
    sb = s_ref.at[idx % 2]
    pb = p_ref.at[idx % 2]
    m_prev = m_ref[idx]
    if bias_c is not None:
        m_new = jnp.maximum(m_prev, m_tile + bias_c)
        m_b = jnp.broadcast_to(m_new - bias_c, (16, tq))
    else:
        m_new = jnp.maximum(m_prev, m_tile)
        m_b = jnp.broadcast_to(m_new, (16, tq))
    m_ref[idx] = m_new
    for c in range(tk // 16):
        pb[c * 16:(c + 1) * 16, :] = jnp.exp2(sb[c * 16:(c + 1) * 16, :] - m_b).astype(BF16)
    alpha = jnp.exp2(m_prev - m_new)
    acc_ref[idx] = alpha * acc_ref[idx] + jnp.dot(load_vt(), pb[...], preferred_element_type=F32)


def _init_softmax_state(m_ref, acc_ref):
    m_ref[...] = jnp.full(m_ref.shape, NEG, F32)
    acc_ref[...] = jnp.zeros_like(acc_ref)


def _diff_attn_kernel(qi_ref, kj_ref, q_ref, k_ref, vt_ref, bias_ref, lam_ref, gb_ref, o_ref,
                      s_ref, p_ref, m_ref, acc_ref, *, out_scale):
    p = pl.program_id(1)
    qi = qi_ref[p]
    kj = kj_ref[p]
    dq = DIFF_QK_DIM
    dv = DIFF_V_DIM
    vrows = dv + V_ONES

    @pl.when(kj == 0)
    def _():
        _init_softmax_state(m_ref, acc_ref)

    def load_q(h, mi):
        qh = q_ref[:, h * 2 * dq:(h + 1) * 2 * dq]
        lane = lax.broadcasted_iota(jnp.int32, qh.shape, 1)
        return jnp.where((lane < dq) if mi == 0 else (lane >= dq), qh, jnp.zeros_like(qh))

    maps = [(lambda h=h: k_ref[:, h * 2 * dq:(h + 1) * 2 * dq],
             functools.partial(load_q, h, mi),
             lambda h=h: vt_ref[h * vrows:(h + 1) * vrows, :], h)
            for h in range(DIFF_HEADS) for mi in range(2)]
    _attn_maps(qi, kj, maps, bias_ref, None, s_ref, p_ref, m_ref, acc_ref)

    @pl.when(kj == qi)
    def _():
        lam = lam_ref[0:1, 0:1]
        for h in range(DIFF_HEADS):
            a1 = acc_ref[2 * h]
            a2 = acc_ref[2 * h + 1]
            a = a1[0:dv] / a1[dv:dv + 1] - lam * (a2[0:dv] / a2[dv:dv + 1])
            y = a * lax.rsqrt(jnp.mean(a * a, axis=0, keepdims=True) + EPS) * gb_ref[...] * out_scale
            o_ref[:, h * dv:(h + 1) * dv] = y.T.astype(o_ref.dtype)


def diff_attention(h_bf16, vt_ext, bsz, seq, q_blk, k_blk, bias, lam_row, subln_g, out_scale, t=512):
    nq = seq // t
    qi, kj = tri_pairs(nq)
    w = DIFF_HEADS * DIFF_V_DIM
    nmap = 2 * DIFF_HEADS
    vrows = DIFF_HEADS * (DIFF_V_DIM + V_ONES)
    g_b = jnp.broadcast_to(subln_g.reshape(DIFF_V_DIM, 1), (DIFF_V_DIM, t))
    grid_spec = pltpu.PrefetchScalarGridSpec(
        num_scalar_prefetch=2,
        grid=(bsz, qi.shape[0]),
        in_specs=[pl.BlockSpec((t, w), lambda b, p, qi, kj: (b * nq + qi[p], q_blk)),
                  pl.BlockSpec((t, w), lambda b, p, qi, kj: (b * nq + kj[p], k_blk)),
                  pl.BlockSpec((vrows, t), lambda b, p, qi, kj: (b, kj[p])),
                  pl.BlockSpec((1, DIFF_HEADS, t, t),
                               lambda b, p, qi, kj: (jnp.minimum(qi[p] - kj[p], 2), 0, 0, 0)),
                  pl.BlockSpec((1, LANES), lambda b, p, qi, kj: (0, 0)),
                  pl.BlockSpec((DIFF_V_DIM, t), lambda b, p, qi, kj: (0, 0))],
        out_specs=pl.BlockSpec((t, w), lambda b, p, qi, kj: (b * nq + qi[p], 0)),
        scratch_shapes=[pltpu.VMEM((2, t, t), F32), pltpu.VMEM((2, t, t), BF16),
                        pltpu.VMEM((nmap, 1, t), F32), pltpu.VMEM((nmap, DIFF_V_DIM + V_ONES, t), F32)],
    )
    return pl.pallas_call(
        functools.partial(_diff_attn_kernel, out_scale=out_scale),
        grid_spec=grid_spec,
        out_shape=jax.ShapeDtypeStruct((bsz * seq, w), BF16),
        compiler_params=_params(("arbitrary", "arbitrary"), 48),
    )(qi, kj, h_bf16, h_bf16, vt_ext, bias, lam_row, g_b)


def _sort_key(x):
    b = lax.bitcast_convert_type(x, jnp.int32)
    return b ^ ((b >> 31) & 0x7FFFFFFF)


def _batcher_pairs(n):
    def merge(lo, hi, r):
        step = r * 2
        if step < hi - lo:
            yield from merge(lo, hi, step)
            yield from merge(lo + r, hi, step)
            yield from [(a, a + r) for a in range(lo + r, hi - r, step)]
        else:
            yield (lo, lo + r)

    def sort(lo, hi):
        if hi - lo >= 1:
            mid = lo + (hi - lo) // 2
            yield from sort(lo, mid)
            yield from sort(mid + 1, hi)
            yield from merge(lo, hi, 1)

    return tuple(sort(0, n - 1))


SORT_GROUP = 16


def _count_ge_sorted(v, cand):
    g = len(v)
    out, masks, level = [], [], 1
    while (1 << level) <= g:
        pivots = [v[(2 * k + 1) * g // (1 << level) - 1] for k in range(1 << (level - 1))]

        def pick(lo, count, bit):
            if count == 1:
                return pivots[lo]
            half = count // 2
            return jnp.where(masks[bit], pick(lo, half, bit + 1), pick(lo + half, half, bit + 1))

        masks.append(pick(0, len(pivots), 0) >= cand)
        out.append((masks[-1], float(g >> level)))
        level += 1
    out.append((v[g - 1] >= cand, 1.0))
    return out


def _dsa_select_kernel(iq_ref, w_ref, ikz_ref, o_ref, key_ref, srt_ref, a_ref, wb_ref, *, tq, tk, topk, seq):
    i = pl.program_id(1)
    last = (i * tq + tq - 1) // tk
    idx_scale = (IDX_DIM * IDX_HEADS) ** -0.5
    iw_row = MISC_IW_COL
    grp = SORT_GROUP
    pairs = _batcher_pairs(grp)

    wt = w_ref[...].T
    for h in range(IDX_HEADS):
        wb_ref[h] = jnp.broadcast_to(wt[iw_row + h:iw_row + h + 1, :], (8, tq))

    def in_prefix(off, n):
        key_pos = off + lax.broadcasted_iota(jnp.int32, (n, tq), 0)
        q_pos = i * tq + lax.broadcasted_iota(jnp.int32, (n, tq), 1)
        return key_pos <= q_pos

    q_stack = [jnp.concatenate([iq_ref[:, (2 * pg) * LANES:(2 * pg + 1) * LANES],
                                iq_ref[:, (2 * pg + 1) * LANES:(2 * pg + 2) * LANES]], axis=0) for pg in range(2)]

    def head_dots(j, slot):
        off = pl.multiple_of(j * tk, tk)
        for half in range(2):
            kt = ikz_ref[pl.ds(off, tk), half * LANES:(half + 1) * LANES]
            for pg in range(2):
                a_ref[slot, 2 * half + pg] = lax.dot_general(kt, q_stack[pg], NT_DIMS, preferred_element_type=F32)

    def score_tile(j, slot, diagonal):
        off = pl.multiple_of(j * tk, tk)
        for c in range(tk // 8):
            r0 = c * 8
            sc = None
            for half in range(2):
                for pg in range(2):
                    for s in range(2):
                        head = 2 * (2 * pg + s) + half
                        term = wb_ref[head] * jnp.maximum(
                            a_ref[slot, 2 * half + pg, r0:r0 + 8, s * tq:(s + 1) * tq], 0.0)
                        sc = term if sc is None else sc + term
            sc = sc * idx_scale
            if diagonal:
                sc = jnp.where(in_prefix(off + r0, 8), sc, -jnp.inf)
            key_ref[pl.ds(pl.multiple_of(off + r0, 8), 8), :] = _sort_key(sc)
        for g in range(tk // (8 * grp)):
            base = off + g * grp * 8
            keys = [key_ref[pl.ds(pl.multiple_of(base + c * 8, 8), 8), :] for c in range(grp)]
            for lo, hi in pairs:
                keys[lo], keys[hi] = jnp.minimum(keys[lo], keys[hi]), jnp.maximum(keys[lo], keys[hi])
            for c in range(grp):
                srt_ref[pl.ds(pl.multiple_of(base + c * 8, 8), 8), :] = keys[c]

    def score_body(p, carry):
        head_dots(2 * p + 1, 1)
        score_tile(2 * p, 0, False)
        head_dots(2 * p + 2, 0)
        score_tile(2 * p + 1, 1, False)
        return carry

    head_dots(0, 0)
    lax.fori_loop(0, last // 2, score_body, 0)

    @pl.when(last % 2 == 0)
    def _():
        score_tile(last, 0, True)

    @pl.when(last % 2 == 1)
    def _():
        head_dots(last, 1)
        score_tile(last - 1, 0, False)
        score_tile(last, 1, True)

    n_pairs = (last + 2) // 2

    @pl.when(n_pairs * 2 > last + 1)
    def _():
        srt_ref[pl.ds(pl.multiple_of((last + 1) * tk, tk), tk), :] = jnp.full((tk, tq), INT_MIN, jnp.int32)

    n_terms = grp.bit_length()

    def count_ge(cand):
        cand_b = jnp.broadcast_to(cand, (8, tq))

        def count_body(j, accs):
            blk = srt_ref[pl.ds(pl.multiple_of(j * 2 * tk, 2 * tk), 2 * tk), :]
            accs = list(accs)
            for g in range(2 * tk // (8 * grp)):
                run = [blk[(g * grp + c) * 8:(g * grp + c + 1) * 8, :] for c in range(grp)]
                for t, (mask, _) in enumerate(_count_ge_sorted(run, cand_b)):
                    accs[t] = accs[t] + jnp.where(mask, 1.0, 0.0)
            return tuple(accs)

        zero = jnp.zeros((8, tq), F32)
        accs = lax.fori_loop(0, n_pairs, count_body, (zero,) * n_terms)
        weights = [float(grp >> (t + 1)) for t in range(n_terms - 1)] + [1.0]
        total = accs[0] * weights[0]
        for t in range(1, n_terms):
            total = total + accs[t] * weights[t]
        return jnp.sum(total, axis=0, keepdims=True)

    def bit_body(it, carry):
        thr, cnt_thr = carry
        cand = thr + lax.shift_left(jnp.int32(1), 31 - it)
        cnt = count_ge(cand)
        ok = cnt >= topk
        return jnp.where(ok, cand, thr), jnp.where(ok, cnt, cnt_thr)

    thr, cnt_thr = lax.fori_loop(0, 32, bit_body, (jnp.full((1, tq), INT_MIN, jnp.int32),
                                                   jnp.full((1, tq), 2.0 * tk, F32) * n_pairs.astype(F32)))
    thr_b = jnp.broadcast_to(thr, (16, tq))

    q_pos = i * tq + lax.broadcasted_iota(jnp.int32, (1, tq), 1)
    excess = jnp.where(q_pos + 1 > topk, cnt_thr - topk, 0.0)
    any_excess = jnp.max(excess) > 0.5

    def out_tile(j, diagonal):
        off = pl.multiple_of(j * tk, tk)
        for c in range(tk // 16):
            rows = pl.ds(pl.multiple_of(off + c * 16, 16), 16)
            keep = key_ref[rows, :] >= thr_b
            if diagonal:
                keep = keep & in_prefix(off + c * 16, 16)
            o_ref[0, rows, :] = jnp.where(keep, 0.0, NEG).astype(o_ref.dtype)

    def out_body(j, carry):
        out_tile(j, False)
        return carry

    @pl.when(jnp.logical_not(any_excess))
    def _():
        lax.fori_loop(0, last, out_body, 0)
        out_tile(last, True)

    @pl.when(any_excess)
    def _():
        ties_kept = topk - count_ge(thr + 1)
        sub = lax.broadcasted_iota(jnp.int32, (8, tq), 0)
        thr8 = jnp.broadcast_to(thr, (8, tq))

        def prefix8(x):
            for sh in (1, 2, 4):
                x = x + jnp.where(sub >= sh, pltpu.roll(x, sh, 0), 0.0)
            return x

        def out_tile_ties(j, diagonal, seen):
            off = pl.multiple_of(j * tk, tk)
            for c in range(tk // 16):
                halves = []
                for hh in range(2):
                    r0 = off + c * 16 + hh * 8
                    key = key_ref[pl.ds(pl.multiple_of(r0, 8), 8), :]
                    tie = key == thr8
                    rank = prefix8(jnp.where(tie, 1.0, 0.0)) + seen
                    keep = (key > thr8) | (tie & (rank <= ties_kept))
                    if diagonal:
                        keep = keep & in_prefix(r0, 8)
                    halves.append(jnp.where(keep, 0.0, NEG))
                    seen = rank[7:8, :]
                o_ref[0, pl.ds(pl.multiple_of(off + c * 16, 16), 16), :] = jnp.concatenate(
                    halves, axis=0).astype(o_ref.dtype)
            return seen

        seen = lax.fori_loop(0, last, lambda j, s: out_tile_ties(j, False, s), jnp.zeros((1, tq), F32))
        out_tile_ties(last, True, seen)

    def fill_body(j, carry):
        o_ref[0, pl.ds(pl.multiple_of(j * tk, tk), tk), :] = jnp.full((tk, tq), NEG, o_ref.dtype)
        return carry

    lax.fori_loop(last + 1, seq // tk, fill_body, 0)


def dsa_select(h_bf16, misc, ikz, bsz, seq, iq_blk, tq=128, tk=512):
    nq = seq // tq
    topk = min(TOPK_MAX, seq // 4)
    w = IDX_HEADS * IDX_DIM
    return pl.pallas_call(
        functools.partial(_dsa_select_kernel, tq=tq, tk=tk, topk=topk, seq=seq),
        grid=(bsz, nq),
        in_specs=[pl.BlockSpec((tq, w), lambda b, i: (b * nq + i, iq_blk)),
                  pl.BlockSpec((tq, LANES), lambda b, i: (b * nq + i, 0)),
                  pl.BlockSpec((seq, 2 * LANES), lambda b, i: (b, 0), pipeline_mode=pl.Buffered(1))],
        out_specs=pl.BlockSpec((1, seq, tq), lambda b, i: (b, 0, i)),
        out_shape=jax.ShapeDtypeStruct((bsz, seq, seq), BF16),
        scratch_shapes=[pltpu.VMEM((seq, tq), jnp.int32), pltpu.VMEM((seq, tq), jnp.int32),
                        pltpu.VMEM((2, IDX_HEADS // 2, tk, 2 * tq), F32), pltpu.VMEM((IDX_HEADS, 8, tq), F32)],
        compiler_params=_params(("arbitrary", "arbitrary"), 48),
    )(h_bf16, misc, ikz)


def _dsa_attn_kernel(qi_ref, kj_ref, q_ref, k_ref, vt_ref, bias_ref, mask_ref, o_ref,
                     s_ref, p_ref, mb_ref, m_ref, acc_ref):
    p = pl.program_id(1)
    qi = qi_ref[p]
    kj = kj_ref[p]
    dh = HEAD_DIM
    vrows = dh + V_ONES

    @pl.when(kj == 0)
    def _():
        _init_softmax_state(m_ref, acc_ref)

    mb_ref[...] = mask_ref[0].astype(F32)
    maps = [(lambda h=h: k_ref[:, h * dh:(h + 1) * dh], lambda h=h: q_ref[:, h * dh:(h + 1) * dh],
             lambda h=h: vt_ref[h * vrows:(h + 1) * vrows, :], h) for h in range(DSA_HEADS)]
    _attn_maps(qi, kj, maps, bias_ref, mb_ref, s_ref, p_ref, m_ref, acc_ref)

    @pl.when(kj == qi)
    def _():
        for h in range(DSA_HEADS):
            a = acc_ref[h]
            o_ref[:, h * dh:(h + 1) * dh] = (a[0:dh] / a[dh:dh + 1]).T.astype(o_ref.dtype)


def dsa_attention(h_bf16, vt_ext, mask, bsz, seq, q_blk, k_blk, bias, t=512):
    nq = seq // t
    qi, kj = tri_pairs(nq)
    w = DSA_HEADS * HEAD_DIM
    vrows = DSA_HEADS * (HEAD_DIM + V_ONES)
    grid_spec = pltpu.PrefetchScalarGridSpec(
        num_scalar_prefetch=2,
        grid=(bsz, qi.shape[0]),
        in_specs=[pl.BlockSpec((t, w), lambda b, p, qi, kj: (b * nq + qi[p], q_blk)),
                  pl.BlockSpec((t, w), lambda b, p, qi, kj: (b * nq + kj[p], k_blk)),
                  pl.BlockSpec((vrows, t), lambda b, p, qi, kj: (b, kj[p])),
                  pl.BlockSpec((1, DSA_HEADS, t, t),
                               lambda b, p, qi, kj: (jnp.minimum(qi[p] - kj[p], 2), 0, 0, 0)),
                  pl.BlockSpec((1, t, t), lambda b, p, qi, kj: (b, kj[p], qi[p]))],
        out_specs=pl.BlockSpec((t, w), lambda b, p, qi, kj: (b * nq + qi[p], 0)),
        scratch_shapes=[pltpu.VMEM((2, t, t), F32), pltpu.VMEM((2, t, t), BF16), pltpu.VMEM((t, t), F32),
                        pltpu.VMEM((DSA_HEADS, 1, t), F32), pltpu.VMEM((DSA_HEADS, HEAD_DIM + V_ONES, t), F32)],
    )
    return pl.pallas_call(
        _dsa_attn_kernel,
        grid_spec=grid_spec,
        out_shape=jax.ShapeDtypeStruct((bsz * seq, w), BF16),
        compiler_params=_params(("arbitrary", "arbitrary"), 48),
    )(qi, kj, h_bf16, h_bf16, vt_ext, bias, mask)


_SPLITS = (512, 512, 512, 512, 64, 8, 512, 512, 512, 512, 1024, 512, 512, 4, 4)
_NAMES = ('a_q', 'a_k', 'a_v', 'a_iq', 'a_ik', 'a_iw', 'b_u', 'c_q', 'c_k', 'c_v', 'd_qk', 'd_v', 'd_o', 'd_i', 'd_f')


def _split_w_in(w_in):
    out, off = {}, 0
    for name, width in zip(_NAMES, _SPLITS):
        out[name] = w_in[:, off:off + width]
        off += width
    return out


def _group_w_in(w_in):
    c = _split_w_in(w_in)
    d = w_in.shape[0]
    z = lambda n: jnp.zeros((d, n), w_in.dtype)
    w_attn = jnp.concatenate([c['a_q'], c['a_k'], c['a_v'], c['a_iq'], c['c_q'], c['c_k'], c['c_v']], axis=1)
    w_scan = jnp.concatenate([c['d_qk'], c['b_u'], c['d_v'], c['d_o']], axis=1)
    w_misc = jnp.concatenate([c['a_ik'], z(IDX_DIM), z(IDX_DIM), c['a_ik'],
                              z(MISC_IW_COL), c['a_iw'], c['d_i'], c['d_f'],
                              z(LANES - MISC_F_COL - MLSTM_HEADS)], axis=1)
    return w_attn.astype(BF16), w_scan.astype(BF16), w_misc.astype(BF16)


def _pad_ffn(w_in, w_out):
    d = w_in.shape[0]
    pad = D_FF_PAD - D_FF
    zi = jnp.zeros((d, pad), w_in.dtype)
    w_in_pad = jnp.concatenate([w_in[:, :D_FF], zi, w_in[:, D_FF:], zi], axis=1).astype(BF16)
    w_out_pad = jnp.concatenate([w_out, jnp.zeros((pad, d), w_out.dtype)], axis=0).astype(BF16)
    return w_in_pad, w_out_pad


def _branches(x, norm_g, l, bsz, seq, w_in, conv_w, i_bias, f_bias, s5, s5_d, s5_w_glu, diff_lambda,
              diff_subln_g, bias_a, bias_c):
    lambda_init = 0.8 - 0.6 * math.exp(-0.3 * l)
    w_attn, w_scan, w_misc = _group_w_in(w_in)
    ones = jnp.ones((512,), F32)
    attn_scale = jnp.concatenate([ones * (HEAD_DIM ** -0.5 * LOG2E), ones, ones, ones,
                                  ones * (DIFF_QK_DIM ** -0.5 * LOG2E), ones, ones]).reshape(1, -1)
    xn, h_attn = norm_matmul(x, norm_g, w_attn, attn_scale, BF16)
    h_scan = matmul(xn, w_scan, jnp.ones((1, w_scan.shape[1]), F32), F32)
    ikz, misc = misc_proj(xn, w_misc)

    mask = dsa_select(h_attn, misc, ikz, bsz, seq, iq_blk=3)
    vt_a = values_t_ext(h_attn, 2 * 512, bsz, seq, DSA_HEADS, HEAD_DIM)
    y_a = dsa_attention(h_attn, vt_a, mask, bsz, seq, 0, 1, bias_a)

    bbd, abar_re, abar_im, cbd = s5
    y_b = s5_branch(h_scan, 2, bsz, seq, bbd, abar_re, abar_im, cbd, s5_d.reshape(1, -1), s5_w_glu.astype(BF16))

    dl = diff_lambda.astype(F32)
    lam = jnp.exp(jnp.sum(dl[0] * dl[1])) - jnp.exp(jnp.sum(dl[2] * dl[3])) + lambda_init
    lam_row = jnp.full((1, LANES), lam, F32)
    vt_c = values_t_ext(h_attn, 6 * 512, bsz, seq, DIFF_HEADS, DIFF_V_DIM)
    y_c = diff_attention(h_attn, vt_c, bsz, seq, 4, 5, bias_c, lam_row, diff_subln_g, 1.0 - lambda_init)

    gate_bias = (jnp.zeros((1, LANES), F32).at[0, MISC_I_COL:MISC_I_COL + MLSTM_HEADS].set(i_bias)
                 .at[0, MISC_F_COL:MISC_F_COL + MLSTM_HEADS].set(f_bias))
    y_d = mlstm_branch(h_scan, misc, gate_bias, conv_w, bsz, seq, qk_blk=0, v_blk=3, o_blk=4)
    return xn, (y_a, y_b, y_c, y_d)


def _mixer(x, l, bsz, seq, norm_g, w_in, conv_w, i_bias, f_bias, s5, s5_d, s5_w_glu, diff_lambda,
           diff_subln_g, bias_a, bias_c, w_gate, b_gate, w_branch, w_out):
    xn, ys = _branches(x, norm_g, l, bsz, seq, w_in, conv_w, i_bias, f_bias, s5, s5_d, s5_w_glu, diff_lambda,
                       diff_subln_g, bias_a, bias_c)
    merged = gated_merge(xn, ys, w_gate.astype(BF16), b_gate, w_branch.astype(BF16))
    return matmul_residual(merged, w_out.astype(BF16), x)


def kernel(x, norm_g, w_ffn_in, w_ffn_out, w_in, conv_w, mlstm_i_bias, mlstm_f_bias, s5_a_re, s5_a_im,
           s5_log_dt, s5_b_re, s5_b_im, s5_c_re, s5_c_im, s5_d, s5_w_glu, diff_lambda, diff_subln_g,
           rel_table, w_gate, b_gate, w_branch, w_out, final_g):
    bsz, seq, d = x.shape
    t_attn = 512
    bias_a = bias_tiles(rel_table[:, :DSA_HEADS], t_attn)
    bias_c = bias_tiles(rel_table[:, DSA_HEADS:], t_attn)
    xf = x.reshape(bsz * seq, d)
    for l in range(DEPTH):
        xf = ffn_block(xf, norm_g[l, 0], *_pad_ffn(w_ffn_in[l, 0], w_ffn_out[l, 0]))
        s5 = s5_tables(s5_a_re[l], s5_a_im[l], s5_log_dt[l], s5_b_re[l], s5_b_im[l], s5_c_re[l], s5_c_im[l])
        xf = _mixer(xf, l, bsz, seq, norm_g[l, 1], w_in[l], conv_w[l], mlstm_i_bias[l], mlstm_f_bias[l],
                    s5, s5_d[l], s5_w_glu[l], diff_lambda[l], diff_subln_g[l], bias_a, bias_c,
                    w_gate[l], b_gate[l], w_branch[l], w_out[l])
        xf = ffn_block(xf, norm_g[l, 2], *_pad_ffn(w_ffn_in[l, 1], w_ffn_out[l, 1]))
    return rmsnorm(xf, final_g, F32).reshape(bsz, seq, d)
```

```python
import functools
import math

import jax
import jax.numpy as jnp
from jax import lax
from jax.experimental import pallas as pl
from jax.experimental.pallas import tpu as pltpu

F32 = jnp.float32
BF16 = jnp.bfloat16

D_MODEL = 2048
DEPTH = 4
N_BRANCH = 4
BRANCH_WIDTH = 512
HEAD_DIM = 128
DSA_HEADS = 4
IDX_HEADS = 8
IDX_DIM = 64
TOPK_MAX = 256
S5_WIDTH = 512
S5_GROUP = 16
S5_GROUPS = S5_WIDTH // S5_GROUP
S5_STATE = 64
DIFF_HEADS = 4
DIFF_QK_DIM = 64
DIFF_V_DIM = 128
MLSTM_HEADS = 4
MLSTM_DIM = 128
CONV_WIDTH = 4
D_FF = 5504
REL_BUCKETS = 32
REL_MAX_DIST = 128
EPS = 1e-6

LANES = 128
NEG = -1e30
INT_MIN = -(2 ** 31)
D_FF_PAD = 5632
MLSTM_L = 128

MISC_IW_COL = 64
MISC_I_COL = MISC_IW_COL + IDX_HEADS
MISC_F_COL = MISC_I_COL + MLSTM_HEADS

V_ONES = 16
LOG2E = math.log2(math.e)

NT_DIMS = (((1,), (1,)), ((), ()))


def _params(sem, vmem_mb):
    return pltpu.CompilerParams(dimension_semantics=sem, vmem_limit_bytes=vmem_mb * 1024 * 1024)


def _rms(x, g):
    return x * lax.rsqrt(jnp.mean(x * x, axis=-1, keepdims=True) + EPS) * g


def _rmsnorm_kernel(x_ref, g_ref, o_ref):
    o_ref[...] = _rms(x_ref[...], g_ref[...]).astype(o_ref.dtype)


def rmsnorm(x, g, out_dtype, tm=512):
    m, d = x.shape
    return pl.pallas_call(
        _rmsnorm_kernel,
        grid=(m // tm,),
        in_specs=[pl.BlockSpec((tm, d), lambda i: (i, 0)), pl.BlockSpec((1, d), lambda i: (0, 0))],
        out_specs=pl.BlockSpec((tm, d), lambda i: (i, 0)),
        out_shape=jax.ShapeDtypeStruct((m, d), out_dtype),
        compiler_params=_params(("parallel",), 40),
    )(x, g.reshape(1, d))


def _mm_kernel(x_ref, w_ref, cs_ref, o_ref):
    acc = jnp.dot(x_ref[...], w_ref[...], preferred_element_type=F32)
    o_ref[...] = (acc * cs_ref[...]).astype(o_ref.dtype)


def matmul(x, w, col_scale, out_dtype, tm=1024, tn=512):
    m, k = x.shape
    n = w.shape[1]
    return pl.pallas_call(
        _mm_kernel,
        grid=(m // tm, n // tn),
        in_specs=[pl.BlockSpec((tm, k), lambda i, j: (i, 0)), pl.BlockSpec((k, tn), lambda i, j: (0, j)),
                  pl.BlockSpec((1, tn), lambda i, j: (0, j))],
        out_specs=pl.BlockSpec((tm, tn), lambda i, j: (i, j)),
        out_shape=jax.ShapeDtypeStruct((m, n), out_dtype),
        compiler_params=_params(("parallel", "arbitrary"), 40),
    )(x, w, col_scale)


def _norm_mm_kernel(x_ref, g_ref, w_ref, cs_ref, xn_ref, o_ref):
    @pl.when(pl.program_id(1) == 0)
    def _():
        xn_ref[...] = _rms(x_ref[...], g_ref[...]).astype(xn_ref.dtype)

    acc = jnp.dot(xn_ref[...], w_ref[...], preferred_element_type=F32)
    o_ref[...] = (acc * cs_ref[...]).astype(o_ref.dtype)


def norm_matmul(x, g, w, col_scale, out_dtype, tm=1024, tn=512):
    m, k = x.shape
    n = w.shape[1]
    return pl.pallas_call(
        _norm_mm_kernel,
        grid=(m // tm, n // tn),
        in_specs=[pl.BlockSpec((tm, k), lambda i, j: (i, 0)), pl.BlockSpec((1, k), lambda i, j: (0, 0)),
                  pl.BlockSpec((k, tn), lambda i, j: (0, j)), pl.BlockSpec((1, tn), lambda i, j: (0, j))],
        out_specs=[pl.BlockSpec((tm, k), lambda i, j: (i, 0)), pl.BlockSpec((tm, tn), lambda i, j: (i, j))],
        out_shape=[jax.ShapeDtypeStruct((m, k), BF16), jax.ShapeDtypeStruct((m, n), out_dtype)],
        compiler_params=_params(("parallel", "arbitrary"), 48),
    )(x, g.reshape(1, k), w, col_scale)


def _mm_res_kernel(x_ref, w_ref, r_ref, o_ref):
    o_ref[...] = r_ref[...] + jnp.dot(x_ref[...], w_ref[...], preferred_element_type=F32)


def matmul_residual(x, w, r, tm=1024, tn=512):
    m, k = x.shape
    n = w.shape[1]
    return pl.pallas_call(
        _mm_res_kernel,
        grid=(m // tm, n // tn),
        in_specs=[pl.BlockSpec((tm, k), lambda i, j: (i, 0)), pl.BlockSpec((k, tn), lambda i, j: (0, j)),
                  pl.BlockSpec((tm, tn), lambda i, j: (i, j))],
        out_specs=pl.BlockSpec((tm, tn), lambda i, j: (i, j)),
        out_shape=jax.ShapeDtypeStruct((m, n), F32),
        compiler_params=_params(("parallel", "arbitrary"), 40),
    )(x, w, r)


def _misc_proj_kernel(x_ref, w_ref, ikz_ref, misc_ref):
    acc = jnp.dot(x_ref[...], w_ref[...], preferred_element_type=F32)
    ikz_ref[...] = acc[:, :2 * LANES].astype(BF16)
    misc_ref[...] = acc[:, 2 * LANES:]


def misc_proj(xn, w, tm=1024):
    m, k = xn.shape
    n = w.shape[1]
    return pl.pallas_call(
        _misc_proj_kernel,
        grid=(m // tm,),
        in_specs=[pl.BlockSpec((tm, k), lambda i: (i, 0)), pl.BlockSpec((k, n), lambda i: (0, 0))],
        out_specs=[pl.BlockSpec((tm, 2 * LANES), lambda i: (i, 0)), pl.BlockSpec((tm, LANES), lambda i: (i, 0))],
        out_shape=[jax.ShapeDtypeStruct((m, 2 * LANES), BF16), jax.ShapeDtypeStruct((m, LANES), F32)],
        compiler_params=_params(("parallel",), 40),
    )(xn, w)


def _ffn_kernel(x_ref, g_ref, wg_ref, wu_ref, wo_ref, o_ref, xn_ref):
    j = pl.program_id(1)

    @pl.when(j == 0)
    def _():
        x = x_ref[...]
        xn_ref[...] = _rms(x, g_ref[...]).astype(BF16)
        o_ref[...] = x

    xn = xn_ref[...]
    g = jnp.dot(xn, wg_ref[...], preferred_element_type=F32)
    u = jnp.dot(xn, wu_ref[...], preferred_element_type=F32)
    a = (g * jax.nn.sigmoid(g) * u * 0.5).astype(BF16)
    o_ref[...] += jnp.dot(a, wo_ref[...], preferred_element_type=F32)


def ffn_block(x, g, w_in_pad, w_out_pad, tm=1024, tf=512):
    m, d = x.shape
    nf = w_out_pad.shape[0] // tf
    return pl.pallas_call(
        _ffn_kernel,
        grid=(m // tm, nf),
        in_specs=[pl.BlockSpec((tm, d), lambda i, j: (i, 0)),
                  pl.BlockSpec((1, d), lambda i, j: (0, 0)),
                  pl.BlockSpec((d, tf), lambda i, j: (0, j)),
                  pl.BlockSpec((d, tf), lambda i, j: (0, j + nf)),
                  pl.BlockSpec((tf, d), lambda i, j: (j, 0))],
        out_specs=pl.BlockSpec((tm, d), lambda i, j: (i, 0)),
        out_shape=jax.ShapeDtypeStruct((m, d), F32),
        scratch_shapes=[pltpu.VMEM((tm, d), BF16)],
        compiler_params=_params(("parallel", "arbitrary"), 56),
    )(x, g.reshape(1, d), w_in_pad, w_in_pad, w_out_pad)


def _merge_kernel(xn_ref, ya_ref, yb_ref, yc_ref, yd_ref, wg_ref, bg_ref, wb_ref, o_ref):
    xn = xn_ref[...]
    acc = None
    for n, y_ref in enumerate((ya_ref, yb_ref, yc_ref, yd_ref)):
        gate = jnp.dot(xn, wg_ref[n], preferred_element_type=F32) + bg_ref[n]
        proj = jnp.dot(y_ref[...], wb_ref[n], preferred_element_type=F32)
        term = jax.nn.sigmoid(gate) * proj
        acc = term if acc is None else acc + term
    o_ref[...] = acc.astype(o_ref.dtype)


def gated_merge(xn, ys, w_gate, b_gate, w_branch, tm=1024, tn=256):
    m, d = xn.shape
    bw = ys[0].shape[1]
    y_spec = pl.BlockSpec((tm, bw), lambda i, j: (i, 0))
    return pl.pallas_call(
        _merge_kernel,
        grid=(m // tm, d // tn),
        in_specs=[pl.BlockSpec((tm, d), lambda i, j: (i, 0)), y_spec, y_spec, y_spec, y_spec,
                  pl.BlockSpec((N_BRANCH, d, tn), lambda i, j: (0, 0, j)),
                  pl.BlockSpec((N_BRANCH, 1, tn), lambda i, j: (0, 0, j)),
                  pl.BlockSpec((N_BRANCH, bw, tn), lambda i, j: (0, 0, j))],
        out_specs=pl.BlockSpec((tm, tn), lambda i, j: (i, j)),
        out_shape=jax.ShapeDtypeStruct((m, d), BF16),
        compiler_params=_params(("parallel", "arbitrary"), 48),
    )(xn, *ys, w_gate, b_gate.reshape(N_BRANCH, 1, d), w_branch)


def _s5_kernel(u_ref, perm_ref, permt_ref, bbd_ref, pwr_ref, pwi_ref, cbd_ref, d_ref, wglu_ref, o_ref,
               x_ref, st_ref, *, tt):
    n = S5_GROUPS * S5_STATE
    sl = tt // 8
    lc = 512

    @pl.when(pl.program_id(1) == 0)
    def _():
        st_ref[...] = jnp.zeros_like(st_ref)

    u = u_ref[...]
    u_perm = jnp.dot(perm_ref[...], u.astype(BF16), preferred_element_type=F32).astype(BF16)
    x_ref[...] = jnp.dot(u_perm, bbd_ref[...], preferred_element_type=F32)

    for c0 in range(0, n, lc):
        re = slice(c0, c0 + lc)
        im = slice(n + c0, n + c0 + lc)
        ar = pwr_ref[0:1, re]
        ai = pwi_ref[0:1, re]
        xr = jnp.zeros((8, lc), F32)
        xi = jnp.zeros((8, lc), F32)
        for j in range(sl):
            rows = slice(8 * j, 8 * j + 8)
            nr = ar * xr - ai * xi + x_ref[rows, re]
            ni = ar * xi + ai * xr + x_ref[rows, im]
            xr, xi = nr, ni
            x_ref[rows, re] = xr
            x_ref[rows, im] = xi

        alr = pwr_ref[sl - 1:sl, re]
        ali = pwi_ref[sl - 1:sl, re]
        cr = [st_ref[0:1, re]]
        ci = [st_ref[1:2, re]]
        for s in range(1, 9):
            pr, pi = cr[-1], ci[-1]
            cr.append(xr[s - 1:s] + alr * pr - ali * pi)
            ci.append(xi[s - 1:s] + alr * pi + ali * pr)
        st_ref[0:1, re] = cr[8]
        st_ref[1:2, re] = ci[8]
        car = jnp.concatenate(cr[:8], axis=0)
        cai = jnp.concatenate(ci[:8], axis=0)
        for j in range(sl):
            rows = slice(8 * j, 8 * j + 8)
            pr = pwr_ref[j:j + 1, re]
            pi = pwi_ref[j:j + 1, re]
            x_ref[rows, re] = x_ref[rows, re] + (pr * car - pi * cai)
            x_ref[rows, im] = x_ref[rows, im] + (pr * cai + pi * car)

    y_perm = jnp.dot(x_ref[...].astype(BF16), cbd_ref[...], preferred_element_type=F32)
    y = jnp.dot(permt_ref[...], y_perm, precision=lax.Precision.HIGHEST, preferred_element_type=F32) + d_ref[...] * u
    z = jax.nn.gelu(y)
    gate = jnp.dot(z.astype(BF16), wglu_ref[...], preferred_element_type=F32)
    o_ref[...] = (z * jax.nn.sigmoid(gate)).astype(o_ref.dtype)


S5_TILE = 256


def s5_branch(h_f32, col_block, bsz, seq, bbd, pw_re, pw_im, cbd, d_skip, w_glu, tt=S5_TILE):
    n = S5_GROUPS * S5_STATE
    nt = seq // tt
    sl = tt // 8
    row = jnp.arange(tt)
    perm = jax.nn.one_hot((row % 8) * sl + row // 8, tt, dtype=F32)
    const = lambda shape: pl.BlockSpec(shape, lambda b, t: (0, 0))
    return pl.pallas_call(
        functools.partial(_s5_kernel, tt=tt),
        grid=(bsz, nt),
        in_specs=[pl.BlockSpec((tt, S5_WIDTH), lambda b, t: (b * nt + t, col_block)),
                  const((tt, tt)), const((tt, tt)),
                  const((S5_WIDTH, 2 * n)), const((sl, n)), const((sl, n)), const((2 * n, S5_WIDTH)),
                  const((1, S5_WIDTH)), const((S5_WIDTH, S5_WIDTH))],
        out_specs=pl.BlockSpec((tt, S5_WIDTH), lambda b, t: (b * nt + t, 0)),
        out_shape=jax.ShapeDtypeStruct((bsz * seq, S5_WIDTH), BF16),
        scratch_shapes=[pltpu.VMEM((tt, 2 * n), F32), pltpu.VMEM((8, n), F32)],
        compiler_params=_params(("arbitrary", "arbitrary"), 48),
    )(h_f32, perm.astype(BF16), perm.T, bbd, pw_re, pw_im, cbd, d_skip, w_glu)


def s5_tables(a_re, a_im, log_dt, b_re, b_im, c_re, c_im, n_pow=S5_TILE // 8):
    g, p = a_re.shape
    dt = jnp.exp(log_dt)[:, None]
    mag = jnp.exp(a_re * dt)
    ab_re, ab_im = mag * jnp.cos(a_im * dt), mag * jnp.sin(a_im * dt)
    steps = jnp.arange(1, n_pow + 1, dtype=F32)[:, None, None]
    pw_mag = jnp.exp(a_re * dt * steps)
    pw_re = (pw_mag * jnp.cos(a_im * dt * steps)).reshape(n_pow, g * p)
    pw_im = (pw_mag * jnp.sin(a_im * dt * steps)).reshape(n_pow, g * p)
    den = a_re * a_re + a_im * a_im
    nr, ni = ab_re - 1.0, ab_im
    coef_re = (nr * a_re + ni * a_im) / den
    coef_im = (ni * a_re - nr * a_im) / den
    bb_re = coef_re[..., None] * b_re - coef_im[..., None] * b_im
    bb_im = coef_re[..., None] * b_im + coef_im[..., None] * b_re
    eye = jnp.eye(g, dtype=F32)
    to_bd_in = lambda t: jnp.einsum('gpc,gh->gchp', t, eye).reshape(g * S5_GROUP, g * p)
    to_bd_out = lambda t: jnp.einsum('gcp,gh->gphc', t, eye).reshape(g * p, g * S5_GROUP)
    bbd = jnp.concatenate([to_bd_in(bb_re), to_bd_in(bb_im)], axis=1).astype(BF16)
    cbd = jnp.concatenate([to_bd_out(c_re), to_bd_out(-c_im)], axis=0).astype(BF16)
    return bbd, pw_re, pw_im, cbd


def _log_sigmoid(x):
    return jnp.minimum(x, 0.0) - jnp.log1p(jnp.exp(-jnp.abs(x)))


def _mlstm_kernel(qk_ref, v_ref, og_ref, gate_ref, gbias_ref, convw_ref, o_ref,
                  xbuf_ref, c_ref, n_ref, m_ref, *, tt):
    L = MLSTM_L
    hd = MLSTM_DIM
    nh = MLSTM_HEADS
    i_col, f_col = MISC_I_COL, MISC_F_COL

    @pl.when(pl.program_id(1) == 0)
    def _():
        xbuf_ref[0:8, :] = jnp.zeros((8, 2 * nh * hd), F32)
        c_ref[...] = jnp.zeros_like(c_ref)
        n_ref[...] = jnp.zeros_like(n_ref)
        m_ref[...] = jnp.zeros_like(m_ref)

    xbuf_ref[8:8 + tt, :] = qk_ref[...]
    cw = convw_ref[...]
    conv = xbuf_ref[8:8 + tt, :] * cw[CONV_WIDTH - 1:CONV_WIDTH]
    for sh in range(1, CONV_WIDTH):
        conv = conv + xbuf_ref[8 - sh:8 - sh + tt, :] * cw[CONV_WIDTH - 1 - sh:CONV_WIDTH - sh]
    xbuf_ref[0:8, :] = xbuf_ref[tt:tt + 8, :]
    qk = conv * jax.nn.sigmoid(conv)

    gates = gate_ref[...] + gbias_ref[...]
    logf = _log_sigmoid(gates)
    row = lax.broadcasted_iota(jnp.int32, (L, L), 0)
    col = lax.broadcasted_iota(jnp.int32, (L, L), 1)
    causal = col <= row
    tri = jnp.where(causal, 1.0, 0.0).astype(F32)

    for c in range(tt // L):
        r0 = c * L
        gc = gates[r0:r0 + L]
        bcum = jnp.dot(tri, logf[r0:r0 + L], precision=lax.Precision.HIGHEST, preferred_element_type=F32)
        gct = gc.T
        bcumt = bcum.T
        for h in range(nh):
            ig_col = gc[:, i_col + h:i_col + h + 1]
            b_col = bcum[:, f_col + h:f_col + h + 1]
            ig_row = gct[i_col + h:i_col + h + 1, :]
            b_row = bcumt[f_col + h:f_col + h + 1, :]
            b_last = b_col[L - 1:L, :]
            q = qk[r0:r0 + L, h * hd:(h + 1) * hd]
            k = qk[r0:r0 + L, (nh + h) * hd:(nh + h + 1) * hd] * (hd ** -0.5)
            vb = v_ref[r0:r0 + L, h * hd:(h + 1) * hd].astype(BF16)
            qb = q.astype(BF16)
            kb = k.astype(BF16)
            c_prev = c_ref[h]
            n_prev = n_ref[h]
            m_prev = m_ref[h][:, 0:1]

            g = b_last - b_col + ig_col
            m_loc = jnp.max(g, axis=0, keepdims=True)
            wk = jnp.exp(g - m_loc) * k
            c_loc = jnp.dot(wk.T.astype(BF16), vb, preferred_element_type=F32)
            n_loc = jnp.sum(wk, axis=0, keepdims=True)

            m_inter = b_col + m_prev
            log_d = jnp.where(causal, b_col - b_row + ig_row, -jnp.inf)
            m_j = jnp.maximum(jnp.max(log_d, axis=1, keepdims=True), m_inter)
            dqk = jnp.exp(log_d - m_j) * lax.dot_general(qb, kb, NT_DIMS, preferred_element_type=F32)
            inter = jnp.exp(m_inter - m_j)
            num = (jnp.dot(dqk.astype(BF16), vb, preferred_element_type=F32)
                   + inter * jnp.dot(qb, c_prev.astype(BF16), preferred_element_type=F32))
            den = jnp.sum(dqk, axis=1, keepdims=True) + inter * jnp.sum(q * n_prev, axis=1, keepdims=True)
            hid = num / jnp.maximum(jnp.abs(den), jnp.exp(-m_j))
            og = jax.nn.sigmoid(og_ref[r0:r0 + L, h * hd:(h + 1) * hd])
            o_ref[r0:r0 + L, h * hd:(h + 1) * hd] = (og * hid).astype(o_ref.dtype)

            m_new = jnp.maximum(b_last + m_prev, m_loc)
            a = jnp.exp(b_last + m_prev - m_new)
            bcoef = jnp.exp(m_loc - m_new)
            c_ref[h] = a * c_prev + bcoef * c_loc
            n_ref[h] = a * n_prev + bcoef * n_loc
            m_ref[h] = jnp.broadcast_to(m_new, (1, LANES))


def mlstm_branch(h_f32, misc, gate_bias, conv_w, bsz, seq, qk_blk, v_blk, o_blk, tt=512):
    nh, hd = MLSTM_HEADS, MLSTM_DIM
    w = nh * hd
    nt = seq // tt
    return pl.pallas_call(
        functools.partial(_mlstm_kernel, tt=tt),
        grid=(bsz, nt),
        in_specs=[pl.BlockSpec((tt, 2 * w), lambda b, t: (b * nt + t, qk_blk)),
                  pl.BlockSpec((tt, w), lambda b, t: (b * nt + t, v_blk)),
                  pl.BlockSpec((tt, w), lambda b, t: (b * nt + t, o_blk)),
                  pl.BlockSpec((tt, LANES), lambda b, t: (b * nt + t, 0)),
                  pl.BlockSpec((1, LANES), lambda b, t: (0, 0)),
                  pl.BlockSpec((CONV_WIDTH, 2 * w), lambda b, t: (0, 0))],
        out_specs=pl.BlockSpec((tt, w), lambda b, t: (b * nt + t, 0)),
        out_shape=jax.ShapeDtypeStruct((bsz * seq, w), BF16),
        scratch_shapes=[pltpu.VMEM((tt + 8, 2 * w), F32), pltpu.VMEM((nh, hd, hd), F32),
                        pltpu.VMEM((nh, 1, hd), F32), pltpu.VMEM((nh, 1, LANES), F32)],
        compiler_params=_params(("arbitrary", "arbitrary"), 48),
    )(h_f32, h_f32, h_f32, misc, gate_bias, conv_w)


def rel_bucket(dist):
    max_exact = REL_BUCKETS // 2
    n = jnp.maximum(dist, 0)
    large = max_exact + (jnp.log(jnp.maximum(n, 1).astype(F32) / max_exact)
                         / math.log(REL_MAX_DIST / max_exact)
                         * (REL_BUCKETS - max_exact)).astype(jnp.int32)
    large = jnp.minimum(large, REL_BUCKETS - 1)
    return jnp.where(n < max_exact, n, large)


def bias_tiles(rel, t):
    assert t >= REL_MAX_DIST
    kk = jnp.arange(t)[:, None]
    qq = jnp.arange(t)[None, :]

    def lookup(dist):
        onehot = (rel_bucket(dist)[..., None] == jnp.arange(REL_BUCKETS)).astype(F32)
        return jnp.einsum('kqb,bh->hkq', onehot, rel.astype(F32), precision=lax.Precision.HIGHEST) * LOG2E

    t0 = jnp.where((qq >= kk)[None], lookup(qq - kk), NEG)
    t1 = lookup(t + qq - kk)
    t2 = lookup(jnp.full((t, t), 2 * t))
    return jnp.stack([t0, t1, t2])


def tri_pairs(nq):
    qi = [q for q in range(nq) for _ in range(q + 1)]
    kj = [k for q in range(nq) for k in range(q + 1)]
    return jnp.asarray(qi, jnp.int32), jnp.asarray(kj, jnp.int32)


def values_t_ext(h_bf16, col0, bsz, seq, heads, dv):
    v = h_bf16[:, col0:col0 + heads * dv].reshape(bsz, seq, heads, dv)
    vt = jnp.transpose(v, (0, 2, 3, 1))
    ones = jnp.ones((bsz, heads, V_ONES, seq), BF16)
    return jnp.concatenate([vt, ones], axis=2).reshape(bsz * heads * (dv + V_ONES), seq)


def _attn_maps(qi, kj, maps, s_ref, p_ref):
    def run(const_bias):
        def qk(i):
            load_k, load_q, _, bias_ref, head, mb_ref, _, _, _ = maps[i]
            s = lax.dot_general(load_k(), load_q(), NT_DIMS, preferred_element_type=F32)
            if not const_bias:
                s = s + bias_ref[0, head]
            if mb_ref is not None:
                s = s + mb_ref[...]
            s_ref[i % 2] = s

        qk(0)
        for i, (_, _, load_vt, bias_ref, head, _, m_ref, acc_ref, state_idx) in enumerate(maps):
            if i + 1 < len(maps):
                qk(i + 1)
            _softmax_pv(i % 2, state_idx, load_vt, bias_ref[0, head, 0:1, :] if const_bias else None,
                        s_ref, p_ref, m_ref, acc_ref)

    far = qi - kj >= 2

    @pl.when(far)
    def _():
        run(True)

    @pl.when(jnp.logical_not(far))
    def _():
        run(False)


def _softmax_pv(buf, idx, load_vt, bias_c, s_ref, p_ref, m_ref, acc_ref):
    tk, tq = s_ref.shape[1:]
    sb = s_ref.at[buf]
    pb = p_ref.at[buf]

    mx = [jnp.full((8, tq), -jnp.inf, F32) for _ in range(4)]
    for c in range(tk // 8):
        mx[c % 4] = jnp.maximum(mx[c % 4], sb[c * 8:(c + 1) * 8, :])
    m_tile = jnp.max(jnp.maximum(jnp.maximum(mx[0], mx[1]), jnp.maximum(mx[2], mx[3])), axis=0, keepdims=True)
    m_prev = m_ref[idx]
    if bias_c is not None:
        m_new = jnp.maximum(m_prev, m_tile + bias_c)
        m_b = jnp.broadcast_to(m_new - bias_c, (16, tq))
    else:
        m_new = jnp.maximum(m_prev, m_tile)
        m_b = jnp.broadcast_to(m_new, (16, tq))
    m_ref[idx] = m_new
    for c in range(tk // 16):
        pb[c * 16:(c + 1) * 16, :] = jnp.exp2(sb[c * 16:(c + 1) * 16, :] - m_b).astype(BF16)
    alpha = jnp.exp2(m_prev - m_new)
    acc_ref[idx] = alpha * acc_ref[idx] + jnp.dot(load_vt(), pb[...], preferred_element_type=F32)


def _init_softmax_state(m_ref, acc_ref):
    m_ref[...] = jnp.full(m_ref.shape, NEG, F32)
    acc_ref[...] = jnp.zeros_like(acc_ref)


def _fused_attn_kernel(qi_ref, kj_ref, qa_ref, ka_ref, vta_ref, biasa_ref, mask_ref,
                       qc_ref, kc_ref, vtc_ref, biasc_ref, lam_ref, gb_ref, oa_ref, oc_ref,
                       s_ref, p_ref, mb_ref, ma_ref, acca_ref, mc_ref, accc_ref, *, out_scale):
    p = pl.program_id(1)
    qi = qi_ref[p]
    kj = kj_ref[p]
    dh = HEAD_DIM
    dq = DIFF_QK_DIM
    dv = DIFF_V_DIM
    vrows_a = dh + V_ONES
    vrows_c = dv + V_ONES

    @pl.when(kj == 0)
    def _():
        _init_softmax_state(ma_ref, acca_ref)
        _init_softmax_state(mc_ref, accc_ref)

    mb_ref[...] = mask_ref[0].astype(F32)

    def load_qc(h, mi):
        qh = qc_ref[:, h * 2 * dq:(h + 1) * 2 * dq]
        lane = lax.broadcasted_iota(jnp.int32, qh.shape, 1)
        return jnp.where((lane < dq) if mi == 0 else (lane >= dq), qh, jnp.zeros_like(qh))

    maps = [(lambda h=h: ka_ref[:, h * dh:(h + 1) * dh], lambda h=h: qa_ref[:, h * dh:(h + 1) * dh],
             lambda h=h: vta_ref[h * vrows_a:(h + 1) * vrows_a, :], biasa_ref, h, mb_ref, ma_ref, acca_ref, h)
            for h in range(DSA_HEADS)]
    maps += [(lambda h=h: kc_ref[:, h * 2 * dq:(h + 1) * 2 * dq], functools.partial(load_qc, h, mi),
              lambda h=h: vtc_ref[h * vrows_c:(h + 1) * vrows_c, :], biasc_ref, h, None, mc_ref, accc_ref, 2 * h + mi)
             for h in range(DIFF_HEADS) for mi in range(2)]
    _attn_maps(qi, kj, maps, s_ref, p_ref)

    @pl.when(kj == qi)
    def _():
        for h in range(DSA_HEADS):
            a = acca_ref[h]
            oa_ref[:, h * dh:(h + 1) * dh] = (a[0:dh] / a[dh:dh + 1]).T.astype(oa_ref.dtype)
        lam = lam_ref[0:1, 0:1]
        for h in range(DIFF_HEADS):
            a1 = accc_ref[2 * h]
            a2 = accc_ref[2 * h + 1]
            a = a1[0:dv] / a1[dv:dv + 1] - lam * (a2[0:dv] / a2[dv:dv + 1])
            y = a * lax.rsqrt(jnp.mean(a * a, axis=0, keepdims=True) + EPS) * gb_ref[...] * out_scale
            oc_ref[:, h * dv:(h + 1) * dv] = y.T.astype(oc_ref.dtype)


def fused_attention(h_bf16, vt_a, vt_c, mask, bsz, seq, bias_a, bias_c, lam_row, subln_g, out_scale, t=512):
    nq = seq // t
    qi, kj = tri_pairs(nq)
    w = DSA_HEADS * HEAD_DIM
    vrows_a = DSA_HEADS * (HEAD_DIM + V_ONES)
    vrows_c = DIFF_HEADS * (DIFF_V_DIM + V_ONES)
    g_b = jnp.broadcast_to(subln_g.reshape(DIFF_V_DIM, 1), (DIFF_V_DIM, t))
    q_spec = lambda blk: pl.BlockSpec((t, w), lambda b, p, qi, kj: (b * nq + qi[p], blk))
    k_spec = lambda blk: pl.BlockSpec((t, w), lambda b, p, qi, kj: (b * nq + kj[p], blk))
    vt_spec = lambda rows: pl.BlockSpec((rows, t), lambda b, p, qi, kj: (b, kj[p]))
    bias_spec = lambda heads: pl.BlockSpec((1, heads, t, t),
                                           lambda b, p, qi, kj: (jnp.minimum(qi[p] - kj[p], 2), 0, 0, 0))
    out_spec = pl.BlockSpec((t, w), lambda b, p, qi, kj: (b * nq + qi[p], 0))
    grid_spec = pltpu.PrefetchScalarGridSpec(
        num_scalar_prefetch=2,
        grid=(bsz, qi.shape[0]),
        in_specs=[q_spec(0), k_spec(1), vt_spec(vrows_a), bias_spec(DSA_HEADS),
                  pl.BlockSpec((1, t, t), lambda b, p, qi, kj: (b, kj[p], qi[p])),
                  q_spec(4), k_spec(5), vt_spec(vrows_c), bias_spec(DIFF_HEADS),
                  pl.BlockSpec((1, LANES), lambda b, p, qi, kj: (0, 0)),
                  pl.BlockSpec((DIFF_V_DIM, t), lambda b, p, qi, kj: (0, 0))],
        out_specs=[out_spec, out_spec],
        scratch_shapes=[pltpu.VMEM((2, t, t), F32), pltpu.VMEM((2, t, t), BF16), pltpu.VMEM((t, t), F32),
                        pltpu.VMEM((DSA_HEADS, 1, t), F32), pltpu.VMEM((DSA_HEADS, HEAD_DIM + V_ONES, t), F32),
                        pltpu.VMEM((2 * DIFF_HEADS, 1, t), F32),
                        pltpu.VMEM((2 * DIFF_HEADS, DIFF_V_DIM + V_ONES, t), F32)],
    )
    return pl.pallas_call(
        functools.partial(_fused_attn_kernel, out_scale=out_scale),
        grid_spec=grid_spec,
        out_shape=[jax.ShapeDtypeStruct((bsz * seq, w), BF16), jax.ShapeDtypeStruct((bsz * seq, w), BF16)],
        compiler_params=_params(("arbitrary", "arbitrary"), 48),
    )(qi, kj, h_bf16, h_bf16, vt_a, bias_a, mask, h_bf16, h_bf16, vt_c, bias_c, lam_row, g_b)


def _sort_key(x):
    b = lax.bitcast_convert_type(x, jnp.int32)
    return b ^ ((b >> 31) & 0x7FFFFFFF)


def _batcher_pairs(n):
    def merge(lo, hi, r):
        step = r * 2
        if step < hi - lo:
            yield from merge(lo, hi, step)
            yield from merge(lo + r, hi, step)
            yield from [(a, a + r) for a in range(lo + r, hi - r, step)]
        else:
            yield (lo, lo + r)

    def sort(lo, hi):
        if hi - lo >= 1:
            mid = lo + (hi - lo) // 2
            yield from sort(lo, mid)
            yield from sort(mid + 1, hi)
            yield from merge(lo, hi, 1)

    return tuple(sort(0, n - 1))


SORT_GROUP = 16


def _count_ge_sorted(v, cand):
    g = len(v)
    out, masks, level = [], [], 1
    while (1 << level) <= g:
        pivots = [v[(2 * k + 1) * g // (1 << level) - 1] for k in range(1 << (level - 1))]

        def pick(lo, count, bit):
            if count == 1:
                return pivots[lo]
            half = count // 2
            return jnp.where(masks[bit], pick(lo, half, bit + 1), pick(lo + half, half, bit + 1))

        masks.append(pick(0, len(pivots), 0) >= cand)
        out.append((masks[-1], float(g >> level)))
        level += 1
    out.append((v[g - 1] >= cand, 1.0))
    return out


def _dsa_select_kernel(iq_ref, w_ref, ikz_ref, o_ref, key_ref, srt_ref, a_ref, wb_ref, *, tq, tk, topk, seq):
    i = pl.program_id(1)
    last = (i * tq + tq - 1) // tk
    idx_scale = (IDX_DIM * IDX_HEADS) ** -0.5
    iw_row = MISC_IW_COL
    grp = SORT_GROUP
    pairs = _batcher_pairs(grp)

    wt = w_ref[...].T
    for h in range(IDX_HEADS):
        wb_ref[h] = jnp.broadcast_to(wt[iw_row + h:iw_row + h + 1, :], (8, tq))

    def in_prefix(off, n):
        key_pos = off + lax.broadcasted_iota(jnp.int32, (n, tq), 0)
        q_pos = i * tq + lax.broadcasted_iota(jnp.int32, (n, tq), 1)
        return key_pos <= q_pos

    q_stack = [jnp.concatenate([iq_ref[:, (2 * pg) * LANES:(2 * pg + 1) * LANES],
                                iq_ref[:, (2 * pg + 1) * LANES:(2 * pg + 2) * LANES]], axis=0) for pg in range(2)]

    def head_dots(j, slot):
        off = pl.multiple_of(j * tk, tk)
        for half in range(2):
            kt = ikz_ref[pl.ds(off, tk), half * LANES:(half + 1) * LANES]
            for pg in range(2):
                a_ref[slot, 2 * half + pg] = lax.dot_general(kt, q_stack[pg], NT_DIMS, preferred_element_type=F32)

    def score_tile(j, slot, diagonal):
        off = pl.multiple_of(j * tk, tk)
        for c in range(tk // 8):
            r0 = c * 8
            sc = None
            for half in range(2):
                for pg in range(2):
                    for s in range(2):
                        head = 2 * (2 * pg + s) + half
                        term = wb_ref[head] * jnp.maximum(
                            a_ref[slot, 2 * half + pg, r0:r0 + 8, s * tq:(s + 1) * tq], 0.0)
                        sc = term if sc is None else sc + term
            sc = sc * idx_scale
            if diagonal:
                sc = jnp.where(in_prefix(off + r0, 8), sc, -jnp.inf)
            key_ref[pl.ds(pl.multiple_of(off + r0, 8), 8), :] = _sort_key(sc)
        for g in range(tk // (8 * grp)):
            base = off + g * grp * 8
            keys = [key_ref[pl.ds(pl.multiple_of(base + c * 8, 8), 8), :] for c in range(grp)]
            for lo, hi in pairs:
                keys[lo], keys[hi] = jnp.minimum(keys[lo], keys[hi]), jnp.maximum(keys[lo], keys[hi])
            for c in range(grp):
                srt_ref[pl.ds(pl.multiple_of(base + c * 8, 8), 8), :] = keys[c]

    def score_body(p, carry):
        head_dots(2 * p + 1, 1)
        score_tile(2 * p, 0, False)
        head_dots(2 * p + 2, 0)
        score_tile(2 * p + 1, 1, False)
        return carry

    head_dots(0, 0)
    lax.fori_loop(0, last // 2, score_body, 0)

    @pl.when(last % 2 == 0)
    def _():
        score_tile(last, 0, True)

    @pl.when(last % 2 == 1)
    def _():
        head_dots(last, 1)
        score_tile(last - 1, 0, False)
        score_tile(last, 1, True)

    n_pairs = (last + 2) // 2

    @pl.when(n_pairs * 2 > last + 1)
    def _():
        srt_ref[pl.ds(pl.multiple_of((last + 1) * tk, tk), tk), :] = jnp.full((tk, tq), INT_MIN, jnp.int32)

    n_terms = grp.bit_length()

    def count_ge(cand):
        cand_b = jnp.broadcast_to(cand, (8, tq))

        def count_body(j, accs):
            blk = srt_ref[pl.ds(pl.multiple_of(j * 2 * tk, 2 * tk), 2 * tk), :]
            accs = list(accs)
            for g in range(2 * tk // (8 * grp)):
                run = [blk[(g * grp + c) * 8:(g * grp + c + 1) * 8, :] for c in range(grp)]
                for t, (mask, _) in enumerate(_count_ge_sorted(run, cand_b)):
                    accs[t] = accs[t] + jnp.where(mask, 1.0, 0.0)
            return tuple(accs)

        zero = jnp.zeros((8, tq), F32)
        accs = lax.fori_loop(0, n_pairs, count_body, (zero,) * n_terms)
        weights = [float(grp >> (t + 1)) for t in range(n_terms - 1)] + [1.0]
        total = accs[0] * weights[0]
        for t in range(1, n_terms):
            total = total + accs[t] * weights[t]
        return jnp.sum(total, axis=0, keepdims=True)

    def bit_body(it, carry):
        thr, cnt_thr = carry
        cand = thr + lax.shift_left(jnp.int32(1), 31 - it)
        cnt = count_ge(cand)
        ok = cnt >= topk
        return jnp.where(ok, cand, thr), jnp.where(ok, cnt, cnt_thr)

    thr, cnt_thr = lax.fori_loop(0, 32, bit_body, (jnp.full((1, tq), INT_MIN, jnp.int32),
                                                   jnp.full((1, tq), 2.0 * tk, F32) * n_pairs.astype(F32)))
    thr_b = jnp.broadcast_to(thr, (16, tq))

    q_pos = i * tq + lax.broadcasted_iota(jnp.int32, (1, tq), 1)
    excess = jnp.where(q_pos + 1 > topk, cnt_thr - topk, 0.0)
    any_excess = jnp.max(excess) > 0.5

    def out_tile(j, diagonal):
        off = pl.multiple_of(j * tk, tk)
        for c in range(tk // 16):
            rows = pl.ds(pl.multiple_of(off + c * 16, 16), 16)
            keep = key_ref[rows, :] >= thr_b
            if diagonal:
                keep = keep & in_prefix(off + c * 16, 16)
            o_ref[0, rows, :] = jnp.where(keep, 0.0, NEG).astype(o_ref.dtype)

    def out_body(j, carry):
        out_tile(j, False)
        return carry

    @pl.when(jnp.logical_not(any_excess))
    def _():
        lax.fori_loop(0, last, out_body, 0)
        out_tile(last, True)

    @pl.when(any_excess)
    def _():
        ties_kept = topk - count_ge(thr + 1)
        sub = lax.broadcasted_iota(jnp.int32, (8, tq), 0)
        thr8 = jnp.broadcast_to(thr, (8, tq))

        def prefix8(x):
            for sh in (1, 2, 4):
                x = x + jnp.where(sub >= sh, pltpu.roll(x, sh, 0), 0.0)
            return x

        def out_tile_ties(j, diagonal, seen):
            off = pl.multiple_of(j * tk, tk)
            for c in range(tk // 16):
                halves = []
                for hh in range(2):
                    r0 = off + c * 16 + hh * 8
                    key = key_ref[pl.ds(pl.multiple_of(r0, 8), 8), :]
                    tie = key == thr8
                    rank = prefix8(jnp.where(tie, 1.0, 0.0)) + seen
                    keep = (key > thr8) | (tie & (rank <= ties_kept))
                    if diagonal:
                        keep = keep & in_prefix(r0, 8)
                    halves.append(jnp.where(keep, 0.0, NEG))
                    seen = rank[7:8, :]
                o_ref[0, pl.ds(pl.multiple_of(off + c * 16, 16), 16), :] = jnp.concatenate(
                    halves, axis=0).astype(o_ref.dtype)
            return seen

        seen = lax.fori_loop(0, last, lambda j, s: out_tile_ties(j, False, s), jnp.zeros((1, tq), F32))
        out_tile_ties(last, True, seen)

    def fill_body(j, carry):
        o_ref[0, pl.ds(pl.multiple_of(j * tk, tk), tk), :] = jnp.full((tk, tq), NEG, o_ref.dtype)
        return carry

    lax.fori_loop(last + 1, seq // tk, fill_body, 0)


def dsa_select(h_bf16, misc, ikz, bsz, seq, iq_blk, tq=128, tk=512):
    nq = seq // tq
    topk = min(TOPK_MAX, seq // 4)
    w = IDX_HEADS * IDX_DIM
    return pl.pallas_call(
        functools.partial(_dsa_select_kernel, tq=tq, tk=tk, topk=topk, seq=seq),
        grid=(bsz, nq),
        in_specs=[pl.BlockSpec((tq, w), lambda b, i: (b * nq + i, iq_blk)),
                  pl.BlockSpec((tq, LANES), lambda b, i: (b * nq + i, 0)),
                  pl.BlockSpec((seq, 2 * LANES), lambda b, i: (b, 0), pipeline_mode=pl.Buffered(1))],
        out_specs=pl.BlockSpec((1, seq, tq), lambda b, i: (b, 0, i)),
        out_shape=jax.ShapeDtypeStruct((bsz, seq, seq), BF16),
        scratch_shapes=[pltpu.VMEM((seq, tq), jnp.int32), pltpu.VMEM((seq, tq), jnp.int32),
                        pltpu.VMEM((2, IDX_HEADS // 2, tk, 2 * tq), F32), pltpu.VMEM((IDX_HEADS, 8, tq), F32)],
        compiler_params=_params(("arbitrary", "arbitrary"), 48),
    )(h_bf16, misc, ikz)


_SPLITS = (512, 512, 512, 512, 64, 8, 512, 512, 512, 512, 1024, 512, 512, 4, 4)
_NAMES = ('a_q', 'a_k', 'a_v', 'a_iq', 'a_ik', 'a_iw', 'b_u', 'c_q', 'c_k', 'c_v', 'd_qk', 'd_v', 'd_o', 'd_i', 'd_f')


def _split_w_in(w_in):
    out, off = {}, 0
    for name, width in zip(_NAMES, _SPLITS):
        out[name] = w_in[:, off:off + width]
        off += width
    return out


def _group_w_in(w_in):
    c = _split_w_in(w_in)
    d = w_in.shape[0]
    z = lambda n: jnp.zeros((d, n), w_in.dtype)
    w_attn = jnp.concatenate([c['a_q'], c['a_k'], c['a_v'], c['a_iq'], c['c_q'], c['c_k'], c['c_v']], axis=1)
    w_scan = jnp.concatenate([c['d_qk'], c['b_u'], c['d_v'], c['d_o']], axis=1)
    w_misc = jnp.concatenate([c['a_ik'], z(IDX_DIM), z(IDX_DIM), c['a_ik'],
                              z(MISC_IW_COL), c['a_iw'], c['d_i'], c['d_f'],
                              z(LANES - MISC_F_COL - MLSTM_HEADS)], axis=1)
    return w_attn.astype(BF16), w_scan.astype(BF16), w_misc.astype(BF16)


def _pad_ffn(w_in, w_out):
    d = w_in.shape[0]
    pad = D_FF_PAD - D_FF
    zi = jnp.zeros((d, pad), w_in.dtype)
    w_in_pad = jnp.concatenate([w_in[:, :D_FF], zi, w_in[:, D_FF:], zi], axis=1).astype(BF16)
    w_out_pad = jnp.concatenate([w_out, jnp.zeros((pad, d), w_out.dtype)], axis=0).astype(BF16)
    return w_in_pad, w_out_pad


def _branches(x, norm_g, l, bsz, seq, w_in, conv_w, i_bias, f_bias, s5, s5_d, s5_w_glu, diff_lambda,
              diff_subln_g, bias_a, bias_c):
    lambda_init = 0.8 - 0.6 * math.exp(-0.3 * l)
    w_attn, w_scan, w_misc = _group_w_in(w_in)
    ones = jnp.ones((512,), F32)
    attn_scale = jnp.concatenate([ones * (HEAD_DIM ** -0.5 * LOG2E), ones, ones, ones,
                                  ones * (DIFF_QK_DIM ** -0.5 * LOG2E), ones, ones]).reshape(1, -1)
    xn, h_attn = norm_matmul(x, norm_g, w_attn, attn_scale, BF16)
    h_scan = matmul(xn, w_scan, jnp.ones((1, w_scan.shape[1]), F32), F32)
    ikz, misc = misc_proj(xn, w_misc)

    mask = dsa_select(h_attn, misc, ikz, bsz, seq, iq_blk=3)
    vt_a = values_t_ext(h_attn, 2 * 512, bsz, seq, DSA_HEADS, HEAD_DIM)
    vt_c = values_t_ext(h_attn, 6 * 512, bsz, seq, DIFF_HEADS, DIFF_V_DIM)
    dl = diff_lambda.astype(F32)
    lam = jnp.exp(jnp.sum(dl[0] * dl[1])) - jnp.exp(jnp.sum(dl[2] * dl[3])) + lambda_init
    lam_row = jnp.full((1, LANES), lam, F32)
    y_a, y_c = fused_attention(h_attn, vt_a, vt_c, mask, bsz, seq, bias_a, bias_c, lam_row, diff_subln_g,
                               1.0 - lambda_init)

    bbd, abar_re, abar_im, cbd = s5
    y_b = s5_branch(h_scan, 2, bsz, seq, bbd, abar_re, abar_im, cbd, s5_d.reshape(1, -1), s5_w_glu.astype(BF16))

    gate_bias = (jnp.zeros((1, LANES), F32).at[0, MISC_I_COL:MISC_I_COL + MLSTM_HEADS].set(i_bias)
                 .at[0, MISC_F_COL:MISC_F_COL + MLSTM_HEADS].set(f_bias))
    y_d = mlstm_branch(h_scan, misc, gate_bias, conv_w, bsz, seq, qk_blk=0, v_blk=3, o_blk=4)
    return xn, (y_a, y_b, y_c, y_d)


def _mixer(x, l, bsz, seq, norm_g, w_in, conv_w, i_bias, f_bias, s5, s5_d, s5_w_glu, diff_lambda,
           diff_subln_g, bias_a, bias_c, w_gate, b_gate, w_branch, w_out):
    xn, ys = _branches(x, norm_g, l, bsz, seq, w_in, conv_w, i_bias, f_bias, s5, s5_d, s5_w_glu, diff_lambda,
                       diff_subln_g, bias_a, bias_c)
    merged = gated_merge(xn, ys, w_gate.astype(BF16), b_gate, w_branch.astype(BF16))
    return matmul_residual(merged, w_out.astype(BF16), x)


def kernel(x, norm_g, w_ffn_in, w_ffn_out, w_in, conv_w, mlstm_i_bias, mlstm_f_bias, s5_a_re, s5_a_im,
           s5_log_dt, s5_b_re, s5_b_im, s5_c_re, s5_c_im, s5_d, s5_w_glu, diff_lambda, diff_subln_g,
           rel_table, w_gate, b_gate, w_branch, w_out, final_g):
    bsz, seq, d = x.shape
    t_attn = 512
    bias_a = bias_tiles(rel_table[:, :DSA_HEADS], t_attn)
    bias_c = bias_tiles(rel_table[:, DSA_HEADS:], t_attn)
    xf = x.reshape(bsz * seq, d)
    for l in range(DEPTH):
        xf = ffn_block(xf, norm_g[l, 0], *_pad_ffn(w_ffn_in[l, 0], w_ffn_out[l, 0]))
        s5 = s5_tables(s5_a_re[l], s5_a_im[l], s5_log_dt[l], s5_b_re[l], s5_b_im[l], s5_c_re[l], s5_c_im[l])
        xf = _mixer(xf, l, bsz, seq, norm_g[l, 1], w_in[l], conv_w[l], mlstm_i_bias[l], mlstm_f_bias[l],
                    s5, s5_d[l], s5_w_glu[l], diff_lambda[l], diff_subln_g[l], bias_a, bias_c,
                    w_gate[l], b_gate[l], w_branch[l], w_out[l])
        xf = ffn_block(xf, norm_g[l, 2], *_pad_ffn(w_ffn_in[l, 1], w_ffn_out[l, 1]))
    return rmsnorm(xf, final_g, F32).reshape(bsz, seq, d)
```
